```python
import math
import jax, jax.numpy as jnp
from jax import lax
import numpy as np

D_MODEL = 2048
BATCH = 2
SEQ = 8192
DEPTH = 1

DA_HEADS = 8
DA_HEAD_DIM = 64
DA_V_DIM = 2 * DA_HEAD_DIM
NSA_HEADS = 16
NSA_KV_GROUPS = 4
NSA_HEADS_PER_GROUP = NSA_HEADS // NSA_KV_GROUPS
NSA_HEAD_DIM = 64
CMP_BLOCK = 32
CMP_STRIDE = 16
CMP_HIDDEN = 256
SLC_BLOCK = 64
SLC_TOPK = 16
N_LOCAL_BLOCKS = 2
WINDOW = 512
Q_BLOCK = 128
REL_BUCKETS = 32
REL_MAX_DIST = 128
REL_HEADS = DA_HEADS + NSA_HEADS
PEER_HEADS = 8
PEER_NKEYS = 128
PEER_EXPERTS = PEER_NKEYS * PEER_NKEYS
PEER_KEY_DIM = 128
PEER_TOPK = 16
PEER_CHUNK = 128
DN_ALPHA = (2 * DEPTH) ** 0.25
DN_BETA = (8 * DEPTH) ** -0.25
LN_EPS = 1e-5
NEG = -1e30
FORCE_SCORE = 1e4

DA_QK_W = DA_HEADS * 2 * DA_HEAD_DIM
DA_V_W = DA_HEADS * DA_V_DIM
NSA_Q_W = NSA_HEADS * NSA_HEAD_DIM
NSA_KV_W = NSA_KV_GROUPS * NSA_HEAD_DIM
NSA_GATE_W = 3 * NSA_HEADS
IN_SIZES = (DA_QK_W, DA_QK_W, DA_V_W, NSA_Q_W, NSA_KV_W, NSA_KV_W, NSA_KV_W, NSA_KV_W, NSA_KV_W, NSA_KV_W, NSA_GATE_W, D_MODEL, D_MODEL)
IN_IS_VALUE = (False, False, True, False, False, True, False, True, False, True, False, False, False)
IN_TOTAL = sum(IN_SIZES)

kernel_name = 'hybrid_diffattn_nsa_peer_deepnorm'


def layer_norm(x, g, b):
    xf = x.astype(jnp.float32)
    mu = jnp.mean(xf, -1, keepdims=True)
    var = jnp.mean(jnp.square(xf - mu), -1, keepdims=True)
    return ((xf - mu) * lax.rsqrt(var + LN_EPS)).astype(x.dtype) * g + b


def rms_norm(x, g):
    xf = x.astype(jnp.float32)
    return (xf * lax.rsqrt(jnp.mean(xf * xf, -1, keepdims=True) + LN_EPS)).astype(x.dtype) * g


def masked_softmax(logits, mask):
    logits = jnp.where(mask, logits.astype(jnp.float32), NEG)
    m = jnp.max(logits, -1, keepdims=True)
    p = jnp.where(mask, jnp.exp(logits - m), 0.0)
    return p / jnp.maximum(jnp.sum(p, -1, keepdims=True), 1e-30)


def rel_bucket(dist):
    n = jnp.maximum(dist, 0)
    max_exact = REL_BUCKETS // 2
    nf = jnp.maximum(n, 1).astype(jnp.float32)
    large = max_exact + (jnp.log(nf / max_exact) / math.log(REL_MAX_DIST / max_exact)
                         * (REL_BUCKETS - max_exact)).astype(jnp.int32)
    large = jnp.minimum(large, REL_BUCKETS - 1)
    return jnp.where(n < max_exact, n, large)


def split_columns(proj):
    outs, start = [], 0
    for w in IN_SIZES:
        outs.append(proj[..., start:start + w])
        start += w
    return outs


def diff_attention(q, k, v, lam, subln_g, lam_init, rel_table):
    B, S = q.shape[:2]
    nb = S // Q_BLOCK
    scale = DA_HEAD_DIM ** -0.5
    qb = q.reshape(B, nb, Q_BLOCK, DA_HEADS, 2, DA_HEAD_DIM).swapaxes(0, 1)
    k_pos = jnp.arange(S)
    table = rel_table.astype(jnp.float32)

    def block(args):
        qi, i = args
        q_pos = i * Q_BLOCK + jnp.arange(Q_BLOCK)
        dist = q_pos[:, None] - k_pos[None, :]
        bias = table[rel_bucket(dist)].transpose(2, 0, 1)
        logits = jnp.einsum('bqhmd,bkhmd->bhmqk', qi, k).astype(jnp.float32) * scale + bias[None, :, None]
        p = masked_softmax(logits, dist >= 0)
        w = p[:, :, 0] - lam * p[:, :, 1]
        return jnp.einsum('bhqk,bkhe->bqhe', w.astype(v.dtype), v)

    o = lax.map(block, (qb, jnp.arange(nb)))
    o = o.swapaxes(0, 1).reshape(B, S, DA_HEADS, DA_V_DIM)
    o = rms_norm(o, subln_g) * (1.0 - lam_init)
    return o.reshape(B, S, DA_V_W)


def compress_blocks(t, pe, w1, w2):
    B, S, G, d = t.shape
    r = CMP_BLOCK // CMP_STRIDE
    nc = S // CMP_STRIDE - r + 1
    ch = t.reshape(B, S // CMP_STRIDE, CMP_STRIDE, G, d)
    blocks = jnp.concatenate([ch[:, j:j + nc] for j in range(r)], axis=2) + pe[:, None, :]
    flat = blocks.transpose(0, 1, 3, 2, 4).reshape(B, nc, G, CMP_BLOCK * d)
    return jax.nn.gelu(flat @ w1) @ w2


def nsa_attention(q, kc, vc, ks, vs, kw, vw, gates, rel_table):
    B, S = q.shape[:2]
    G, HPG, d = NSA_KV_GROUPS, NSA_HEADS_PER_GROUP, NSA_HEAD_DIM
    nb = S // Q_BLOCK
    nc = kc.shape[1]
    ns = S // SLC_BLOCK
    k_sel = min(SLC_TOPK, ns)
    span = WINDOW + Q_BLOCK
    scale = d ** -0.5
    table = rel_table.astype(jnp.float32)
    table_g = table.reshape(REL_BUCKETS, G, HPG).transpose(1, 0, 2)
    cmp_start = jnp.arange(nc) * CMP_STRIDE
    cmp_end = cmp_start + CMP_BLOCK - 1
    slc_start = jnp.arange(ns) * SLC_BLOCK
    overlap = ((cmp_start[:, None] <= slc_start[None, :] + SLC_BLOCK - 1)
               & (cmp_end[:, None] >= slc_start[None, :])).astype(jnp.float32)
    ks_blk = ks.reshape(B, ns, SLC_BLOCK, G, d).transpose(0, 3, 1, 2, 4)
    vs_blk = vs.reshape(B, ns, SLC_BLOCK, G, d).transpose(0, 3, 1, 2, 4)
    kw_pad = jnp.pad(kw, ((0, 0), (WINDOW, 0), (0, 0), (0, 0)))
    vw_pad = jnp.pad(vw, ((0, 0), (WINDOW, 0), (0, 0), (0, 0)))
    gather = jax.vmap(jax.vmap(lambda tab, ix: tab[ix]))
    blk = jnp.arange(ns)
    g_idx = jnp.arange(G)[None, :, None, None, None]
    qb = q.reshape(B, nb, Q_BLOCK, NSA_HEADS, d).swapaxes(0, 1)
    gb = gates.reshape(B, nb, Q_BLOCK, NSA_HEADS, 3).swapaxes(0, 1)

    def block(args):
        qi, gi, i = args
        q0 = i * Q_BLOCK
        q_pos = q0 + jnp.arange(Q_BLOCK)
        qg = qi.reshape(B, Q_BLOCK, G, HPG, d)
        lg = jnp.einsum('bqghd,bcgd->bghqc', qg, kc).astype(jnp.float32) * scale
        p_cmp = masked_softmax(lg, cmp_end[None, :] <= q_pos[:, None])
        o_cmp = jnp.einsum('bghqc,bcgd->bqghd', p_cmp.astype(vc.dtype), vc)
        imp = jnp.einsum('bghqc,cs->bgqs', p_cmp, overlap)
        cur = q_pos // SLC_BLOCK
        valid = blk[None, :] <= cur[:, None]
        forced = valid & ((blk[None, :] == 0) | (blk[None, :] > cur[:, None] - N_LOCAL_BLOCKS))
        score = jnp.where(forced, FORCE_SCORE, jnp.where(valid, imp, NEG))
        _, idx = lax.top_k(score, k_sel)
        kg = gather(ks_blk, idx)
        vg = gather(vs_blk, idx)
        tok = idx[..., None] * SLC_BLOCK + jnp.arange(SLC_BLOCK)
        dist = q_pos[:, None, None] - tok
        smask = (dist >= 0).reshape(B, G, 1, Q_BLOCK, k_sel * SLC_BLOCK)
        sbias = table_g[g_idx, rel_bucket(dist)]
        sbias = jnp.moveaxis(sbias, -1, 2).reshape(B, G, HPG, Q_BLOCK, k_sel * SLC_BLOCK)
        lg = jnp.einsum('bqghd,bgqksd->bghqks', qg, kg).astype(jnp.float32)
        lg = lg.reshape(B, G, HPG, Q_BLOCK, k_sel * SLC_BLOCK) * scale + sbias
        p = masked_softmax(lg, smask)
        o_slc = jnp.einsum('bghqn,bgqnd->bqghd', p.astype(vg.dtype),
                           vg.reshape(B, G, Q_BLOCK, k_sel * SLC_BLOCK, d))
        kwi = lax.dynamic_slice_in_dim(kw_pad, q0, span, axis=1)
        vwi = lax.dynamic_slice_in_dim(vw_pad, q0, span, axis=1)
        k_pos = q0 - WINDOW + jnp.arange(span)
        wdist = q_pos[:, None] - k_pos[None, :]
        wmask = (wdist >= 0) & (wdist < WINDOW) & (k_pos[None, :] >= 0)
        wbias = table[rel_bucket(wdist)].reshape(Q_BLOCK, span, G, HPG).transpose(2, 3, 0, 1)
        lg = jnp.einsum('bqghd,blgd->bghql', qg, kwi).astype(jnp.float32) * scale + wbias
        p = masked_softmax(lg, wmask)
        o_win = jnp.einsum('bghql,blgd->bqghd', p.astype(vwi.dtype), vwi)
        g = jax.nn.sigmoid(gi).reshape(B, Q_BLOCK, G, HPG, 3)
        o = g[..., 0:1] * o_cmp + g[..., 1:2] * o_slc + g[..., 2:3] * o_win
        return o.reshape(B, Q_BLOCK, NSA_Q_W)

    o = lax.map(block, (qb, gb, jnp.arange(nb)))
    return o.swapaxes(0, 1).reshape(B, S, NSA_Q_W)


def peer_ffn(y, w_q, sub_k1, sub_k2, u_tab, v_tab):
    B, S, D = y.shape
    T = B * S
    n_chunks = T // PEER_CHUNK
    half = PEER_KEY_DIM // 2

    def chunk(yc):
        q = (yc @ w_q).reshape(PEER_CHUNK, PEER_HEADS, 2, half)
        s1 = jnp.einsum('chd,nd->chn', q[:, :, 0], sub_k1).astype(jnp.float32)
        s2 = jnp.einsum('chd,nd->chn', q[:, :, 1], sub_k2).astype(jnp.float32)
        v1, i1 = lax.top_k(s1, PEER_TOPK)
        v2, i2 = lax.top_k(s2, PEER_TOPK)
        cand = (v1[..., :, None] + v2[..., None, :]).reshape(PEER_CHUNK, PEER_HEADS, PEER_TOPK * PEER_TOPK)
        cid = (i1[..., :, None] * PEER_NKEYS + i2[..., None, :]).reshape(PEER_CHUNK, PEER_HEADS, PEER_TOPK * PEER_TOPK)
        sc, j = lax.top_k(cand, PEER_TOPK)
        eid = jnp.take_along_axis(cid, j, axis=-1)
        gate = jax.nn.softmax(sc, axis=-1)
        act = jax.nn.gelu(jnp.einsum('chkd,cd->chk', u_tab[eid], yc))
        return jnp.einsum('chk,chkd->cd', (gate * act).astype(yc.dtype), v_tab[eid])

    out = lax.map(chunk, y.reshape(n_chunks, PEER_CHUNK, D))
    return out.reshape(B, S, D)


def setup_inputs(seed: int = 0) -> dict:
    key = jax.random.key(seed)
    ks = jax.random.split(key, 26)
    f32 = jnp.float32

    def nrm(k, shape, s):
        return jax.random.normal(k, shape, f32) * s

    d = NSA_HEAD_DIM
    col_scale = jnp.concatenate([jnp.full((w,), DN_BETA if is_v else 1.0, f32)
                                 for w, is_v in zip(IN_SIZES, IN_IS_VALUE)])
    return {
        'x': nrm(ks[0], (BATCH, SEQ, D_MODEL), 1.0),
        'w_in': nrm(ks[1], (DEPTH, D_MODEL, IN_TOTAL), D_MODEL ** -0.5) * col_scale,
        'da_lam_q': nrm(ks[2], (DEPTH, 2, DA_HEAD_DIM), 0.1),
        'da_lam_k': nrm(ks[3], (DEPTH, 2, DA_HEAD_DIM), 0.1),
        'da_subln_g': 1.0 + nrm(ks[4], (DEPTH, DA_V_DIM), 0.02),
        'cmp_pe_k': nrm(ks[5], (DEPTH, CMP_BLOCK, d), 0.1),
        'cmp_w1_k': nrm(ks[6], (DEPTH, CMP_BLOCK * d, CMP_HIDDEN), (CMP_BLOCK * d) ** -0.5),
        'cmp_w2_k': nrm(ks[7], (DEPTH, CMP_HIDDEN, d), CMP_HIDDEN ** -0.5),
        'cmp_pe_v': nrm(ks[8], (DEPTH, CMP_BLOCK, d), 0.1),
        'cmp_w1_v': nrm(ks[9], (DEPTH, CMP_BLOCK * d, CMP_HIDDEN), (CMP_BLOCK * d) ** -0.5),
        'cmp_w2_v': nrm(ks[10], (DEPTH, CMP_HIDDEN, d), CMP_HIDDEN ** -0.5),
        'w_branch_da': nrm(ks[11], (DEPTH, DA_V_W, D_MODEL), DA_V_W ** -0.5),
        'w_branch_nsa': nrm(ks[12], (DEPTH, NSA_Q_W, D_MODEL), NSA_Q_W ** -0.5),
        'w_out': nrm(ks[13], (DEPTH, D_MODEL, D_MODEL), D_MODEL ** -0.5 * DN_BETA),
        'ln1_g': 1.0 + nrm(ks[14], (DEPTH, D_MODEL), 0.02),
        'ln1_b': nrm(ks[15], (DEPTH, D_MODEL), 0.02),
        'peer_wq': nrm(ks[16], (DEPTH, D_MODEL, PEER_HEADS * PEER_KEY_DIM), D_MODEL ** -0.5),
        'peer_subkey1': nrm(ks[17], (DEPTH, PEER_NKEYS, PEER_KEY_DIM // 2), (PEER_KEY_DIM // 2) ** -0.5),
        'peer_subkey2': nrm(ks[18], (DEPTH, PEER_NKEYS, PEER_KEY_DIM // 2), (PEER_KEY_DIM // 2) ** -0.5),
        'peer_u': nrm(ks[19], (DEPTH, PEER_EXPERTS, D_MODEL), D_MODEL ** -0.5),
        'peer_v': nrm(ks[20], (DEPTH, PEER_EXPERTS, D_MODEL), DN_BETA * PEER_HEADS ** -0.5),
        'ln2_g': 1.0 + nrm(ks[21], (DEPTH, D_MODEL), 0.02),
        'ln2_b': nrm(ks[22], (DEPTH, D_MODEL), 0.02),
        'rel_bias': nrm(ks[23], (REL_BUCKETS, REL_HEADS), 0.5),
    }


def reference(x, w_in, da_lam_q, da_lam_k, da_subln_g, cmp_pe_k, cmp_w1_k, cmp_w2_k,
              cmp_pe_v, cmp_w1_v, cmp_w2_v, w_branch_da, w_branch_nsa, w_out, ln1_g, ln1_b,
              peer_wq, peer_subkey1, peer_subkey2, peer_u, peer_v, ln2_g, ln2_b, rel_bias):
    B, S, _ = x.shape
    G, d = NSA_KV_GROUPS, NSA_HEAD_DIM
    rel_da = rel_bias[:, :DA_HEADS]
    rel_nsa = rel_bias[:, DA_HEADS:]
    for l in range(DEPTH):
        lam_init = 0.8 - 0.6 * math.exp(-0.3 * l)
        (qd, kd, vd, qn, kc, vc, ksl, vsl, kwn, vwn,
         g_nsa_br, g_da_merge, g_nsa_merge) = split_columns(x @ w_in[l])
        lam_e = jnp.exp(jnp.sum(da_lam_q[l].astype(jnp.float32) * da_lam_k[l].astype(jnp.float32), -1))
        lam = lam_e[0] - lam_e[1] + lam_init
        o_da = diff_attention(qd.reshape(B, S, DA_HEADS, 2, DA_HEAD_DIM),
                              kd.reshape(B, S, DA_HEADS, 2, DA_HEAD_DIM),
                              vd.reshape(B, S, DA_HEADS, DA_V_DIM),
                              lam, da_subln_g[l], lam_init, rel_da)
        kc_c = compress_blocks(kc.reshape(B, S, G, d), cmp_pe_k[l], cmp_w1_k[l], cmp_w2_k[l])
        vc_c = compress_blocks(vc.reshape(B, S, G, d), cmp_pe_v[l], cmp_w1_v[l], cmp_w2_v[l])
        o_nsa = nsa_attention(qn.reshape(B, S, NSA_HEADS, d), kc_c, vc_c,
                              ksl.reshape(B, S, G, d), vsl.reshape(B, S, G, d),
                              kwn.reshape(B, S, G, d), vwn.reshape(B, S, G, d),
                              g_nsa_br.reshape(B, S, NSA_HEADS, 3), rel_nsa)
        mixed = (jax.nn.sigmoid(g_da_merge) * (o_da @ w_branch_da[l])
                 + jax.nn.sigmoid(g_nsa_merge) * (o_nsa @ w_branch_nsa[l]))
        h = layer_norm(DN_ALPHA * x + mixed @ w_out[l], ln1_g[l], ln1_b[l])
        x = layer_norm(DN_ALPHA * h + peer_ffn(h, peer_wq[l], peer_subkey1[l], peer_subkey2[l],
                                              peer_u[l], peer_v[l]), ln2_g[l], ln2_b[l])
    return x
```

```python
import functools
import math

import jax
import jax.numpy as jnp
from jax import lax
from jax.experimental import pallas as pl
from jax.experimental.pallas import tpu as pltpu

F32 = jnp.float32
BF16 = jnp.bfloat16

DA_HEADS = 8
DA_HEAD_DIM = 64
DA_V_DIM = 128
NSA_HEADS = 16
NSA_GROUPS = 4
NSA_HPG = 4
NSA_DIM = 64
CMP_BLOCK = 32
CMP_STRIDE = 16
SLC_BLOCK = 64
SLC_TOPK = 16
N_LOCAL_BLOCKS = 2
WINDOW = 512
REL_BUCKETS = 32
REL_MAX_DIST = 128
PEER_HEADS = 8
PEER_NKEYS = 128
PEER_TOPK = 16
DEPTH = 1
DN_ALPHA = (2 * DEPTH) ** 0.25
LN_EPS = 1e-5
NEG = -1e30
FORCE_SCORE = 1e4

LANES = 128
VMEM_LIMIT = 48 * 1024 * 1024
ATT_TILE = 256
WIN_TQ = 256
CMP_TQ = 128
PEER_TT = 128


def _cparams(n_axes):
    return pltpu.CompilerParams(dimension_semantics=("arbitrary",) * n_axes,
                                vmem_limit_bytes=VMEM_LIMIT)


def _dot_nt(a, b):
    return lax.dot_general(a, b, (((1,), (1,)), ((), ())), preferred_element_type=F32)


def _dot_exact01(x, onehot_bf16):
    hi = x.astype(BF16)
    r1 = x - hi.astype(F32)
    mid = r1.astype(BF16)
    lo = (r1 - mid.astype(F32)).astype(BF16)
    d = lambda a: jnp.dot(a, onehot_bf16, preferred_element_type=F32)
    return d(hi) + d(mid) + d(lo)


def _mm_kernel(a_ref, b_ref, o_ref):
    o_ref[...] = jnp.dot(a_ref[...], b_ref[...], preferred_element_type=F32).astype(o_ref.dtype)


def _matmul(a, b, out_dtype, tm, tn):
    m, k = a.shape
    n = b.shape[1]
    return pl.pallas_call(
        _mm_kernel,
        grid=(n // tn, m // tm),
        in_specs=[pl.BlockSpec((tm, k), lambda j, i: (i, 0)),
                  pl.BlockSpec((k, tn), lambda j, i: (0, j))],
        out_specs=pl.BlockSpec((tm, tn), lambda j, i: (i, j)),
        out_shape=jax.ShapeDtypeStruct((m, n), out_dtype),
        compiler_params=_cparams(2),
        name="matmul",
    )(a, b)


def _rel_bucket(dist):
    n = jnp.maximum(dist, 0)
    max_exact = REL_BUCKETS // 2
    nf = jnp.maximum(n, 1).astype(F32)
    large = max_exact + (jnp.log(nf / max_exact) / math.log(REL_MAX_DIST / max_exact)
                         * (REL_BUCKETS - max_exact)).astype(jnp.int32)
    large = jnp.minimum(large, REL_BUCKETS - 1)
    return jnp.where(n < max_exact, n, large)


def _bias_tile(table, dist, mask):
    b = table[_rel_bucket(dist)]
    b = jnp.where(mask[..., None], b, NEG)
    return b.transpose(2, 0, 1)


def _causal_bias_tiles(table, t):
    i = jnp.arange(t)[:, None]
    j = jnp.arange(t)[None, :]
    d0 = i - j
    d1 = d0 + t
    return jnp.stack([_bias_tile(table, d0, d0 >= 0), _bias_tile(table, d1, d1 >= 0)], axis=1)


def _flash_init(m_scr, l_scr, acc_scr):
    m_scr[...] = jnp.full(m_scr.shape, NEG, F32)
    l_scr[...] = jnp.zeros(l_scr.shape, F32)
    acc_scr[...] = jnp.zeros(acc_scr.shape, F32)


def _flash_update(s, v, m_scr, l_scr, acc_scr):
    r, tq, tk = s.shape
    m_old = m_scr[...]
    m_new = jnp.maximum(m_old, jnp.max(s, axis=-1, keepdims=True))
    alpha = jnp.exp(m_old - m_new)
    p = jnp.exp(s - m_new)
    l_scr[...] = alpha * l_scr[...] + jnp.sum(p, axis=-1, keepdims=True)
    pv = jnp.dot(p.reshape(r * tq, tk).astype(BF16), v, preferred_element_type=F32)
    acc_scr[...] = alpha * acc_scr[...] + pv.reshape(r, tq, v.shape[-1])
    m_scr[...] = m_new


def _head_gates(gate_ref, group, branch):
    sig = jax.nn.sigmoid(gate_ref[...])
    lane = lax.broadcasted_iota(jnp.int32, (1, sig.shape[-1]), 1)
    cols = [jnp.sum(jnp.where(lane == (group * NSA_HPG + h) * 3 + branch, sig, 0.0), axis=-1, keepdims=True)
            for h in range(NSA_HPG)]
    return jnp.stack(cols)


def _da_kernel(lam_ref, cfar_ref, q_ref, k_ref, v_ref, bias_ref, g_ref, o_ref,
               m_scr, l_scr, acc_scr, *, t, lam_init):
    h = pl.program_id(1)
    qi = pl.program_id(2)
    q = q_ref[...]
    _flash_init(m_scr, l_scr, acc_scr)

    def tile(kj, bias):
        start = pl.multiple_of(kj * t, t)
        ks = k_ref[:, pl.ds(start, t), :]
        vs = v_ref[pl.ds(start, t), :]
        s = jnp.einsum("mqd,mkd->mqk", q, ks, preferred_element_type=F32) + bias
        _flash_update(s, vs, m_scr, l_scr, acc_scr)

    cfar = cfar_ref[h]

    def far_body(kj, carry):
        tile(kj, cfar)
        return carry

    lax.fori_loop(0, jnp.maximum(qi - 1, 0), far_body, 0)

    @pl.when(qi >= 1)
    def _():
        tile(qi - 1, bias_ref[1][None])

    tile(qi, bias_ref[0][None])

    acc = acc_scr[...]
    l = l_scr[...]
    o = acc[0] / l[0] - lam_ref[0] * (acc[1] / l[1])
    ms = jnp.mean(o * o, axis=-1, keepdims=True)
    o = (o * lax.rsqrt(ms + LN_EPS)) * g_ref[...] * (1.0 - lam_init)
    o_ref[...] = o.astype(o_ref.dtype)


def _diff_attention(q, k, v, lam, cfar, bias, subln_g, batch, seq, lam_init):
    t = ATT_TILE
    nq = seq // t
    kern = functools.partial(_da_kernel, t=t, lam_init=lam_init)
    smem = pl.BlockSpec(memory_space=pltpu.SMEM)
    return pl.pallas_call(
        kern,
        grid=(batch, DA_HEADS, nq),
        in_specs=[smem, smem,
                  pl.BlockSpec((None, 2, t, DA_HEAD_DIM), lambda b, h, i: (h, 0, b * nq + i, 0)),
                  pl.BlockSpec((None, 2, seq, DA_HEAD_DIM), lambda b, h, i: (h, 0, b, 0)),
                  pl.BlockSpec((seq, DA_V_DIM), lambda b, h, i: (b, h)),
                  pl.BlockSpec((None, 2, t, t), lambda b, h, i: (h, 0, 0, 0)),
                  pl.BlockSpec((1, DA_V_DIM), lambda b, h, i: (0, 0))],
        out_specs=pl.BlockSpec((t, DA_V_DIM), lambda b, h, i: (b * nq + i, h)),
        out_shape=jax.ShapeDtypeStruct((batch * seq, DA_HEADS * DA_V_DIM), BF16),
        scratch_shapes=[pltpu.VMEM((2, t, 1), F32), pltpu.VMEM((2, t, 1), F32),
                        pltpu.VMEM((2, t, DA_V_DIM), F32)],
        compiler_params=_cparams(3),
        name="diff_attention",
    )(lam, cfar, q, k, v, bias, subln_g)


def _compress_kernel(x_ref, pe_ref, w1_ref, w2_ref, o_ref):
    x = (x_ref[...].astype(F32) + pe_ref[...]).astype(BF16)
    hid = jax.nn.gelu(jnp.dot(x, w1_ref[...], preferred_element_type=F32))
    o_ref[...] = jnp.dot(hid.astype(BF16), w2_ref[...], preferred_element_type=F32).astype(o_ref.dtype)


def _compress(t, pe, w1, w2, batch, seq):
    g, d = NSA_GROUPS, NSA_DIM
    r = CMP_BLOCK // CMP_STRIDE
    nch = seq // CMP_STRIDE
    nc = nch - r + 1
    ch = t.reshape(batch, nch, CMP_STRIDE, g, d)
    blocks = jnp.concatenate([ch[:, j:j + nc] for j in range(r)], axis=2)
    flat = blocks.transpose(0, 1, 3, 2, 4).reshape(batch * nc * g, CMP_BLOCK * d)
    rows = flat.shape[0]
    tm = 512
    rows_p = -(-rows // tm) * tm
    flat = jnp.pad(flat, ((0, rows_p - rows), (0, 0)))
    hidden = w1.shape[1]
    out = pl.pallas_call(
        _compress_kernel,
        grid=(rows_p // tm,),
        in_specs=[pl.BlockSpec((tm, CMP_BLOCK * d), lambda i: (i, 0)),
                  pl.BlockSpec((1, CMP_BLOCK * d), lambda i: (0, 0)),
                  pl.BlockSpec((CMP_BLOCK * d, hidden), lambda i: (0, 0)),
                  pl.BlockSpec((hidden, d), lambda i: (0, 0))],
        out_specs=pl.BlockSpec((tm, d), lambda i: (i, 0)),
        out_shape=jax.ShapeDtypeStruct((rows_p, d), BF16),
        compiler_params=_cparams(1),
        name="compress_mlp",
    )(flat, pe.reshape(1, CMP_BLOCK * d), w1.astype(BF16), w2.astype(BF16))
    out = out[:rows].reshape(batch, nc, g, d).transpose(2, 0, 1, 3)
    return jnp.pad(out, ((0, 0), (0, 0), (0, nch - nc), (0, 0)))


def _cmp_select_kernel(q_ref, kc_ref, vc_ref, ov_ref, gate_ref, o_ref, sel_ref, *, tq, k_sel):
    qi = pl.program_id(2)
    ncp = kc_ref.shape[0]
    nsp = sel_ref.shape[-1]
    q = q_ref[...].reshape(NSA_HPG * tq, NSA_DIM)
    q_pos = qi * tq + lax.broadcasted_iota(jnp.int32, (tq, 1), 0)
    c_end = lax.broadcasted_iota(jnp.int32, (1, ncp), 1) * CMP_STRIDE + (CMP_BLOCK - 1)
    cmask = (c_end <= q_pos)[None]
    s = _dot_nt(q, kc_ref[...]).reshape(NSA_HPG, tq, ncp)
    s = jnp.where(cmask, s, NEG)
    m = jnp.max(s, axis=-1, keepdims=True)
    p = jnp.where(cmask, jnp.exp(s - m), 0.0)
    p = p / jnp.maximum(jnp.sum(p, axis=-1, keepdims=True), 1e-30)
    o = jnp.dot(p.reshape(NSA_HPG * tq, ncp).astype(BF16), vc_ref[...], preferred_element_type=F32)
    o_ref[...] = o.reshape(NSA_HPG, tq, NSA_DIM) * _head_gates(gate_ref, pl.program_id(0), 0)

    psum = p[0] + p[1] + p[2] + p[3]
    imp = _dot_exact01(psum, ov_ref[...])
    blk = lax.broadcasted_iota(jnp.int32, (1, nsp), 1)
    cur = q_pos // SLC_BLOCK
    valid = blk <= cur
    forced = valid & ((blk == 0) | (blk > cur - N_LOCAL_BLOCKS))
    score = jnp.where(forced, FORCE_SCORE, jnp.where(valid, imp, NEG))
    blk_f = blk.astype(F32)
    sel = jnp.zeros((tq, nsp), F32)
    for _ in range(k_sel):
        mx = jnp.max(score, axis=-1, keepdims=True)
        idx = jnp.min(jnp.where(score == mx, blk_f, float(nsp)), axis=-1, keepdims=True)
        hit = blk_f == idx
        sel = jnp.where(hit, 1.0, sel)
        score = jnp.where(hit, -jnp.inf, score)
    sel_ref[...] = sel.astype(sel_ref.dtype)


def _cmp_select(q, kc, vc, overlap, gate, batch, seq, k_sel):
    tq = CMP_TQ
    nq = seq // tq
    ncp = kc.shape[2]
    nsp = overlap.shape[1]
    kern = functools.partial(_cmp_select_kernel, tq=tq, k_sel=k_sel)
    return pl.pallas_call(
        kern,
        grid=(NSA_GROUPS, batch, nq),
        in_specs=[pl.BlockSpec((NSA_HPG, tq, NSA_DIM), lambda g, b, i: (g, b * nq + i, 0)),
                  pl.BlockSpec((None, None, ncp, NSA_DIM), lambda g, b, i: (g, b, 0, 0)),
                  pl.BlockSpec((None, None, ncp, NSA_DIM), lambda g, b, i: (g, b, 0, 0)),
                  pl.BlockSpec((ncp, nsp), lambda g, b, i: (0, 0)),
                  pl.BlockSpec((tq, gate.shape[1]), lambda g, b, i: (b * nq + i, 0))],
        out_specs=[pl.BlockSpec((NSA_HPG, tq, NSA_DIM), lambda g, b, i: (g, b * nq + i, 0)),
                   pl.BlockSpec((None, tq, nsp), lambda g, b, i: (g, b * nq + i, 0))],
        out_shape=[jax.ShapeDtypeStruct((NSA_HEADS, batch * seq, NSA_DIM), F32),
                   jax.ShapeDtypeStruct((NSA_GROUPS, batch * seq, nsp), BF16)],
        compiler_params=_cparams(3),
        name="nsa_cmp_select",
    )(q, kc, vc, overlap, gate)


def _window_kernel(q_ref, k_ref, v_ref, bias_ref, gate_ref, o_ref, *, tq, span):
    qi = pl.program_id(2)
    q0 = pl.multiple_of(qi * tq, tq)
    q = q_ref[...].reshape(NSA_HPG * tq, NSA_DIM)
    ks = k_ref[pl.ds(q0, span), :]
    vs = v_ref[pl.ds(q0, span), :]
    s = _dot_nt(q, ks).reshape(NSA_HPG, tq, span) + bias_ref[...]
    in_seq = (lax.broadcasted_iota(jnp.int32, (1, 1, span), 2) + q0) >= WINDOW
    s = jnp.where(in_seq, s, NEG)
    m = jnp.max(s, axis=-1, keepdims=True)
    p = jnp.exp(s - m)
    l = jnp.sum(p, axis=-1, keepdims=True)
    o = jnp.dot(p.reshape(NSA_HPG * tq, span).astype(BF16), vs, preferred_element_type=F32)
    o = o.reshape(NSA_HPG, tq, NSA_DIM) / l
    o_ref[...] = o * _head_gates(gate_ref, pl.program_id(0), 2)


def _window_attention(q, kp, vp, bias, gate, batch, seq):
    tq = WIN_TQ
    span = WINDOW + tq
    nq = seq // tq
    kern = functools.partial(_window_kernel, tq=tq, span=span)
    return pl.pallas_call(
        kern,
        grid=(NSA_GROUPS, batch, nq),
        in_specs=[pl.BlockSpec((NSA_HPG, tq, NSA_DIM), lambda g, b, i: (g, b * nq + i, 0)),
                  pl.BlockSpec((None, None, WINDOW + seq, NSA_DIM), lambda g, b, i: (g, b, 0, 0)),
                  pl.BlockSpec((None, None, WINDOW + seq, NSA_DIM), lambda g, b, i: (g, b, 0, 0)),
                  pl.BlockSpec((NSA_HPG, tq, span), lambda g, b, i: (g, 0, 0)),
                  pl.BlockSpec((tq, gate.shape[1]), lambda g, b, i: (b * nq + i, 0))],
        out_specs=pl.BlockSpec((NSA_HPG, tq, NSA_DIM), lambda g, b, i: (g, b * nq + i, 0)),
        out_shape=jax.ShapeDtypeStruct((NSA_HEADS, batch * seq, NSA_DIM), F32),
        compiler_params=_cparams(3),
        name="nsa_window",
    )(q, kp, vp, bias, gate)


def _slc_kernel(cfar_ref, q_ref, k_ref, v_ref, sel_ref, bias_ref, gate_ref, oc_ref, ow_ref, o_ref,
                m_scr, l_scr, acc_scr, *, t):
    g = pl.program_id(0)
    qi = pl.program_id(2)
    nsp = sel_ref.shape[-1]
    q = q_ref[...].reshape(NSA_HPG * t, NSA_DIM)
    sel = sel_ref[...]
    _flash_init(m_scr, l_scr, acc_scr)
    blk_row = lax.broadcasted_iota(jnp.int32, (nsp, t), 0)
    key_blk = lax.broadcasted_iota(jnp.int32, (nsp, t), 1) // SLC_BLOCK

    def tile(kj, bias):
        start = pl.multiple_of(kj * t, t)
        ks = k_ref[pl.ds(start, t), :]
        vs = v_ref[pl.ds(start, t), :]
        s = _dot_nt(q, ks).reshape(NSA_HPG, t, t) + bias
        expand = (blk_row == key_blk + kj * (t // SLC_BLOCK)).astype(BF16)
        chosen = jnp.dot(sel, expand, preferred_element_type=F32) > 0.5
        s = jnp.where(chosen[None], s, NEG)
        _flash_update(s, vs, m_scr, l_scr, acc_scr)

    head = lax.broadcasted_iota(jnp.int32, (NSA_HPG, 1, 1), 0)
    cfar = jnp.zeros((NSA_HPG, 1, 1), F32)
    for h in range(NSA_HPG):
        cfar = jnp.where(head == h, cfar_ref[g * NSA_HPG + h], cfar)

    def far_body(kj, carry):
        tile(kj, cfar)
        return carry

    lax.fori_loop(0, jnp.maximum(qi - 1, 0), far_body, 0)

    @pl.when(qi >= 1)
    def _():
        tile(qi - 1, bias_ref[:, 1])

    tile(qi, bias_ref[:, 0])

    o = acc_scr[...] / l_scr[...]
    o = oc_ref[...] + ow_ref[...] + o * _head_gates(gate_ref, g, 1)
    o_ref[...] = o.astype(o_ref.dtype)


def _slc_attention(q, ks, vs, sel, bias, cfar, gate, o_cmp, o_win, batch, seq):
    t = ATT_TILE
    nq = seq // t
    nsp = sel.shape[-1]
    kern = functools.partial(_slc_kernel, t=t)
    head_blk = lambda last: pl.BlockSpec((NSA_HPG, t, last), lambda g, b, i: (g, b * nq + i, 0))
    return pl.pallas_call(
        kern,
        grid=(NSA_GROUPS, batch, nq),
        in_specs=[pl.BlockSpec(memory_space=pltpu.SMEM),
                  head_blk(NSA_DIM),
                  pl.BlockSpec((None, None, seq, NSA_DIM), lambda g, b, i: (g, b, 0, 0)),
                  pl.BlockSpec((None, None, seq, NSA_DIM), lambda g, b, i: (g, b, 0, 0)),
                  pl.BlockSpec((None, t, nsp), lambda g, b, i: (g, b * nq + i, 0)),
                  pl.BlockSpec((NSA_HPG, 2, t, t), lambda g, b, i: (g, 0, 0, 0)),
                  pl.BlockSpec((t, gate.shape[1]), lambda g, b, i: (b * nq + i, 0)),
                  head_blk(NSA_DIM), head_blk(NSA_DIM)],
        out_specs=head_blk(NSA_DIM),
        out_shape=jax.ShapeDtypeStruct((NSA_HEADS, batch * seq, NSA_DIM), BF16),
        scratch_shapes=[pltpu.VMEM((NSA_HPG, t, 1), F32), pltpu.VMEM((NSA_HPG, t, 1), F32),
                        pltpu.VMEM((NSA_HPG, t, NSA_DIM), F32)],
        compiler_params=_cparams(3),
        name="nsa_selected",
    )(cfar, q, ks, vs, sel, bias, gate, o_cmp, o_win)


def _merge_kernel(oda_ref, onsa_ref, wda_ref, wnsa_ref, gda_ref, gnsa_ref, o_ref):
    a = jnp.dot(oda_ref[...], wda_ref[...], preferred_element_type=F32)
    n = jnp.dot(onsa_ref[...], wnsa_ref[...], preferred_element_type=F32)
    mixed = jax.nn.sigmoid(gda_ref[...]) * a + jax.nn.sigmoid(gnsa_ref[...]) * n
    o_ref[...] = mixed.astype(o_ref.dtype)


def _merge(o_da, o_nsa, w_da, w_nsa, gates):
    tokens, kd = o_da.shape
    d = w_da.shape[1]
    tm, tn = 512, 512
    nb = d // tn
    return pl.pallas_call(
        _merge_kernel,
        grid=(tokens // tm, nb),
        in_specs=[pl.BlockSpec((tm, kd), lambda i, j: (i, 0)),
                  pl.BlockSpec((tm, kd), lambda i, j: (i, 0)),
                  pl.BlockSpec((kd, tn), lambda i, j: (0, j)),
                  pl.BlockSpec((kd, tn), lambda i, j: (0, j)),
                  pl.BlockSpec((tm, tn), lambda i, j: (i, j)),
                  pl.BlockSpec((tm, tn), lambda i, j: (i, nb + j))],
        out_specs=pl.BlockSpec((tm, tn), lambda i, j: (i, j)),
        out_shape=jax.ShapeDtypeStruct((tokens, d), BF16),
        compiler_params=_cparams(2),
        name="gated_merge",
    )(o_da, o_nsa, w_da, w_nsa, gates, gates)


def _layer_norm(z, g, b):
    mu = jnp.mean(z, axis=-1, keepdims=True)
    zc = z - mu
    var = jnp.mean(zc * zc, axis=-1, keepdims=True)
    return (zc * lax.rsqrt(var + LN_EPS)) * g + b


def _outproj_ln_kernel(mixed_ref, w_ref, x_ref, g_ref, b_ref, h_ref, hb_ref):
    z = DN_ALPHA * x_ref[...] + jnp.dot(mixed_ref[...], w_ref[...], preferred_element_type=F32)
    h = _layer_norm(z, g_ref[...], b_ref[...])
    h_ref[...] = h
    hb_ref[...] = h.astype(hb_ref.dtype)


def _outproj_ln(mixed, w_out, x, g, b):
    tokens, d = x.shape
    tm = 256
    row = pl.BlockSpec((tm, d), lambda i: (i, 0))
    vec = pl.BlockSpec((1, d), lambda i: (0, 0))
    return pl.pallas_call(
        _outproj_ln_kernel,
        grid=(tokens // tm,),
        in_specs=[row, pl.BlockSpec((d, d), lambda i: (0, 0)), row, vec, vec],
        out_specs=[row, row],
        out_shape=[jax.ShapeDtypeStruct((tokens, d), F32), jax.ShapeDtypeStruct((tokens, d), BF16)],
        compiler_params=_cparams(1),
        name="outproj_ln1",
    )(mixed, w_out, x, g, b)


def _topk_rows(x, k):
    n, t = x.shape
    rows = lax.broadcasted_iota(jnp.int32, (n, t), 0).astype(F32)
    slot = lax.broadcasted_iota(jnp.int32, (k, t), 0)
    vals = jnp.zeros((k, t), F32)
    idxs = jnp.zeros((k, t), F32)
    for r in range(k):
        mx = jnp.max(x, axis=0, keepdims=True)
        idx = jnp.min(jnp.where(x == mx, rows, float(n)), axis=0, keepdims=True)
        vals = jnp.where(slot == r, mx, vals)
        idxs = jnp.where(slot == r, idx, idxs)
        x = jnp.where(rows == idx, -jnp.inf, x)
    return vals, idxs


def _peer_topk_kernel(q_ref, sk_ref, gate_ref, eid_ref, *, tm):
    k = PEER_TOPK
    nk = PEER_NKEYS
    for h in range(PEER_HEADS):
        qh = q_ref[:, h * LANES:(h + 1) * LANES]
        st = _dot_nt(sk_ref[...], qh)
        v1, i1 = _topk_rows(st[:nk], k)
        v2, i2 = _topk_rows(st[nk:], k)
        cand = jnp.concatenate([v1[a:a + 1] + v2 for a in range(k)], axis=0)
        cid = jnp.concatenate([i1[a:a + 1] * float(nk) + i2 for a in range(k)], axis=0)
        sc, j = _topk_rows(cand, k)
        rows = lax.broadcasted_iota(jnp.int32, (k * k, tm), 0).astype(F32)
        slot = lax.broadcasted_iota(jnp.int32, (k, tm), 0)
        eid = jnp.zeros((k, tm), F32)
        for r in range(k):
            picked = jnp.max(jnp.where(rows == j[r:r + 1], cid, -1.0), axis=0, keepdims=True)
            eid = jnp.where(slot == r, picked, eid)
        e = jnp.exp(sc - sc[0:1])
        gate_ref[h] = e / jnp.sum(e, axis=0, keepdims=True)
        eid_ref[h] = eid.astype(jnp.int32)


def _peer_topk(q, sk):
    tokens = q.shape[0]
    tm = 256
    kern = functools.partial(_peer_topk_kernel, tm=tm)
    out_blk = pl.BlockSpec((PEER_HEADS, PEER_TOPK, tm), lambda i: (0, 0, i))
    return pl.pallas_call(
        kern,
        grid=(tokens // tm,),
        in_specs=[pl.BlockSpec((tm, PEER_HEADS * LANES), lambda i: (i, 0)),
                  pl.BlockSpec((2 * PEER_NKEYS, LANES), lambda i: (0, 0))],
        out_specs=[out_blk, out_blk],
        out_shape=[jax.ShapeDtypeStruct((PEER_HEADS, PEER_TOPK, tokens), F32),
                   jax.ShapeDtypeStruct((PEER_HEADS, PEER_TOPK, tokens), jnp.int32)],
        compiler_params=_cparams(1),
        name="peer_topk",
    )(q, sk)


def _peer_gather_kernel(eid_hbm, uv_hbm, gate_ref, h_ref, g_ref, b_ref, o_ref,
                        idx_smem, rows, acc, idx_sem, row_sem, *, tt, d):
    i = pl.program_id(0)
    nsel = PEER_HEADS * PEER_TOPK

    idx_copy = pltpu.make_async_copy(eid_hbm.at[i], idx_smem, idx_sem)
    idx_copy.start()
    idx_copy.wait()

    def issue(tok, slot):
        for j in range(nsel):
            e = idx_smem[tok * nsel + j]
            pltpu.make_async_copy(uv_hbm.at[e], rows.at[slot, j], row_sem.at[slot]).start()

    def wait_slot(slot):
        pltpu.make_async_copy(rows.at[slot], rows.at[slot], row_sem.at[slot]).wait()

    issue(0, 0)
    lane = lax.broadcasted_iota(jnp.int32, (nsel, tt), 1)
    gates = gate_ref[...]

    def body(tok, carry):
        slot = tok % 2

        @pl.when(tok + 1 < tt)
        def _():
            issue(tok + 1, 1 - slot)

        wait_slot(slot)
        y = h_ref[pl.ds(tok, 1), :]
        part = rows[slot, :, 0:LANES] * y[:, 0:LANES]
        for c in range(1, d // LANES):
            part = part + rows[slot, :, c * LANES:(c + 1) * LANES] * y[:, c * LANES:(c + 1) * LANES]
        act = jax.nn.gelu(jnp.sum(part, axis=-1, keepdims=True))
        gcol = jnp.sum(jnp.where(lane == tok, gates, 0.0), axis=-1, keepdims=True)
        w = gcol * act
        acc[pl.ds(tok, 1), :] = jnp.sum(w * rows[slot, :, d:2 * d], axis=0, keepdims=True)
        return carry

    lax.fori_loop(0, tt, body, 0)
    z = DN_ALPHA * h_ref[...] + acc[...]
    o_ref[...] = _layer_norm(z, g_ref[...], b_ref[...])


def _peer_gather(eid, uv, gate, h, g, b):
    tokens, d = h.shape
    tt = PEER_TT
    nsel = PEER_HEADS * PEER_TOPK
    kern = functools.partial(_peer_gather_kernel, tt=tt, d=d)
    vec = pl.BlockSpec((1, d), lambda i: (0, 0))
    return pl.pallas_call(
        kern,
        grid=(tokens // tt,),
        in_specs=[pl.BlockSpec(memory_space=pl.ANY),
                  pl.BlockSpec(memory_space=pl.ANY),
                  pl.BlockSpec((nsel, tt), lambda i: (0, i)),
                  pl.BlockSpec((tt, d), lambda i: (i, 0)),
                  vec, vec],
        out_specs=pl.BlockSpec((tt, d), lambda i: (i, 0)),
        out_shape=jax.ShapeDtypeStruct((tokens, d), F32),
        scratch_shapes=[pltpu.SMEM((tt * nsel,), jnp.int32),
                        pltpu.VMEM((2, nsel, 2 * d), F32),
                        pltpu.VMEM((tt, d), F32),
                        pltpu.SemaphoreType.DMA,
                        pltpu.SemaphoreType.DMA((2,))],
        compiler_params=_cparams(1),
        name="peer_gather_ln2",
    )(eid, uv, gate, h, g, b)


def _heads_major(t, heads, width):
    return t.reshape(t.shape[0], heads, width).transpose(1, 0, 2)


def kernel(x, w_in, da_lam_q, da_lam_k, da_subln_g, cmp_pe_k, cmp_w1_k, cmp_w2_k, cmp_pe_v, cmp_w1_v,
           cmp_w2_v, w_branch_da, w_branch_nsa, w_out, ln1_g, ln1_b, peer_wq, peer_subkey1, peer_subkey2,
           peer_u, peer_v, ln2_g, ln2_b, rel_bias):
    batch, seq, d_model = x.shape
    tokens = batch * seq
    g, hd = NSA_GROUPS, NSA_DIM
    table = rel_bias.astype(F32)
    xs = x.reshape(tokens, d_model)
    for l in range(DEPTH):
        lam_init = 0.8 - 0.6 * math.exp(-0.3 * l)
        xb = xs.astype(BF16)
        w = w_in[l]
        scale = DA_HEAD_DIM ** -0.5
        c_daq, c_dak, c_dav, c_nq = 0, 1024, 2048, 3072
        c_kv, c_gate, c_mg, c_end = 4096, 5632, 5680, 9776
        w_att = jnp.concatenate([w[:, c_daq:c_dak] * scale, w[:, c_dak:c_nq], w[:, c_nq:c_kv] * scale],
                                axis=1).astype(BF16)
        att = _matmul(xb, w_att, BF16, 512, 512)
        kv = _matmul(xb, w[:, c_kv:c_gate].astype(BF16), F32, 512, 512)
        w_gate = jnp.pad(w[:, c_gate:c_mg], ((0, 0), (0, LANES - (c_mg - c_gate)))).astype(BF16)
        br_gate = _matmul(xb, w_gate, F32, 512, LANES)[:, :c_mg - c_gate]
        mg_gate = _matmul(xb, w[:, c_mg:c_end].astype(BF16), F32, 512, 512)

        lam_e = jnp.exp(jnp.sum(da_lam_q[l].astype(F32) * da_lam_k[l].astype(F32), -1))
        lam = (lam_e[0] - lam_e[1] + lam_init).reshape(1)
        q_da = att[:, 0:1024].reshape(tokens, DA_HEADS, 2, DA_HEAD_DIM).transpose(1, 2, 0, 3)
        k_da = att[:, 1024:2048].reshape(tokens, DA_HEADS, 2, DA_HEAD_DIM).transpose(1, 2, 0, 3)
        v_da = att[:, 2048:3072]
        bias_all = _causal_bias_tiles(table, ATT_TILE)
        cfar_all = table[REL_BUCKETS - 1]
        o_da = _diff_attention(q_da, k_da, v_da, lam, cfar_all[:DA_HEADS], bias_all[:DA_HEADS],
                               da_subln_g[l].reshape(1, DA_V_DIM), batch, seq, lam_init)

        q_n = _heads_major(att[:, 3072:4096], NSA_HEADS, hd)
        kv6 = kv.reshape(tokens, 6, g * hd)
        kc = _compress(kv6[:, 0], cmp_pe_k[l], cmp_w1_k[l], cmp_w2_k[l], batch, seq)
        vc = _compress(kv6[:, 1], cmp_pe_v[l], cmp_w1_v[l], cmp_w2_v[l], batch, seq)
        grp = lambda t: t.astype(BF16).reshape(batch, seq, g, hd).transpose(2, 0, 1, 3)
        k_s, v_s, k_w, v_w = grp(kv6[:, 2]), grp(kv6[:, 3]), grp(kv6[:, 4]), grp(kv6[:, 5])

        ncp = seq // CMP_STRIDE
        ns = seq // SLC_BLOCK
        nsp = -(-ns // LANES) * LANES
        k_sel = min(SLC_TOPK, ns)
        cmp_start = jnp.arange(ncp) * CMP_STRIDE
        slc_start = jnp.arange(nsp) * SLC_BLOCK
        overlap = ((cmp_start[:, None] <= slc_start[None, :] + SLC_BLOCK - 1)
                   & (cmp_start[:, None] + CMP_BLOCK - 1 >= slc_start[None, :])).astype(BF16)
        o_cmp, sel = _cmp_select(q_n, kc, vc, overlap, br_gate, batch, seq, k_sel)

        span = WINDOW + WIN_TQ
        wi = jnp.arange(WIN_TQ)[:, None]
        wj = jnp.arange(span)[None, :]
        wdist = wi - wj + WINDOW
        win_bias = _bias_tile(table[:, DA_HEADS:], wdist, (wdist >= 0) & (wdist < WINDOW))
        pad_w = lambda t: jnp.pad(t, ((0, 0), (0, 0), (WINDOW, 0), (0, 0)))
        o_win = _window_attention(q_n, pad_w(k_w), pad_w(v_w), win_bias, br_gate, batch, seq)

        o_nsa = _slc_attention(q_n, k_s, v_s, sel, bias_all[DA_HEADS:], cfar_all[DA_HEADS:], br_gate,
                               o_cmp, o_win, batch, seq)
        o_nsa = o_nsa.transpose(1, 0, 2).reshape(tokens, NSA_HEADS * hd)

        mixed = _merge(o_da, o_nsa, w_branch_da[l].astype(BF16), w_branch_nsa[l].astype(BF16), mg_gate)
        h, hb = _outproj_ln(mixed, w_out[l].astype(BF16), xs, ln1_g[l].reshape(1, -1), ln1_b[l].reshape(1, -1))

        pq = _matmul(hb, peer_wq[l].astype(BF16), BF16, 512, 512)
        half = peer_subkey1.shape[-1]
        zeros = jnp.zeros((PEER_NKEYS, half), F32)
        sk = jnp.concatenate([jnp.concatenate([peer_subkey1[l], zeros], axis=1),
                              jnp.concatenate([zeros, peer_subkey2[l]], axis=1)], axis=0).astype(BF16)
        gate, eid = _peer_topk(pq, sk)
        nsel = PEER_HEADS * PEER_TOPK
        eid_tok = eid.reshape(nsel, tokens).T.reshape(tokens // PEER_TT, PEER_TT * nsel)
        uv = jnp.concatenate([peer_u[l], peer_v[l]], axis=1)
        xs = _peer_gather(eid_tok, uv, gate.reshape(nsel, tokens), h,
                          ln2_g[l].reshape(1, -1), ln2_b[l].reshape(1, -1))
    return xs.reshape(batch, seq, d_model)
```

```python
import functools
import math

import jax
import jax.numpy as jnp
from jax import lax
from jax.experimental import pallas as pl
from jax.experimental.pallas import tpu as pltpu

F32 = jnp.float32
BF16 = jnp.bfloat16

DA_HEADS = 8
DA_HEAD_DIM = 64
DA_V_DIM = 128
NSA_HEADS = 16
NSA_GROUPS = 4
NSA_HPG = 4
NSA_DIM = 64
CMP_BLOCK = 32
CMP_STRIDE = 16
SLC_BLOCK = 64
SLC_TOPK = 16
N_LOCAL_BLOCKS = 2
WINDOW = 512
REL_BUCKETS = 32
REL_MAX_DIST = 128
PEER_HEADS = 8
PEER_NKEYS = 128
PEER_TOPK = 16
DEPTH = 1
DN_ALPHA = (2 * DEPTH) ** 0.25
LN_EPS = 1e-5
NEG = -1e30
FORCE_SCORE = 1e4

LANES = 128
VMEM_LIMIT = 48 * 1024 * 1024
DA_TILE, DA_WIDE = 512, 1024
SLC_TILE, SLC_WIDE = 256, 512
WIN_TQ = 256
CMP_TQ = 128
PEER_TT = 128
PEER_SLOTS = 8


def _cparams(n_axes):
    return pltpu.CompilerParams(dimension_semantics=("arbitrary",) * n_axes,
                                vmem_limit_bytes=VMEM_LIMIT)


def _dot_nt(a, b):
    return lax.dot_general(a, b, (((1,), (1,)), ((), ())), preferred_element_type=F32)


def _dot_exact01(x, onehot_bf16):
    hi = x.astype(BF16)
    r1 = x - hi.astype(F32)
    mid = r1.astype(BF16)
    lo = (r1 - mid.astype(F32)).astype(BF16)
    d = lambda a: jnp.dot(a, onehot_bf16, preferred_element_type=F32)
    return d(hi) + d(mid) + d(lo)


def _mm_kernel(a_ref, b_ref, o_ref):
    o_ref[...] = jnp.dot(a_ref[...], b_ref[...], preferred_element_type=F32).astype(o_ref.dtype)


def _matmul(a, b, out_dtype, tm, tn):
    m, k = a.shape
    n = b.shape[1]
    return pl.pallas_call(
        _mm_kernel,
        grid=(n // tn, m // tm),
        in_specs=[pl.BlockSpec((tm, k), lambda j, i: (i, 0)),
                  pl.BlockSpec((k, tn), lambda j, i: (0, j))],
        out_specs=pl.BlockSpec((tm, tn), lambda j, i: (i, j)),
        out_shape=jax.ShapeDtypeStruct((m, n), out_dtype),
        compiler_params=_cparams(2),
        name="matmul",
    )(a, b)


def _rel_bucket(dist):
    n = jnp.maximum(dist, 0)
    max_exact = REL_BUCKETS // 2
    nf = jnp.maximum(n, 1).astype(F32)
    large = max_exact + (jnp.log(nf / max_exact) / math.log(REL_MAX_DIST / max_exact)
                         * (REL_BUCKETS - max_exact)).astype(jnp.int32)
    large = jnp.minimum(large, REL_BUCKETS - 1)
    return jnp.where(n < max_exact, n, large)


def _toeplitz(rd, n_i, n_j):
    h, length = rd.shape
    rev = rd[:, ::-1]
    flat = jnp.broadcast_to(rev[:, None, :], (h, n_i, length)).reshape(h, n_i * length)
    skew = jnp.pad(flat, ((0, 0), (0, n_i))).reshape(h, n_i, length + 1)
    return skew[:, ::-1, :n_j]


def _bias_by_distance(table, n):
    return table[_rel_bucket(jnp.arange(n))].T


def _causal_bias_tiles(table, t):
    bd = _bias_by_distance(table, 2 * t)
    neg = jnp.full((bd.shape[0], t - 1), NEG, F32)
    diag = _toeplitz(jnp.concatenate([neg, bd[:, :t]], axis=1), t, t)
    off = _toeplitz(bd[:, 1:], t, t)
    return jnp.stack([diag, off], axis=1)


def _window_bias_tile(table, tq, span):
    bd = _bias_by_distance(table, WINDOW)
    h = bd.shape[0]
    lo = jnp.full((h, span - 1 - WINDOW), NEG, F32)
    hi = jnp.full((h, tq), NEG, F32)
    return _toeplitz(jnp.concatenate([lo, bd, hi], axis=1), tq, span)


def _flash_init(m_scr, acc_scr):
    m_scr[...] = jnp.full(m_scr.shape, NEG, F32)
    acc_scr[...] = jnp.zeros(acc_scr.shape, F32)


def _flash_update(s, v1, m_scr, acc_scr):
    m_old = m_scr[...]
    m_new = jnp.maximum(m_old, jnp.max(s, axis=-1, keepdims=True))
    alpha = jnp.exp(m_old - m_new)
    p = jnp.exp(s - m_new).astype(BF16)
    acc_scr[...] = alpha * acc_scr[...] + jnp.dot(p, v1, preferred_element_type=F32)
    m_scr[...] = m_new


def _causal_sweep(qi, t, wide, tile_fn):
    per = wide // t
    n_far = jnp.maximum(qi - 1, 0)
    n_wide = n_far // per

    def body(j, carry):
        tile_fn(pl.multiple_of(j * wide, wide), wide, "far")
        return carry

    lax.fori_loop(0, n_wide, body, 0)
    for r in range(per - 1):
        @pl.when(n_far - n_wide * per > r)
        def _():
            tile_fn(pl.multiple_of((n_wide * per + r) * t, t), t, "far")

    @pl.when(qi >= 1)
    def _():
        tile_fn(pl.multiple_of((qi - 1) * t, t), t, "off")

    tile_fn(pl.multiple_of(qi * t, t), t, "diag")


def _head_gates(gate_ref, group, branch):
    sig = jax.nn.sigmoid(gate_ref[...])
    lane = lax.broadcasted_iota(jnp.int32, (1, sig.shape[-1]), 1)
    cols = [jnp.sum(jnp.where(lane == (group * NSA_HPG + h) * 3 + branch, sig, 0.0), axis=-1, keepdims=True)
            for h in range(NSA_HPG)]
    return jnp.stack(cols)


def _da_kernel(lam_ref, q_ref, k_ref, v_ref, bias_ref, g_ref, o_ref, m_scr, acc_scr, *, t, wide, lam_init):
    qi = pl.program_id(2)
    q = q_ref[...]
    lane = lax.broadcasted_iota(jnp.int32, (1, LANES), 1)
    zero = jnp.zeros_like(q)
    q2 = jnp.concatenate([jnp.where(lane < DA_HEAD_DIM, q, zero), jnp.where(lane >= DA_HEAD_DIM, q, zero)], axis=0)
    _flash_init(m_scr, acc_scr)

    def tile(start, width, kind):
        ks = k_ref[pl.ds(start, width), :]
        v1 = jnp.concatenate([v_ref[pl.ds(start, width), :], jnp.ones((width, LANES), BF16)], axis=-1)
        s = _dot_nt(q2, ks)
        if kind != "far":
            s = (s.reshape(2, t, width) + bias_ref[0 if kind == "diag" else 1][None]).reshape(2 * t, width)
        _flash_update(s, v1, m_scr, acc_scr)

    _causal_sweep(qi, t, wide, tile)

    acc = acc_scr[...]
    o = acc[:t, :DA_V_DIM] / acc[:t, DA_V_DIM:] - lam_ref[0] * (acc[t:, :DA_V_DIM] / acc[t:, DA_V_DIM:])
    ms = jnp.mean(o * o, axis=-1, keepdims=True)
    o = (o * lax.rsqrt(ms + LN_EPS)) * g_ref[...] * (1.0 - lam_init)
    o_ref[...] = o.astype(o_ref.dtype)


def _diff_attention(att, lam, bias, subln_g, batch, seq, lam_init):
    t, wide = DA_TILE, DA_WIDE
    nq = seq // t
    kern = functools.partial(_da_kernel, t=t, wide=wide, lam_init=lam_init)
    return pl.pallas_call(
        kern,
        grid=(batch, DA_HEADS, nq),
        in_specs=[pl.BlockSpec(memory_space=pltpu.SMEM),
                  pl.BlockSpec((t, LANES), lambda b, h, i: (b * nq + i, h)),
                  pl.BlockSpec((seq, LANES), lambda b, h, i: (b, DA_HEADS + h)),
                  pl.BlockSpec((seq, LANES), lambda b, h, i: (b, 2 * DA_HEADS + h)),
                  pl.BlockSpec((None, 2, t, t), lambda b, h, i: (h, 0, 0, 0)),
                  pl.BlockSpec((1, DA_V_DIM), lambda b, h, i: (0, 0))],
        out_specs=pl.BlockSpec((t, DA_V_DIM), lambda b, h, i: (b * nq + i, h)),
        out_shape=jax.ShapeDtypeStruct((batch * seq, DA_HEADS * DA_V_DIM), BF16),
        scratch_shapes=[pltpu.VMEM((2 * t, 1), F32), pltpu.VMEM((2 * t, 2 * DA_V_DIM), F32)],
        compiler_params=_cparams(3),
        name="diff_attention",
    )(lam, att, att, att, bias, subln_g)


def _compress_kernel(x_ref, pe_ref, w1_ref, w2_ref, o_ref):
    x = (x_ref[...].astype(F32) + pe_ref[...]).astype(BF16)
    hid = jax.nn.gelu(jnp.dot(x, w1_ref[...], preferred_element_type=F32))
    o_ref[...] = jnp.dot(hid.astype(BF16), w2_ref[...], preferred_element_type=F32).astype(o_ref.dtype)


def _compress(t, pe, w1, w2, batch, seq):
    g, d = NSA_GROUPS, NSA_DIM
    r = CMP_BLOCK // CMP_STRIDE
    nch = seq // CMP_STRIDE
    nc = nch - r + 1
    ch = t.reshape(batch, nch, CMP_STRIDE, g, d)
    blocks = jnp.concatenate([ch[:, j:j + nc] for j in range(r)], axis=2)
    flat = blocks.transpose(0, 1, 3, 2, 4).reshape(batch * nc * g, CMP_BLOCK * d)
    rows = flat.shape[0]
    tm = 512
    rows_p = -(-rows // tm) * tm
    flat = jnp.pad(flat, ((0, rows_p - rows), (0, 0)))
    hidden = w1.shape[1]
    out = pl.pallas_call(
        _compress_kernel,
        grid=(rows_p // tm,),
        in_specs=[pl.BlockSpec((tm, CMP_BLOCK * d), lambda i: (i, 0)),
                  pl.BlockSpec((1, CMP_BLOCK * d), lambda i: (0, 0)),
                  pl.BlockSpec((CMP_BLOCK * d, hidden), lambda i: (0, 0)),
                  pl.BlockSpec((hidden, d), lambda i: (0, 0))],
        out_specs=pl.BlockSpec((tm, d), lambda i: (i, 0)),
        out_shape=jax.ShapeDtypeStruct((rows_p, d), BF16),
        compiler_params=_cparams(1),
        name="compress_mlp",
    )(flat, pe.reshape(1, CMP_BLOCK * d), w1.astype(BF16), w2.astype(BF16))
    out = out[:rows].reshape(batch, nc, g, d).transpose(2, 0, 1, 3)
    return jnp.pad(out, ((0, 0), (0, 0), (0, nch - nc), (0, 0)))


def _cmp_select_kernel(q_ref, kc_ref, vc_ref, ov_ref, gate_ref, o_ref, sel_ref, *, tq, k_sel):
    qi = pl.program_id(2)
    ncp = kc_ref.shape[0]
    nsp = sel_ref.shape[-1]
    q = q_ref[...].reshape(NSA_HPG * tq, NSA_DIM)
    q_pos = qi * tq + lax.broadcasted_iota(jnp.int32, (tq, 1), 0)
    c_end = lax.broadcasted_iota(jnp.int32, (1, ncp), 1) * CMP_STRIDE + (CMP_BLOCK - 1)
    cmask = (c_end <= q_pos)[None]
    s = _dot_nt(q, kc_ref[...]).reshape(NSA_HPG, tq, ncp)
    s = jnp.where(cmask, s, NEG)
    m = jnp.max(s, axis=-1, keepdims=True)
    p = jnp.where(cmask, jnp.exp(s - m), 0.0)
    p = p / jnp.maximum(jnp.sum(p, axis=-1, keepdims=True), 1e-30)
    o = jnp.dot(p.reshape(NSA_HPG * tq, ncp).astype(BF16), vc_ref[...], preferred_element_type=F32)
    o_ref[...] = o.reshape(NSA_HPG, tq, NSA_DIM) * _head_gates(gate_ref, pl.program_id(0), 0)

    psum = p[0] + p[1] + p[2] + p[3]
    imp = _dot_exact01(psum, ov_ref[...])
    blk = lax.broadcasted_iota(jnp.int32, (1, nsp), 1)
    cur = q_pos // SLC_BLOCK
    valid = blk <= cur
    forced = valid & ((blk == 0) | (blk > cur - N_LOCAL_BLOCKS))
    score = jnp.where(forced, FORCE_SCORE, jnp.where(valid, imp, NEG))
    blk_f = blk.astype(F32)
    sel = jnp.zeros((tq, nsp), F32)
    for _ in range(k_sel):
        mx = jnp.max(score, axis=-1, keepdims=True)
        idx = jnp.min(jnp.where(score == mx, blk_f, float(nsp)), axis=-1, keepdims=True)
        hit = blk_f == idx
        sel = jnp.where(hit, 1.0, sel)
        score = jnp.where(hit, -jnp.inf, score)
    sel_ref[...] = sel.astype(sel_ref.dtype)


def _cmp_select(q, kc, vc, overlap, gate, batch, seq, k_sel):
    tq = CMP_TQ
    nq = seq // tq
    ncp = kc.shape[2]
    nsp = overlap.shape[1]
    kern = functools.partial(_cmp_select_kernel, tq=tq, k_sel=k_sel)
    return pl.pallas_call(
        kern,
        grid=(NSA_GROUPS, batch, nq),
        in_specs=[pl.BlockSpec((NSA_HPG, tq, NSA_DIM), lambda g, b, i: (g, b * nq + i, 0)),
                  pl.BlockSpec((None, None, ncp, NSA_DIM), lambda g, b, i: (g, b, 0, 0)),
                  pl.BlockSpec((None, None, ncp, NSA_DIM), lambda g, b, i: (g, b, 0, 0)),
                  pl.BlockSpec((ncp, nsp), lambda g, b, i: (0, 0)),
                  pl.BlockSpec((tq, gate.shape[1]), lambda g, b, i: (b * nq + i, 0))],
        out_specs=[pl.BlockSpec((NSA_HPG, tq, NSA_DIM), lambda g, b, i: (g, b * nq + i, 0)),
                   pl.BlockSpec((None, tq, nsp), lambda g, b, i: (g, b * nq + i, 0))],
        out_shape=[jax.ShapeDtypeStruct((NSA_HEADS, batch * seq, NSA_DIM), F32),
                   jax.ShapeDtypeStruct((NSA_GROUPS, batch * seq, nsp), BF16)],
        compiler_params=_cparams(3),
        name="nsa_cmp_select",
    )(q, kc, vc, overlap, gate)


def _window_kernel(q_ref, k_ref, v_ref, bias_ref, gate_ref, o_ref, *, tq, span):
    qi = pl.program_id(2)
    q0 = pl.multiple_of(qi * tq, tq)
    q = q_ref[...].reshape(NSA_HPG * tq, NSA_DIM)
    ks = k_ref[pl.ds(q0, span), :]
    vs = v_ref[pl.ds(q0, span), :]
    s = _dot_nt(q, ks).reshape(NSA_HPG, tq, span) + bias_ref[...]
    in_seq = (lax.broadcasted_iota(jnp.int32, (1, 1, span), 2) + q0) >= WINDOW
    s = jnp.where(in_seq, s, NEG)
    m = jnp.max(s, axis=-1, keepdims=True)
    p = jnp.exp(s - m)
    l = jnp.sum(p, axis=-1, keepdims=True)
    o = jnp.dot(p.reshape(NSA_HPG * tq, span).astype(BF16), vs, preferred_element_type=F32)
    o = o.reshape(NSA_HPG, tq, NSA_DIM) / l
    o_ref[...] = o * _head_gates(gate_ref, pl.program_id(0), 2)


def _window_attention(q, kp, vp, bias, gate, batch, seq):
    tq = WIN_TQ
    span = WINDOW + tq
    nq = seq // tq
    kern = functools.partial(_window_kernel, tq=tq, span=span)
    return pl.pallas_call(
        kern,
        grid=(NSA_GROUPS, batch, nq),
        in_specs=[pl.BlockSpec((NSA_HPG, tq, NSA_DIM), lambda g, b, i: (g, b * nq + i, 0)),
                  pl.BlockSpec((None, None, WINDOW + seq, NSA_DIM), lambda g, b, i: (g, b, 0, 0)),
                  pl.BlockSpec((None, None, WINDOW + seq, NSA_DIM), lambda g, b, i: (g, b, 0, 0)),
                  pl.BlockSpec((NSA_HPG, tq, span), lambda g, b, i: (g, 0, 0)),
                  pl.BlockSpec((tq, gate.shape[1]), lambda g, b, i: (b * nq + i, 0))],
        out_specs=pl.BlockSpec((NSA_HPG, tq, NSA_DIM), lambda g, b, i: (g, b * nq + i, 0)),
        out_shape=jax.ShapeDtypeStruct((NSA_HEADS, batch * seq, NSA_DIM), F32),
        compiler_params=_cparams(3),
        name="nsa_window",
    )(q, kp, vp, bias, gate)


def _slc_kernel(q_ref, k_ref, v_ref, sel_ref, bias_ref, gate_ref, oc_ref, ow_ref, o_ref,
                m_scr, acc_scr, *, t, wide):
    g = pl.program_id(0)
    qi = pl.program_id(2)
    nsp = sel_ref.shape[-1]
    rows = NSA_HPG * t
    q = q_ref[...].reshape(rows, NSA_DIM)
    sel = sel_ref[...]
    _flash_init(m_scr, acc_scr)

    def tile(start, width, kind):
        ks = k_ref[pl.ds(start, width), :]
        v1 = v_ref[pl.ds(start, width), :]
        s = _dot_nt(q, ks).reshape(NSA_HPG, t, width)
        blk_row = lax.broadcasted_iota(jnp.int32, (nsp, width), 0)
        key_blk = lax.broadcasted_iota(jnp.int32, (nsp, width), 1) // SLC_BLOCK + start // SLC_BLOCK
        expand = jnp.where(blk_row == key_blk, 1.0, 0.0).astype(BF16)
        off = (jnp.dot(sel, expand, preferred_element_type=F32) - 1.0) * (-NEG)
        if kind != "far":
            off = off[None] + bias_ref[:, 0 if kind == "diag" else 1]
        else:
            off = off[None]
        _flash_update((s + off).reshape(rows, width), v1, m_scr, acc_scr)

    _causal_sweep(qi, t, wide, tile)

    acc = acc_scr[...]
    o = (acc[:, :NSA_DIM] / acc[:, NSA_DIM:]).reshape(NSA_HPG, t, NSA_DIM)
    o = oc_ref[...] + ow_ref[...] + o * _head_gates(gate_ref, g, 1)
    o_ref[...] = o.astype(o_ref.dtype)


def _slc_attention(q, ks, vs1, sel, bias, gate, o_cmp, o_win, batch, seq):
    t, wide = SLC_TILE, SLC_WIDE
    nq = seq // t
    nsp = sel.shape[-1]
    kern = functools.partial(_slc_kernel, t=t, wide=wide)
    head_blk = lambda last: pl.BlockSpec((NSA_HPG, t, last), lambda g, b, i: (g, b * nq + i, 0))
    return pl.pallas_call(
        kern,
        grid=(NSA_GROUPS, batch, nq),
        in_specs=[head_blk(NSA_DIM),
                  pl.BlockSpec((None, None, seq, NSA_DIM), lambda g, b, i: (g, b, 0, 0)),
                  pl.BlockSpec((None, None, seq, 2 * NSA_DIM), lambda g, b, i: (g, b, 0, 0)),
                  pl.BlockSpec((None, t, nsp), lambda g, b, i: (g, b * nq + i, 0)),
                  pl.BlockSpec((NSA_HPG, 2, t, t), lambda g, b, i: (g, 0, 0, 0)),
                  pl.BlockSpec((t, gate.shape[1]), lambda g, b, i: (b * nq + i, 0)),
                  head_blk(NSA_DIM), head_blk(NSA_DIM)],
        out_specs=head_blk(NSA_DIM),
        out_shape=jax.ShapeDtypeStruct((NSA_HEADS, batch * seq, NSA_DIM), BF16),
        scratch_shapes=[pltpu.VMEM((NSA_HPG * t, 1), F32), pltpu.VMEM((NSA_HPG * t, 2 * NSA_DIM), F32)],
        compiler_params=_cparams(3),
        name="nsa_selected",
    )(q, ks, vs1, sel, bias, gate, o_cmp, o_win)


def _merge_kernel(oda_ref, onsa_ref, wda_ref, wnsa_ref, gda_ref, gnsa_ref, o_ref):
    a = jnp.dot(oda_ref[...], wda_ref[...], preferred_element_type=F32)
    n = jnp.dot(onsa_ref[...], wnsa_ref[...], preferred_element_type=F32)
    mixed = jax.nn.sigmoid(gda_ref[...]) * a + jax.nn.sigmoid(gnsa_ref[...]) * n
    o_ref[...] = mixed.astype(o_ref.dtype)


def _merge(o_da, o_nsa, w_da, w_nsa, gates):
    tokens, kd = o_da.shape
    d = w_da.shape[1]
    tm, tn = 512, 512
    nb = d // tn
    return pl.pallas_call(
        _merge_kernel,
        grid=(tokens // tm, nb),
        in_specs=[pl.BlockSpec((tm, kd), lambda i, j: (i, 0)),
                  pl.BlockSpec((tm, kd), lambda i, j: (i, 0)),
                  pl.BlockSpec((kd, tn), lambda i, j: (0, j)),
                  pl.BlockSpec((kd, tn), lambda i, j: (0, j)),
                  pl.BlockSpec((tm, tn), lambda i, j: (i, j)),
                  pl.BlockSpec((tm, tn), lambda i, j: (i, nb + j))],
        out_specs=pl.BlockSpec((tm, tn), lambda i, j: (i, j)),
        out_shape=jax.ShapeDtypeStruct((tokens, d), BF16),
        compiler_params=_cparams(2),
        name="gated_merge",
    )(o_da, o_nsa, w_da, w_nsa, gates, gates)


def _layer_norm(z, g, b):
    mu = jnp.mean(z, axis=-1, keepdims=True)
    zc = z - mu
    var = jnp.mean(zc * zc, axis=-1, keepdims=True)
    return (zc * lax.rsqrt(var + LN_EPS)) * g + b


def _outproj_ln_kernel(mixed_ref, w_ref, x_ref, g_ref, b_ref, h_ref, hb_ref):
    z = DN_ALPHA * x_ref[...] + jnp.dot(mixed_ref[...], w_ref[...], preferred_element_type=F32)
    h = _layer_norm(z, g_ref[...], b_ref[...])
    h_ref[...] = h
    hb_ref[...] = h.astype(hb_ref.dtype)


def _outproj_ln(mixed, w_out, x, g, b):
    tokens, d = x.shape
    tm = 256
    row = pl.BlockSpec((tm, d), lambda i: (i, 0))
    vec = pl.BlockSpec((1, d), lambda i: (0, 0))
    return pl.pallas_call(
        _outproj_ln_kernel,
        grid=(tokens // tm,),
        in_specs=[row, pl.BlockSpec((d, d), lambda i: (0, 0)), row, vec, vec],
        out_specs=[row, row],
        out_shape=[jax.ShapeDtypeStruct((tokens, d), F32), jax.ShapeDtypeStruct((tokens, d), BF16)],
        compiler_params=_cparams(1),
        name="outproj_ln1",
    )(mixed, w_out, x, g, b)


def _topk_rows(x, k):
    n, t = x.shape
    rows = lax.broadcasted_iota(jnp.int32, (n, t), 0).astype(F32)
    slot = lax.broadcasted_iota(jnp.int32, (k, t), 0)
    vals = jnp.zeros((k, t), F32)
    idxs = jnp.zeros((k, t), F32)
    for r in range(k):
        mx = jnp.max(x, axis=0, keepdims=True)
        idx = jnp.min(jnp.where(x == mx, rows, float(n)), axis=0, keepdims=True)
        vals = jnp.where(slot == r, mx, vals)
        idxs = jnp.where(slot == r, idx, idxs)
        x = jnp.where(rows == idx, -jnp.inf, x)
    return vals, idxs


def _peer_topk_kernel(q_ref, sk_ref, gate_ref, eid_ref, *, tm):
    k = PEER_TOPK
    nk = PEER_NKEYS
    for h in range(PEER_HEADS):
        qh = q_ref[:, h * LANES:(h + 1) * LANES]
        st = _dot_nt(sk_ref[...], qh)
        v1, i1 = _topk_rows(st[:nk], k)
        v2, i2 = _topk_rows(st[nk:], k)
        cand = jnp.concatenate([v1[a:a + 1] + v2 for a in range(k)], axis=0)
        cid = jnp.concatenate([i1[a:a + 1] * float(nk) + i2 for a in range(k)], axis=0)
        sc, j = _topk_rows(cand, k)
        rows = lax.broadcasted_iota(jnp.int32, (k * k, tm), 0).astype(F32)
        slot = lax.broadcasted_iota(jnp.int32, (k, tm), 0)
        eid = jnp.zeros((k, tm), F32)
        for r in range(k):
            picked = jnp.max(jnp.where(rows == j[r:r + 1], cid, -1.0), axis=0, keepdims=True)
            eid = jnp.where(slot == r, picked, eid)
        e = jnp.exp(sc - sc[0:1])
        gate_ref[h] = e / jnp.sum(e, axis=0, keepdims=True)
        eid_ref[h] = eid.astype(jnp.int32)


def _peer_topk(q, sk):
    tokens = q.shape[0]
    tm = 256
    kern = functools.partial(_peer_topk_kernel, tm=tm)
    out_blk = pl.BlockSpec((PEER_HEADS, PEER_TOPK, tm), lambda i: (0, 0, i))
    return pl.pallas_call(
        kern,
        grid=(tokens // tm,),
        in_specs=[pl.BlockSpec((tm, PEER_HEADS * LANES), lambda i: (i, 0)),
                  pl.BlockSpec((2 * PEER_NKEYS, LANES), lambda i: (0, 0))],
        out_specs=[out_blk, out_blk],
        out_shape=[jax.ShapeDtypeStruct((PEER_HEADS, PEER_TOPK, tokens), F32),
                   jax.ShapeDtypeStruct((PEER_HEADS, PEER_TOPK, tokens), jnp.int32)],
        compiler_params=_cparams(1),
        name="peer_topk",
    )(q, sk)


def _peer_gather_kernel(eid_hbm, uv_hbm, gate_ref, h_ref, g_ref, b_ref, o_ref, idx_smem, acc, idx_sem, row_sem,
                        *rows, tt, d):
    i = pl.program_id(0)
    nsel = PEER_HEADS * PEER_TOPK
    ns = len(rows)

    idx_copy = pltpu.make_async_copy(eid_hbm.at[i], idx_smem, idx_sem)
    idx_copy.start()
    idx_copy.wait()

    def issue(tok, slot):
        for j in range(nsel):
            e = idx_smem[tok * nsel + j]
            pltpu.make_async_copy(uv_hbm.at[e], rows[slot].at[j], row_sem.at[slot]).start()

    def wait_slot(slot):
        pltpu.make_async_copy(rows[slot], rows[slot], row_sem.at[slot]).wait()

    for s in range(ns - 1):
        issue(s, s)
    lane = lax.broadcasted_iota(jnp.int32, (nsel, tt), 1)

    def compute(tok, slot):
        buf = rows[slot]
        y = h_ref[pl.ds(tok, 1), :]
        part = buf[:, 0:LANES] * y[:, 0:LANES]
        for c in range(1, d // LANES):
            part = part + buf[:, c * LANES:(c + 1) * LANES] * y[:, c * LANES:(c + 1) * LANES]
        act = jax.nn.gelu(jnp.sum(part, axis=-1, keepdims=True))
        gcol = jnp.sum(jnp.where(lane == tok, gate_ref[...], 0.0), axis=-1, keepdims=True)
        w = gcol * act
        acc[pl.ds(tok, 1), :] = jnp.sum(w * buf[:, d:2 * d], axis=0, keepdims=True)

    def body(grp, carry):
        for s in range(ns):
            tok = grp * ns + s
            ahead = tok + ns - 1

            @pl.when(ahead < tt)
            def _():
                issue(ahead, (s + ns - 1) % ns)

            wait_slot(s)
            compute(tok, s)
        return carry

    lax.fori_loop(0, tt // ns, body, 0)
    z = DN_ALPHA * h_ref[...] + acc[...]
    o_ref[...] = _layer_norm(z, g_ref[...], b_ref[...])


def _peer_gather(eid, uv, gate, h, g, b):
    tokens, d = h.shape
    tt = PEER_TT
    nsel = PEER_HEADS * PEER_TOPK
    kern = functools.partial(_peer_gather_kernel, tt=tt, d=d)
    vec = pl.BlockSpec((1, d), lambda i: (0, 0))
    return pl.pallas_call(
        kern,
        grid=(tokens // tt,),
        in_specs=[pl.BlockSpec(memory_space=pl.ANY),
                  pl.BlockSpec(memory_space=pl.ANY),
                  pl.BlockSpec((nsel, tt), lambda i: (0, i)),
                  pl.BlockSpec((tt, d), lambda i: (i, 0)),
                  vec, vec],
        out_specs=pl.BlockSpec((tt, d), lambda i: (i, 0)),
        out_shape=jax.ShapeDtypeStruct((tokens, d), F32),
        scratch_shapes=[pltpu.SMEM((tt * nsel,), jnp.int32),
                        pltpu.VMEM((tt, d), F32),
                        pltpu.SemaphoreType.DMA,
                        pltpu.SemaphoreType.DMA((PEER_SLOTS,))]
                       + [pltpu.VMEM((nsel, 2 * d), F32) for _ in range(PEER_SLOTS)],
        compiler_params=_cparams(1),
        name="peer_gather_ln2",
    )(eid, uv, gate, h, g, b)


def _heads_major(t, heads, width):
    return t.reshape(t.shape[0], heads, width).transpose(1, 0, 2)


def kernel(x, w_in, da_lam_q, da_lam_k, da_subln_g, cmp_pe_k, cmp_w1_k, cmp_w2_k, cmp_pe_v, cmp_w1_v,
           cmp_w2_v, w_branch_da, w_branch_nsa, w_out, ln1_g, ln1_b, peer_wq, peer_subkey1, peer_subkey2,
           peer_u, peer_v, ln2_g, ln2_b, rel_bias):
    batch, seq, d_model = x.shape
    tokens = batch * seq
    g, hd = NSA_GROUPS, NSA_DIM
    table = rel_bias.astype(F32)
    xs = x.reshape(tokens, d_model)
    for l in range(DEPTH):
        lam_init = 0.8 - 0.6 * math.exp(-0.3 * l)
        xb = xs.astype(BF16)
        w = w_in[l]
        scale = DA_HEAD_DIM ** -0.5
        c_daq, c_dak, c_dav, c_nq = 0, 1024, 2048, 3072
        c_kv, c_gate, c_mg, c_end = 4096, 5632, 5680, 9776
        w_att = jnp.concatenate([w[:, c_daq:c_dak] * scale, w[:, c_dak:c_nq], w[:, c_nq:c_kv] * scale],
                                axis=1).astype(BF16)
        att = _matmul(xb, w_att, BF16, 512, 512)
        kv = _matmul(xb, w[:, c_kv:c_gate].astype(BF16), F32, 512, 512)
        w_gate = jnp.pad(w[:, c_gate:c_mg], ((0, 0), (0, LANES - (c_mg - c_gate)))).astype(BF16)
        br_gate = _matmul(xb, w_gate, F32, 512, LANES)[:, :c_mg - c_gate]
        mg_gate = _matmul(xb, w[:, c_mg:c_end].astype(BF16), F32, 512, 512)

        lam_e = jnp.exp(jnp.sum(da_lam_q[l].astype(F32) * da_lam_k[l].astype(F32), -1))
        lam = (lam_e[0] - lam_e[1] + lam_init).reshape(1)
        table_rel = table - table[REL_BUCKETS - 1]
        o_da = _diff_attention(att, lam, _causal_bias_tiles(table_rel[:, :DA_HEADS], DA_TILE),
                               da_subln_g[l].reshape(1, DA_V_DIM), batch, seq, lam_init)

        q_n = _heads_major(att[:, 3072:4096], NSA_HEADS, hd)
        kv6 = kv.reshape(tokens, 6, g * hd)
        kc = _compress(kv6[:, 0], cmp_pe_k[l], cmp_w1_k[l], cmp_w2_k[l], batch, seq)
        vc = _compress(kv6[:, 1], cmp_pe_v[l], cmp_w1_v[l], cmp_w2_v[l], batch, seq)
        grp = lambda t: t.astype(BF16).reshape(batch, seq, g, hd).transpose(2, 0, 1, 3)
        k_s, v_s, k_w, v_w = grp(kv6[:, 2]), grp(kv6[:, 3]), grp(kv6[:, 4]), grp(kv6[:, 5])

        ncp = seq // CMP_STRIDE
        ns = seq // SLC_BLOCK
        nsp = -(-ns // LANES) * LANES
        k_sel = min(SLC_TOPK, ns)
        cmp_start = jnp.arange(ncp) * CMP_STRIDE
        slc_start = jnp.arange(nsp) * SLC_BLOCK
        overlap = ((cmp_start[:, None] <= slc_start[None, :] + SLC_BLOCK - 1)
                   & (cmp_start[:, None] + CMP_BLOCK - 1 >= slc_start[None, :])).astype(BF16)
        o_cmp, sel = _cmp_select(q_n, kc, vc, overlap, br_gate, batch, seq, k_sel)

        win_bias = _window_bias_tile(table[:, DA_HEADS:], WIN_TQ, WINDOW + WIN_TQ)
        pad_w = lambda t: jnp.pad(t, ((0, 0), (0, 0), (WINDOW, 0), (0, 0)))
        o_win = _window_attention(q_n, pad_w(k_w), pad_w(v_w), win_bias, br_gate, batch, seq)

        v_s1 = jnp.concatenate([v_s, jnp.ones_like(v_s)], axis=-1)
        o_nsa = _slc_attention(q_n, k_s, v_s1, sel, _causal_bias_tiles(table_rel[:, DA_HEADS:], SLC_TILE),
                               br_gate, o_cmp, o_win, batch, seq)
        o_nsa = o_nsa.transpose(1, 0, 2).reshape(tokens, NSA_HEADS * hd)

        mixed = _merge(o_da, o_nsa, w_branch_da[l].astype(BF16), w_branch_nsa[l].astype(BF16), mg_gate)
        h, hb = _outproj_ln(mixed, w_out[l].astype(BF16), xs, ln1_g[l].reshape(1, -1), ln1_b[l].reshape(1, -1))

        pq = _matmul(hb, peer_wq[l].astype(BF16), BF16, 512, 512)
        half = peer_subkey1.shape[-1]
        zeros = jnp.zeros((PEER_NKEYS, half), F32)
        sk = jnp.concatenate([jnp.concatenate([peer_subkey1[l], zeros], axis=1),
                              jnp.concatenate([zeros, peer_subkey2[l]], axis=1)], axis=0).astype(BF16)
        gate, eid = _peer_topk(pq, sk)
        nsel = PEER_HEADS * PEER_TOPK
        eid_tok = eid.reshape(nsel, tokens).T.reshape(tokens // PEER_TT, PEER_TT * nsel)
        uv = jnp.concatenate([peer_u[l], peer_v[l]], axis=1)
        xs = _peer_gather(eid_tok, uv, gate.reshape(nsel, tokens), h,
                          ln2_g[l].reshape(1, -1), ln2_b[l].reshape(1, -1))
    return xs.reshape(batch, seq, d_model)
```

```python
import functools
import math

import jax
import jax.numpy as jnp
from jax import lax
from jax.experimental import pallas as pl
from jax.experimental.pallas import tpu as pltpu

F32 = jnp.float32
BF16 = jnp.bfloat16

DA_HEADS = 8
DA_HEAD_DIM = 64
DA_V_DIM = 128
NSA_HEADS = 16
NSA_GROUPS = 4
NSA_HPG = 4
NSA_DIM = 64
CMP_BLOCK = 32
CMP_STRIDE = 16
SLC_BLOCK = 64
SLC_TOPK = 16
N_LOCAL_BLOCKS = 2
WINDOW = 512
REL_BUCKETS = 32
REL_MAX_DIST = 128
PEER_HEADS = 8
PEER_NKEYS = 128
PEER_TOPK = 16
DEPTH = 1
DN_ALPHA = (2 * DEPTH) ** 0.25
LN_EPS = 1e-5
NEG = -1e30
FORCE_SCORE = 1e4

LANES = 128
VMEM_LIMIT = 48 * 1024 * 1024
DA_TILE = 512
SLC_TILE = 512
WIN_TQ = 256
CMP_TQ = 128
PEER_TT = 128
PEER_SLOTS = 8


def _cparams(n_axes):
    return pltpu.CompilerParams(dimension_semantics=("arbitrary",) * n_axes,
                                vmem_limit_bytes=VMEM_LIMIT)


def _dot_nt(a, b):
    return lax.dot_general(a, b, (((1,), (1,)), ((), ())), preferred_element_type=F32)


def _dot_exact01(x, onehot_bf16):
    hi = x.astype(BF16)
    r1 = x - hi.astype(F32)
    mid = r1.astype(BF16)
    lo = (r1 - mid.astype(F32)).astype(BF16)
    d = lambda a: jnp.dot(a, onehot_bf16, preferred_element_type=F32)
    return d(hi) + d(mid) + d(lo)


def _mm_kernel(a_ref, b_ref, o_ref):
    o_ref[...] = jnp.dot(a_ref[...], b_ref[...], preferred_element_type=F32).astype(o_ref.dtype)


def _matmul(a, b, out_dtype, tm, tn):
    m, k = a.shape
    n = b.shape[1]
    return pl.pallas_call(
        _mm_kernel,
        grid=(n // tn, m // tm),
        in_specs=[pl.BlockSpec((tm, k), lambda j, i: (i, 0)),
                  pl.BlockSpec((k, tn), lambda j, i: (0, j))],
        out_specs=pl.BlockSpec((tm, tn), lambda j, i: (i, j)),
        out_shape=jax.ShapeDtypeStruct((m, n), out_dtype),
        compiler_params=_cparams(2),
        name="matmul",
    )(a, b)


def _rel_bucket(dist):
    n = jnp.maximum(dist, 0)
    max_exact = REL_BUCKETS // 2
    nf = jnp.maximum(n, 1).astype(F32)
    large = max_exact + (jnp.log(nf / max_exact) / math.log(REL_MAX_DIST / max_exact)
                         * (REL_BUCKETS - max_exact)).astype(jnp.int32)
    large = jnp.minimum(large, REL_BUCKETS - 1)
    return jnp.where(n < max_exact, n, large)


def _toeplitz(rd, n_i, n_j):
    h, length = rd.shape
    rev = rd[:, ::-1]
    flat = jnp.broadcast_to(rev[:, None, :], (h, n_i, length)).reshape(h, n_i * length)
    skew = jnp.pad(flat, ((0, 0), (0, n_i))).reshape(h, n_i, length + 1)
    return skew[:, ::-1, :n_j]


def _bias_by_distance(table, n):
    return table[_rel_bucket(jnp.arange(n))].T


def _causal_bias_tiles(table, t):
    bd = _bias_by_distance(table, 2 * t)
    neg = jnp.full((bd.shape[0], t - 1), NEG, F32)
    diag = _toeplitz(jnp.concatenate([neg, bd[:, :t]], axis=1), t, t)
    off = _toeplitz(bd[:, 1:], t, t)
    return jnp.stack([jnp.zeros_like(off), off, diag], axis=0)


def _window_bias_tile(table, tq, span):
    bd = _bias_by_distance(table, WINDOW)
    h = bd.shape[0]
    lo = jnp.full((h, span - 1 - WINDOW), NEG, F32)
    hi = jnp.full((h, tq), NEG, F32)
    return _toeplitz(jnp.concatenate([lo, bd, hi], axis=1), tq, span)


def _flash_init(m_scr, acc_scr):
    m_scr[...] = jnp.full(m_scr.shape, NEG, F32)
    acc_scr[...] = jnp.zeros(acc_scr.shape, F32)


def _flash_update(s, v1, m_scr, acc_scr):
    m_old = m_scr[...]
    m_new = jnp.maximum(m_old, jnp.max(s, axis=-1, keepdims=True))
    alpha = jnp.exp(m_old - m_new)
    p = jnp.exp(s - m_new).astype(BF16)
    acc_scr[...] = alpha * acc_scr[...] + jnp.dot(p, v1, preferred_element_type=F32)
    m_scr[...] = m_new


def _pipelined_sweep(n_tiles, logits_fn, consume_fn, s_scr):
    last = n_tiles - 1
    s_scr[0] = logits_fn(0)

    def body(pair, carry):
        a = 2 * pair
        s_scr[1] = logits_fn(a + 1)
        consume_fn(a, 0)
        s_scr[0] = logits_fn(jnp.minimum(a + 2, last))
        consume_fn(a + 1, 1)
        return carry

    lax.fori_loop(0, n_tiles // 2, body, 0)

    @pl.when(n_tiles % 2 == 1)
    def _():
        consume_fn(last, 0)


def _near_kind(j, qi):
    return jnp.maximum(j - (qi - 2), 0)


def _head_gates(gate_ref, group, branch):
    sig = jax.nn.sigmoid(gate_ref[...])
    lane = lax.broadcasted_iota(jnp.int32, (1, sig.shape[-1]), 1)
    cols = [jnp.sum(jnp.where(lane == (group * NSA_HPG + h) * 3 + branch, sig, 0.0), axis=-1, keepdims=True)
            for h in range(NSA_HPG)]
    return jnp.stack(cols)


def _da_kernel(lam_ref, q_ref, k_ref, v_ref, bias_ref, g_ref, o_ref, m_scr, acc_scr, s_scr, *, t, lam_init):
    qi = pl.program_id(2)
    q = q_ref[...]
    lane = lax.broadcasted_iota(jnp.int32, (1, LANES), 1)
    zero = jnp.zeros_like(q)
    q2 = jnp.concatenate([jnp.where(lane < DA_HEAD_DIM, q, zero), jnp.where(lane >= DA_HEAD_DIM, q, zero)], axis=0)
    _flash_init(m_scr, acc_scr)

    def logits(j):
        return _dot_nt(q2, k_ref[pl.ds(pl.multiple_of(j * t, t), t), :])

    def consume(j, slot):
        s = (s_scr[slot].reshape(2, t, t) + bias_ref[_near_kind(j, qi)][None]).reshape(2 * t, t)
        v = v_ref[pl.ds(pl.multiple_of(j * t, t), t), :]
        v1 = jnp.concatenate([v, jnp.ones((t, LANES), BF16)], axis=-1)
        _flash_update(s, v1, m_scr, acc_scr)

    _pipelined_sweep(qi + 1, logits, consume, s_scr)

    acc = acc_scr[...]
    o = acc[:t, :DA_V_DIM] / acc[:t, DA_V_DIM:] - lam_ref[0] * (acc[t:, :DA_V_DIM] / acc[t:, DA_V_DIM:])
    ms = jnp.mean(o * o, axis=-1, keepdims=True)
    o = (o * lax.rsqrt(ms + LN_EPS)) * g_ref[...] * (1.0 - lam_init)
    o_ref[...] = o.astype(o_ref.dtype)


def _diff_attention(att, lam, bias, subln_g, batch, seq, lam_init):
    t = DA_TILE
    nq = seq // t
    kern = functools.partial(_da_kernel, t=t, lam_init=lam_init)
    return pl.pallas_call(
        kern,
        grid=(batch, DA_HEADS, nq),
        in_specs=[pl.BlockSpec(memory_space=pltpu.SMEM),
                  pl.BlockSpec((t, LANES), lambda b, h, i: (b * nq + i, h)),
                  pl.BlockSpec((seq, LANES), lambda b, h, i: (b, DA_HEADS + h)),
                  pl.BlockSpec((seq, LANES), lambda b, h, i: (b, 2 * DA_HEADS + h)),
                  pl.BlockSpec((3, None, t, t), lambda b, h, i: (0, h, 0, 0)),
                  pl.BlockSpec((1, DA_V_DIM), lambda b, h, i: (0, 0))],
        out_specs=pl.BlockSpec((t, DA_V_DIM), lambda b, h, i: (b * nq + i, h)),
        out_shape=jax.ShapeDtypeStruct((batch * seq, DA_HEADS * DA_V_DIM), BF16),
        scratch_shapes=[pltpu.VMEM((2 * t, 1), F32), pltpu.VMEM((2 * t, 2 * DA_V_DIM), F32),
                        pltpu.VMEM((2, 2 * t, t), F32)],
        compiler_params=_cparams(3),
        name="diff_attention",
    )(lam, att, att, att, bias, subln_g)


def _compress_kernel(x_ref, pe_ref, w1_ref, w2_ref, o_ref):
    x = (x_ref[...].astype(F32) + pe_ref[...]).astype(BF16)
    hid = jax.nn.gelu(jnp.dot(x, w1_ref[...], preferred_element_type=F32))
    o_ref[...] = jnp.dot(hid.astype(BF16), w2_ref[...], preferred_element_type=F32).astype(o_ref.dtype)


def _compress(t, pe, w1, w2, batch, seq):
    g, d = NSA_GROUPS, NSA_DIM
    r = CMP_BLOCK // CMP_STRIDE
    nch = seq // CMP_STRIDE
    nc = nch - r + 1
    ch = t.reshape(batch, nch, CMP_STRIDE, g, d)
    blocks = jnp.concatenate([ch[:, j:j + nc] for j in range(r)], axis=2)
    flat = blocks.transpose(0, 1, 3, 2, 4).reshape(batch * nc * g, CMP_BLOCK * d)
    rows = flat.shape[0]
    tm = 512
    rows_p = -(-rows // tm) * tm
    flat = jnp.pad(flat, ((0, rows_p - rows), (0, 0)))
    hidden = w1.shape[1]
    out = pl.pallas_call(
        _compress_kernel,
        grid=(rows_p // tm,),
        in_specs=[pl.BlockSpec((tm, CMP_BLOCK * d), lambda i: (i, 0)),
                  pl.BlockSpec((1, CMP_BLOCK * d), lambda i: (0, 0)),
                  pl.BlockSpec((CMP_BLOCK * d, hidden), lambda i: (0, 0)),
                  pl.BlockSpec((hidden, d), lambda i: (0, 0))],
        out_specs=pl.BlockSpec((tm, d), lambda i: (i, 0)),
        out_shape=jax.ShapeDtypeStruct((rows_p, d), BF16),
        compiler_params=_cparams(1),
        name="compress_mlp",
    )(flat, pe.reshape(1, CMP_BLOCK * d), w1.astype(BF16), w2.astype(BF16))
    out = out[:rows].reshape(batch, nc, g, d).transpose(2, 0, 1, 3)
    return jnp.pad(out, ((0, 0), (0, 0), (0, nch - nc), (0, 0)))


def _cmp_select_kernel(q_ref, kc_ref, vc_ref, ov_ref, gate_ref, o_ref, sel_ref, *, tq, k_sel):
    qi = pl.program_id(2)
    ncp = kc_ref.shape[0]
    nsp = sel_ref.shape[-1]
    q = q_ref[...].reshape(NSA_HPG * tq, NSA_DIM)
    q_pos = qi * tq + lax.broadcasted_iota(jnp.int32, (tq, 1), 0)
    c_end = lax.broadcasted_iota(jnp.int32, (1, ncp), 1) * CMP_STRIDE + (CMP_BLOCK - 1)
    cmask = (c_end <= q_pos)[None]
    s = _dot_nt(q, kc_ref[...]).reshape(NSA_HPG, tq, ncp)
    s = jnp.where(cmask, s, NEG)
    m = jnp.max(s, axis=-1, keepdims=True)
    p = jnp.where(cmask, jnp.exp(s - m), 0.0)
    p = p / jnp.maximum(jnp.sum(p, axis=-1, keepdims=True), 1e-30)
    o = jnp.dot(p.reshape(NSA_HPG * tq, ncp).astype(BF16), vc_ref[...], preferred_element_type=F32)
    o_ref[...] = o.reshape(NSA_HPG, tq, NSA_DIM) * _head_gates(gate_ref, pl.program_id(0), 0)

    psum = p[0] + p[1] + p[2] + p[3]
    imp = _dot_exact01(psum, ov_ref[...])
    blk = lax.broadcasted_iota(jnp.int32, (1, nsp), 1)
    cur = q_pos // SLC_BLOCK
    valid = blk <= cur
    forced = valid & ((blk == 0) | (blk > cur - N_LOCAL_BLOCKS))
    score = jnp.where(forced, FORCE_SCORE, jnp.where(valid, imp, NEG))
    blk_f = blk.astype(F32)
    sel = jnp.zeros((tq, nsp), F32)
    for _ in range(k_sel):
        mx = jnp.max(score, axis=-1, keepdims=True)
        idx = jnp.min(jnp.where(score == mx, blk_f, float(nsp)), axis=-1, keepdims=True)
        hit = blk_f == idx
        sel = jnp.where(hit, 1.0, sel)
        score = jnp.where(hit, -jnp.inf, score)
    sel_ref[...] = sel.astype(sel_ref.dtype)


def _cmp_select(q, kc, vc, overlap, gate, batch, seq, k_sel):
    tq = CMP_TQ
    nq = seq // tq
    ncp = kc.shape[2]
    nsp = overlap.shape[1]
    kern = functools.partial(_cmp_select_kernel, tq=tq, k_sel=k_sel)
    return pl.pallas_call(
        kern,
        grid=(NSA_GROUPS, batch, nq),
        in_specs=[pl.BlockSpec((NSA_HPG, tq, NSA_DIM), lambda g, b, i: (g, b * nq + i, 0)),
                  pl.BlockSpec((None, None, ncp, NSA_DIM), lambda g, b, i: (g, b, 0, 0)),
                  pl.BlockSpec((None, None, ncp, NSA_DIM), lambda g, b, i: (g, b, 0, 0)),
                  pl.BlockSpec((ncp, nsp), lambda g, b, i: (0, 0)),
                  pl.BlockSpec((tq, gate.shape[1]), lambda g, b, i: (b * nq + i, 0))],
        out_specs=[pl.BlockSpec((NSA_HPG, tq, NSA_DIM), lambda g, b, i: (g, b * nq + i, 0)),
                   pl.BlockSpec((None, tq, nsp), lambda g, b, i: (g, b * nq + i, 0))],
        out_shape=[jax.ShapeDtypeStruct((NSA_HEADS, batch * seq, NSA_DIM), F32),
                   jax.ShapeDtypeStruct((NSA_GROUPS, batch * seq, nsp), BF16)],
        compiler_params=_cparams(3),
        name="nsa_cmp_select",
    )(q, kc, vc, overlap, gate)


def _window_kernel(q_ref, k_ref, v_ref, bias_ref, gate_ref, o_ref, *, tq, span):
    qi = pl.program_id(2)
    q0 = pl.multiple_of(qi * tq, tq)
    q = q_ref[...].reshape(NSA_HPG * tq, NSA_DIM)
    ks = k_ref[pl.ds(q0, span), :]
    vs = v_ref[pl.ds(q0, span), :]
    s = _dot_nt(q, ks).reshape(NSA_HPG, tq, span) + bias_ref[...]
    in_seq = (lax.broadcasted_iota(jnp.int32, (1, 1, span), 2) + q0) >= WINDOW
    s = jnp.where(in_seq, s, NEG)
    m = jnp.max(s, axis=-1, keepdims=True)
    p = jnp.exp(s - m)
    l = jnp.sum(p, axis=-1, keepdims=True)
    o = jnp.dot(p.reshape(NSA_HPG * tq, span).astype(BF16), vs, preferred_element_type=F32)
    o = o.reshape(NSA_HPG, tq, NSA_DIM) / l
    o_ref[...] = o * _head_gates(gate_ref, pl.program_id(0), 2)


def _window_attention(q, kp, vp, bias, gate, batch, seq):
    tq = WIN_TQ
    span = WINDOW + tq
    nq = seq // tq
    kern = functools.partial(_window_kernel, tq=tq, span=span)
    return pl.pallas_call(
        kern,
        grid=(NSA_GROUPS, batch, nq),
        in_specs=[pl.BlockSpec((NSA_HPG, tq, NSA_DIM), lambda g, b, i: (g, b * nq + i, 0)),
                  pl.BlockSpec((None, None, WINDOW + seq, NSA_DIM), lambda g, b, i: (g, b, 0, 0)),
                  pl.BlockSpec((None, None, WINDOW + seq, NSA_DIM), lambda g, b, i: (g, b, 0, 0)),
                  pl.BlockSpec((NSA_HPG, tq, span), lambda g, b, i: (g, 0, 0)),
                  pl.BlockSpec((tq, gate.shape[1]), lambda g, b, i: (b * nq + i, 0))],
        out_specs=pl.BlockSpec((NSA_HPG, tq, NSA_DIM), lambda g, b, i: (g, b * nq + i, 0)),
        out_shape=jax.ShapeDtypeStruct((NSA_HEADS, batch * seq, NSA_DIM), F32),
        compiler_params=_cparams(3),
        name="nsa_window",
    )(q, kp, vp, bias, gate)


def _slc_kernel(q_ref, k_ref, v_ref, sel_ref, bias_ref, gate_ref, oc_ref, ow_ref, o_ref,
                m_scr, acc_scr, s_scr, *, t):
    g = pl.program_id(0)
    qi = pl.program_id(2)
    nsp = sel_ref.shape[-1]
    rows = NSA_HPG * t
    q = q_ref[...].reshape(rows, NSA_DIM)
    sel = sel_ref[...]
    _flash_init(m_scr, acc_scr)
    blk_row = lax.broadcasted_iota(jnp.int32, (nsp, t), 0)
    key_blk = lax.broadcasted_iota(jnp.int32, (nsp, t), 1) // SLC_BLOCK

    def logits(j):
        return _dot_nt(q, k_ref[pl.ds(pl.multiple_of(j * t, t), t), :])

    def consume(j, slot):
        expand = jnp.where(blk_row == key_blk + j * (t // SLC_BLOCK), 1.0, 0.0).astype(BF16)
        off = (jnp.dot(sel, expand, preferred_element_type=F32) - 1.0) * (-NEG)
        s = s_scr[slot].reshape(NSA_HPG, t, t) + (off[None] + bias_ref[_near_kind(j, qi)])
        v1 = v_ref[pl.ds(pl.multiple_of(j * t, t), t), :]
        _flash_update(s.reshape(rows, t), v1, m_scr, acc_scr)

    _pipelined_sweep(qi + 1, logits, consume, s_scr)

    acc = acc_scr[...]
    o = (acc[:, :NSA_DIM] / acc[:, NSA_DIM:]).reshape(NSA_HPG, t, NSA_DIM)
    o = oc_ref[...] + ow_ref[...] + o * _head_gates(gate_ref, g, 1)
    o_ref[...] = o.astype(o_ref.dtype)


def _slc_attention(q, ks, vs1, sel, bias, gate, o_cmp, o_win, batch, seq):
    t = SLC_TILE
    nq = seq // t
    nsp = sel.shape[-1]
    kern = functools.partial(_slc_kernel, t=t)
    head_blk = lambda last: pl.BlockSpec((NSA_HPG, t, last), lambda g, b, i: (g, b * nq + i, 0))
    return pl.pallas_call(
        kern,
        grid=(NSA_GROUPS, batch, nq),
        in_specs=[head_blk(NSA_DIM),
                  pl.BlockSpec((None, None, seq, NSA_DIM), lambda g, b, i: (g, b, 0, 0)),
                  pl.BlockSpec((None, None, seq, 2 * NSA_DIM), lambda g, b, i: (g, b, 0, 0)),
                  pl.BlockSpec((None, t, nsp), lambda g, b, i: (g, b * nq + i, 0)),
                  pl.BlockSpec((3, NSA_HPG, t, t), lambda g, b, i: (0, g, 0, 0), pipeline_mode=pl.Buffered(1)),
                  pl.BlockSpec((t, gate.shape[1]), lambda g, b, i: (b * nq + i, 0)),
                  head_blk(NSA_DIM), head_blk(NSA_DIM)],
        out_specs=head_blk(NSA_DIM),
        out_shape=jax.ShapeDtypeStruct((NSA_HEADS, batch * seq, NSA_DIM), BF16),
        scratch_shapes=[pltpu.VMEM((NSA_HPG * t, 1), F32), pltpu.VMEM((NSA_HPG * t, 2 * NSA_DIM), F32),
                        pltpu.VMEM((2, NSA_HPG * t, t), F32)],
        compiler_params=_cparams(3),
        name="nsa_selected",
    )(q, ks, vs1, sel, bias, gate, o_cmp, o_win)


def _merge_kernel(oda_ref, onsa_ref, wda_ref, wnsa_ref, gda_ref, gnsa_ref, o_ref):
    a = jnp.dot(oda_ref[...], wda_ref[...], preferred_element_type=F32)
    n = jnp.dot(onsa_ref[...], wnsa_ref[...], preferred_element_type=F32)
    mixed = jax.nn.sigmoid(gda_ref[...]) * a + jax.nn.sigmoid(gnsa_ref[...]) * n
    o_ref[...] = mixed.astype(o_ref.dtype)


def _merge(o_da, o_nsa, w_da, w_nsa, gates):
    tokens, kd = o_da.shape
    d = w_da.shape[1]
    tm, tn = 512, 512
    nb = d // tn
    return pl.pallas_call(
        _merge_kernel,
        grid=(tokens // tm, nb),
        in_specs=[pl.BlockSpec((tm, kd), lambda i, j: (i, 0)),
                  pl.BlockSpec((tm, kd), lambda i, j: (i, 0)),
                  pl.BlockSpec((kd, tn), lambda i, j: (0, j)),
                  pl.BlockSpec((kd, tn), lambda i, j: (0, j)),
                  pl.BlockSpec((tm, tn), lambda i, j: (i, j)),
                  pl.BlockSpec((tm, tn), lambda i, j: (i, nb + j))],
        out_specs=pl.BlockSpec((tm, tn), lambda i, j: (i, j)),
        out_shape=jax.ShapeDtypeStruct((tokens, d), BF16),
        compiler_params=_cparams(2),
        name="gated_merge",
    )(o_da, o_nsa, w_da, w_nsa, gates, gates)


def _layer_norm(z, g, b):
    mu = jnp.mean(z, axis=-1, keepdims=True)
    zc = z - mu
    var = jnp.mean(zc * zc, axis=-1, keepdims=True)
    return (zc * lax.rsqrt(var + LN_EPS)) * g + b


def _outproj_ln_kernel(mixed_ref, w_ref, x_ref, g_ref, b_ref, h_ref, hb_ref):
    z = DN_ALPHA * x_ref[...] + jnp.dot(mixed_ref[...], w_ref[...], preferred_element_type=F32)
    h = _layer_norm(z, g_ref[...], b_ref[...])
    h_ref[...] = h
    hb_ref[...] = h.astype(hb_ref.dtype)


def _outproj_ln(mixed, w_out, x, g, b):
    tokens, d = x.shape
    tm = 256
    row = pl.BlockSpec((tm, d), lambda i: (i, 0))
    vec = pl.BlockSpec((1, d), lambda i: (0, 0))
    return pl.pallas_call(
        _outproj_ln_kernel,
        grid=(tokens // tm,),
        in_specs=[row, pl.BlockSpec((d, d), lambda i: (0, 0)), row, vec, vec],
        out_specs=[row, row],
        out_shape=[jax.ShapeDtypeStruct((tokens, d), F32), jax.ShapeDtypeStruct((tokens, d), BF16)],
        compiler_params=_cparams(1),
        name="outproj_ln1",
    )(mixed, w_out, x, g, b)


def _topk_rows(x, k):
    n, t = x.shape
    rows = lax.broadcasted_iota(jnp.int32, (n, t), 0).astype(F32)
    slot = lax.broadcasted_iota(jnp.int32, (k, t), 0)
    vals = jnp.zeros((k, t), F32)
    idxs = jnp.zeros((k, t), F32)
    for r in range(k):
        mx = jnp.max(x, axis=0, keepdims=True)
        idx = jnp.min(jnp.where(x == mx, rows, float(n)), axis=0, keepdims=True)
        vals = jnp.where(slot == r, mx, vals)
        idxs = jnp.where(slot == r, idx, idxs)
        x = jnp.where(rows == idx, -jnp.inf, x)
    return vals, idxs


def _peer_topk_kernel(q_ref, sk_ref, gate_ref, eid_ref, *, tm):
    k = PEER_TOPK
    nk = PEER_NKEYS
    for h in range(PEER_HEADS):
        qh = q_ref[:, h * LANES:(h + 1) * LANES]
        st = _dot_nt(sk_ref[...], qh)
        v1, i1 = _topk_rows(st[:nk], k)
        v2, i2 = _topk_rows(st[nk:], k)
        cand = jnp.concatenate([v1[a:a + 1] + v2 for a in range(k)], axis=0)
        cid = jnp.concatenate([i1[a:a + 1] * float(nk) + i2 for a in range(k)], axis=0)
        sc, j = _topk_rows(cand, k)
        rows = lax.broadcasted_iota(jnp.int32, (k * k, tm), 0).astype(F32)
        slot = lax.broadcasted_iota(jnp.int32, (k, tm), 0)
        eid = jnp.zeros((k, tm), F32)
        for r in range(k):
            picked = jnp.max(jnp.where(rows == j[r:r + 1], cid, -1.0), axis=0, keepdims=True)
            eid = jnp.where(slot == r, picked, eid)
        e = jnp.exp(sc - sc[0:1])
        gate_ref[h] = e / jnp.sum(e, axis=0, keepdims=True)
        eid_ref[h] = eid.astype(jnp.int32)


def _peer_topk(q, sk):
    tokens = q.shape[0]
    tm = 256
    kern = functools.partial(_peer_topk_kernel, tm=tm)
    out_blk = pl.BlockSpec((PEER_HEADS, PEER_TOPK, tm), lambda i: (0, 0, i))
    return pl.pallas_call(
        kern,
        grid=(tokens // tm,),
        in_specs=[pl.BlockSpec((tm, PEER_HEADS * LANES), lambda i: (i, 0)),
                  pl.BlockSpec((2 * PEER_NKEYS, LANES), lambda i: (0, 0))],
        out_specs=[out_blk, out_blk],
        out_shape=[jax.ShapeDtypeStruct((PEER_HEADS, PEER_TOPK, tokens), F32),
                   jax.ShapeDtypeStruct((PEER_HEADS, PEER_TOPK, tokens), jnp.int32)],
        compiler_params=_cparams(1),
        name="peer_topk",
    )(q, sk)


def _peer_gather_kernel(eid_hbm, uv_hbm, gate_ref, h_ref, g_ref, b_ref, o_ref, idx_smem, acc, idx_sem, row_sem,
                        *rows, tt, d):
    i = pl.program_id(0)
    nsel = PEER_HEADS * PEER_TOPK
    ns = len(rows)

    idx_copy = pltpu.make_async_copy(eid_hbm.at[i], idx_smem, idx_sem)
    idx_copy.start()
    idx_copy.wait()

    def issue(tok, slot):
        for j in range(nsel):
            e = idx_smem[tok * nsel + j]
            pltpu.make_async_copy(uv_hbm.at[e], rows[slot].at[j], row_sem.at[slot]).start()

    def wait_slot(slot):
        pltpu.make_async_copy(rows[slot], rows[slot], row_sem.at[slot]).wait()

    for s in range(ns - 1):
        issue(s, s)
    lane = lax.broadcasted_iota(jnp.int32, (nsel, tt), 1)

    def compute(tok, slot):
        buf = rows[slot]
        y = h_ref[pl.ds(tok, 1), :]
        part = buf[:, 0:LANES] * y[:, 0:LANES]
        for c in range(1, d // LANES):
            part = part + buf[:, c * LANES:(c + 1) * LANES] * y[:, c * LANES:(c + 1) * LANES]
        act = jax.nn.gelu(jnp.sum(part, axis=-1, keepdims=True))
        gcol = jnp.sum(jnp.where(lane == tok, gate_ref[...], 0.0), axis=-1, keepdims=True)
        w = gcol * act
        acc[pl.ds(tok, 1), :] = jnp.sum(w * buf[:, d:2 * d], axis=0, keepdims=True)

    def body(grp, carry):
        for s in range(ns):
            tok = grp * ns + s
            ahead = tok + ns - 1

            @pl.when(ahead < tt)
            def _():
                issue(ahead, (s + ns - 1) % ns)

            wait_slot(s)
            compute(tok, s)
        return carry

    lax.fori_loop(0, tt // ns, body, 0)
    z = DN_ALPHA * h_ref[...] + acc[...]
    o_ref[...] = _layer_norm(z, g_ref[...], b_ref[...])


def _peer_gather(eid, uv, gate, h, g, b):
    tokens, d = h.shape
    tt = PEER_TT
    nsel = PEER_HEADS * PEER_TOPK
    kern = functools.partial(_peer_gather_kernel, tt=tt, d=d)
    vec = pl.BlockSpec((1, d), lambda i: (0, 0))
    return pl.pallas_call(
        kern,
        grid=(tokens // tt,),
        in_specs=[pl.BlockSpec(memory_space=pl.ANY),
                  pl.BlockSpec(memory_space=pl.ANY),
                  pl.BlockSpec((nsel, tt), lambda i: (0, i)),
                  pl.BlockSpec((tt, d), lambda i: (i, 0)),
                  vec, vec],
        out_specs=pl.BlockSpec((tt, d), lambda i: (i, 0)),
        out_shape=jax.ShapeDtypeStruct((tokens, d), F32),
        scratch_shapes=[pltpu.SMEM((tt * nsel,), jnp.int32),
                        pltpu.VMEM((tt, d), F32),
                        pltpu.SemaphoreType.DMA,
                        pltpu.SemaphoreType.DMA((PEER_SLOTS,))]
                       + [pltpu.VMEM((nsel, 2 * d), F32) for _ in range(PEER_SLOTS)],
        compiler_params=_cparams(1),
        name="peer_gather_ln2",
    )(eid, uv, gate, h, g, b)


def _heads_major(t, heads, width):
    return t.reshape(t.shape[0], heads, width).transpose(1, 0, 2)


def kernel(x, w_in, da_lam_q, da_lam_k, da_subln_g, cmp_pe_k, cmp_w1_k, cmp_w2_k, cmp_pe_v, cmp_w1_v,
           cmp_w2_v, w_branch_da, w_branch_nsa, w_out, ln1_g, ln1_b, peer_wq, peer_subkey1, peer_subkey2,
           peer_u, peer_v, ln2_g, ln2_b, rel_bias):
    batch, seq, d_model = x.shape
    tokens = batch * seq
    g, hd = NSA_GROUPS, NSA_DIM
    table = rel_bias.astype(F32)
    xs = x.reshape(tokens, d_model)
    for l in range(DEPTH):
        lam_init = 0.8 - 0.6 * math.exp(-0.3 * l)
        xb = xs.astype(BF16)
        w = w_in[l]
        scale = DA_HEAD_DIM ** -0.5
        c_daq, c_dak, c_dav, c_nq = 0, 1024, 2048, 3072
        c_kv, c_gate, c_mg, c_end = 4096, 5632, 5680, 9776
        w_att = jnp.concatenate([w[:, c_daq:c_dak] * scale, w[:, c_dak:c_nq], w[:, c_nq:c_kv] * scale],
                                axis=1).astype(BF16)
        att = _matmul(xb, w_att, BF16, 512, 512)
        kv = _matmul(xb, w[:, c_kv:c_gate].astype(BF16), F32, 512, 512)
        w_gate = jnp.pad(w[:, c_gate:c_mg], ((0, 0), (0, LANES - (c_mg - c_gate)))).astype(BF16)
        br_gate = _matmul(xb, w_gate, F32, 512, LANES)[:, :c_mg - c_gate]
        mg_gate = _matmul(xb, w[:, c_mg:c_end].astype(BF16), F32, 512, 512)

        lam_e = jnp.exp(jnp.sum(da_lam_q[l].astype(F32) * da_lam_k[l].astype(F32), -1))
        lam = (lam_e[0] - lam_e[1] + lam_init).reshape(1)
        table_rel = table - table[REL_BUCKETS - 1]
        o_da = _diff_attention(att, lam, _causal_bias_tiles(table_rel[:, :DA_HEADS], DA_TILE),
                               da_subln_g[l].reshape(1, DA_V_DIM), batch, seq, lam_init)

        q_n = _heads_major(att[:, 3072:4096], NSA_HEADS, hd)
        kv6 = kv.reshape(tokens, 6, g * hd)
        kc = _compress(kv6[:, 0], cmp_pe_k[l], cmp_w1_k[l], cmp_w2_k[l], batch, seq)
        vc = _compress(kv6[:, 1], cmp_pe_v[l], cmp_w1_v[l], cmp_w2_v[l], batch, seq)
        grp = lambda t: t.astype(BF16).reshape(batch, seq, g, hd).transpose(2, 0, 1, 3)
        k_s, v_s, k_w, v_w = grp(kv6[:, 2]), grp(kv6[:, 3]), grp(kv6[:, 4]), grp(kv6[:, 5])

        ncp = seq // CMP_STRIDE
        ns = seq // SLC_BLOCK
        nsp = -(-ns // LANES) * LANES
        k_sel = min(SLC_TOPK, ns)
        cmp_start = jnp.arange(ncp) * CMP_STRIDE
        slc_start = jnp.arange(nsp) * SLC_BLOCK
        overlap = ((cmp_start[:, None] <= slc_start[None, :] + SLC_BLOCK - 1)
                   & (cmp_start[:, None] + CMP_BLOCK - 1 >= slc_start[None, :])).astype(BF16)
        o_cmp, sel = _cmp_select(q_n, kc, vc, overlap, br_gate, batch, seq, k_sel)

        win_bias = _window_bias_tile(table[:, DA_HEADS:], WIN_TQ, WINDOW + WIN_TQ)
        pad_w = lambda t: jnp.pad(t, ((0, 0), (0, 0), (WINDOW, 0), (0, 0)))
        o_win = _window_attention(q_n, pad_w(k_w), pad_w(v_w), win_bias, br_gate, batch, seq)

        v_s1 = jnp.concatenate([v_s, jnp.ones_like(v_s)], axis=-1)
        o_nsa = _slc_attention(q_n, k_s, v_s1, sel, _causal_bias_tiles(table_rel[:, DA_HEADS:], SLC_TILE),
                               br_gate, o_cmp, o_win, batch, seq)
        o_nsa = o_nsa.transpose(1, 0, 2).reshape(tokens, NSA_HEADS * hd)

        mixed = _merge(o_da, o_nsa, w_branch_da[l].astype(BF16), w_branch_nsa[l].astype(BF16), mg_gate)
        h, hb = _outproj_ln(mixed, w_out[l].astype(BF16), xs, ln1_g[l].reshape(1, -1), ln1_b[l].reshape(1, -1))

        pq = _matmul(hb, peer_wq[l].astype(BF16), BF16, 512, 512)
        half = peer_subkey1.shape[-1]
        zeros = jnp.zeros((PEER_NKEYS, half), F32)
        sk = jnp.concatenate([jnp.concatenate([peer_subkey1[l], zeros], axis=1),
                              jnp.concatenate([zeros, peer_subkey2[l]], axis=1)], axis=0).astype(BF16)
        gate, eid = _peer_topk(pq, sk)
        nsel = PEER_HEADS * PEER_TOPK
        eid_tok = eid.reshape(nsel, tokens).T.reshape(tokens // PEER_TT, PEER_TT * nsel)
        uv = jnp.concatenate([peer_u[l], peer_v[l]], axis=1)
        xs = _peer_gather(eid_tok, uv, gate.reshape(nsel, tokens), h,
                          ln2_g[l].reshape(1, -1), ln2_b[l].reshape(1, -1))
    return xs.reshape(batch, seq, d_model)
```

```python
import functools
import math

import jax
import jax.numpy as jnp
from jax import lax
from jax.experimental import pallas as pl
from jax.experimental.pallas import tpu as pltpu

F32 = jnp.float32
BF16 = jnp.bfloat16

DA_HEADS = 8
DA_HEAD_DIM = 64
DA_V_DIM = 128
NSA_HEADS = 16
NSA_GROUPS = 4
NSA_HPG = 4
NSA_DIM = 64
CMP_BLOCK = 32
CMP_STRIDE = 16
SLC_BLOCK = 64
SLC_TOPK = 16
N_LOCAL_BLOCKS = 2
WINDOW = 512
REL_BUCKETS = 32
REL_MAX_DIST = 128
PEER_HEADS = 8
PEER_NKEYS = 128
PEER_TOPK = 16
DEPTH = 1
DN_ALPHA = (2 * DEPTH) ** 0.25
LN_EPS = 1e-5
NEG = -1e30
FORCE_SCORE = 1e4

LANES = 128
VMEM_LIMIT = 48 * 1024 * 1024
DA_TILE = 512
SLC_TILE = 512
WIN_TQ = 256
CMP_TQ = 128
PEER_TT = 128
PEER_SLOTS = 8


def _cparams(n_axes):
    return pltpu.CompilerParams(dimension_semantics=("arbitrary",) * n_axes,
                                vmem_limit_bytes=VMEM_LIMIT)


def _dot_nt(a, b):
    return lax.dot_general(a, b, (((1,), (1,)), ((), ())), preferred_element_type=F32)


def _dot_exact01(x, onehot_bf16):
    hi = x.astype(BF16)
    r1 = x - hi.astype(F32)
    mid = r1.astype(BF16)
    lo = (r1 - mid.astype(F32)).astype(BF16)
    d = lambda a: jnp.dot(a, onehot_bf16, preferred_element_type=F32)
    return d(hi) + d(mid) + d(lo)


def _mm_kernel(a_ref, b_ref, o_ref):
    o_ref[...] = jnp.dot(a_ref[...], b_ref[...], preferred_element_type=F32).astype(o_ref.dtype)


def _matmul(a, b, out_dtype, tm, tn):
    m, k = a.shape
    n = b.shape[1]
    return pl.pallas_call(
        _mm_kernel,
        grid=(n // tn, m // tm),
        in_specs=[pl.BlockSpec((tm, k), lambda j, i: (i, 0)),
                  pl.BlockSpec((k, tn), lambda j, i: (0, j))],
        out_specs=pl.BlockSpec((tm, tn), lambda j, i: (i, j)),
        out_shape=jax.ShapeDtypeStruct((m, n), out_dtype),
        compiler_params=_cparams(2),
        name="matmul",
    )(a, b)


def _rel_bucket(dist):
    n = jnp.maximum(dist, 0)
    max_exact = REL_BUCKETS // 2
    nf = jnp.maximum(n, 1).astype(F32)
    large = max_exact + (jnp.log(nf / max_exact) / math.log(REL_MAX_DIST / max_exact)
                         * (REL_BUCKETS - max_exact)).astype(jnp.int32)
    large = jnp.minimum(large, REL_BUCKETS - 1)
    return jnp.where(n < max_exact, n, large)


def _toeplitz(rd, n_i, n_j):
    h, length = rd.shape
    rev = jnp.pad(rd[:, ::-1], ((0, 0), (0, 1)))
    flat = jnp.broadcast_to(rev[:, None, :], (h, n_i, length + 1)).reshape(h, n_i * (length + 1))
    skew = flat[:, :n_i * length].reshape(h, n_i, length)
    return skew[:, :, n_i - 1:n_i - 1 + n_j]


def _bias_by_distance(table, n):
    return table[_rel_bucket(jnp.arange(n))].T


def _causal_bias_tiles(table, t):
    bd = _bias_by_distance(table, 2 * t)
    neg = jnp.full((bd.shape[0], t - 1), NEG, F32)
    diag = _toeplitz(jnp.concatenate([neg, bd[:, :t]], axis=1), t, t)
    off = _toeplitz(bd[:, 1:], t, t)
    return jnp.stack([jnp.zeros_like(off), off, diag], axis=0)


def _window_bias_tile(table, tq, span):
    bd = _bias_by_distance(table, WINDOW)
    h = bd.shape[0]
    lo = jnp.full((h, span - 1 - WINDOW), NEG, F32)
    hi = jnp.full((h, tq), NEG, F32)
    return _toeplitz(jnp.concatenate([lo, bd, hi], axis=1), tq, span)


def _flash_init(m_scr, acc_scr):
    m_scr[...] = jnp.full(m_scr.shape, NEG, F32)
    acc_scr[...] = jnp.zeros(acc_scr.shape, F32)


def _flash_update(s, v1, m_scr, acc_scr):
    m_old = m_scr[...]
    m_new = jnp.maximum(m_old, jnp.max(s, axis=-1, keepdims=True))
    alpha = jnp.exp(m_old - m_new)
    p = jnp.exp(s - m_new).astype(BF16)
    acc_scr[...] = alpha * acc_scr[...] + jnp.dot(p, v1, preferred_element_type=F32)
    m_scr[...] = m_new


def _pipelined_sweep(n_tiles, logits_fn, consume_fn, s_scr):
    last = n_tiles - 1
    s_scr[0] = logits_fn(0)

    def body(pair, carry):
        a = 2 * pair
        s_scr[1] = logits_fn(a + 1)
        consume_fn(a, 0)
        s_scr[0] = logits_fn(jnp.minimum(a + 2, last))
        consume_fn(a + 1, 1)
        return carry

    lax.fori_loop(0, n_tiles // 2, body, 0)

    @pl.when(n_tiles % 2 == 1)
    def _():
        consume_fn(last, 0)


def _near_kind(j, qi):
    return jnp.maximum(j - (qi - 2), 0)


def _head_gates(gate_ref, group, branch):
    sig = jax.nn.sigmoid(gate_ref[...])
    lane = lax.broadcasted_iota(jnp.int32, (1, sig.shape[-1]), 1)
    cols = [jnp.sum(jnp.where(lane == (group * NSA_HPG + h) * 3 + branch, sig, 0.0), axis=-1, keepdims=True)
            for h in range(NSA_HPG)]
    return jnp.stack(cols)


def _da_kernel(lam_ref, q_ref, k_ref, v_ref, bias_ref, g_ref, o_ref, m_scr, acc_scr, s_scr, *, t, lam_init):
    qi = pl.program_id(2)
    q = q_ref[...]
    lane = lax.broadcasted_iota(jnp.int32, (1, LANES), 1)
    zero = jnp.zeros_like(q)
    q2 = jnp.concatenate([jnp.where(lane < DA_HEAD_DIM, q, zero), jnp.where(lane >= DA_HEAD_DIM, q, zero)], axis=0)
    _flash_init(m_scr, acc_scr)

    def logits(j):
        return _dot_nt(q2, k_ref[pl.ds(pl.multiple_of(j * t, t), t), :])

    def consume(j, slot):
        s = (s_scr[slot].reshape(2, t, t) + bias_ref[_near_kind(j, qi)][None]).reshape(2 * t, t)
        v = v_ref[pl.ds(pl.multiple_of(j * t, t), t), :]
        v1 = jnp.concatenate([v, jnp.ones((t, LANES), BF16)], axis=-1)
        _flash_update(s, v1, m_scr, acc_scr)

    _pipelined_sweep(qi + 1, logits, consume, s_scr)

    acc = acc_scr[...]
    o = acc[:t, :DA_V_DIM] / acc[:t, DA_V_DIM:] - lam_ref[0] * (acc[t:, :DA_V_DIM] / acc[t:, DA_V_DIM:])
    ms = jnp.mean(o * o, axis=-1, keepdims=True)
    o = (o * lax.rsqrt(ms + LN_EPS)) * g_ref[...] * (1.0 - lam_init)
    o_ref[...] = o.astype(o_ref.dtype)


def _diff_attention(att, lam, bias, subln_g, batch, seq, lam_init):
    t = DA_TILE
    nq = seq // t
    kern = functools.partial(_da_kernel, t=t, lam_init=lam_init)
    return pl.pallas_call(
        kern,
        grid=(batch, DA_HEADS, nq),
        in_specs=[pl.BlockSpec(memory_space=pltpu.SMEM),
                  pl.BlockSpec((t, LANES), lambda b, h, i: (b * nq + i, h)),
                  pl.BlockSpec((seq, LANES), lambda b, h, i: (b, DA_HEADS + h)),
                  pl.BlockSpec((seq, LANES), lambda b, h, i: (b, 2 * DA_HEADS + h)),
                  pl.BlockSpec((3, None, t, t), lambda b, h, i: (0, h, 0, 0)),
                  pl.BlockSpec((1, DA_V_DIM), lambda b, h, i: (0, 0))],
        out_specs=pl.BlockSpec((t, DA_V_DIM), lambda b, h, i: (b * nq + i, h)),
        out_shape=jax.ShapeDtypeStruct((batch * seq, DA_HEADS * DA_V_DIM), BF16),
        scratch_shapes=[pltpu.VMEM((2 * t, 1), F32), pltpu.VMEM((2 * t, 2 * DA_V_DIM), F32),
                        pltpu.VMEM((2, 2 * t, t), F32)],
        compiler_params=_cparams(3),
        name="diff_attention",
    )(lam, att, att, att, bias, subln_g)


def _compress_kernel(x_ref, pe_ref, w1_ref, w2_ref, o_ref):
    x = (x_ref[...].astype(F32) + pe_ref[...]).astype(BF16)
    hid = jax.nn.gelu(jnp.dot(x, w1_ref[...], preferred_element_type=F32))
    o_ref[...] = jnp.dot(hid.astype(BF16), w2_ref[...], preferred_element_type=F32).astype(o_ref.dtype)


def _compress(t, pe, w1, w2, batch, seq):
    g, d = NSA_GROUPS, NSA_DIM
    r = CMP_BLOCK // CMP_STRIDE
    nch = seq // CMP_STRIDE
    nc = nch - r + 1
    ch = t.reshape(batch, nch, CMP_STRIDE, g, d)
    blocks = jnp.concatenate([ch[:, j:j + nc] for j in range(r)], axis=2)
    flat = blocks.transpose(0, 1, 3, 2, 4).reshape(batch * nc * g, CMP_BLOCK * d)
    rows = flat.shape[0]
    tm = 512
    rows_p = -(-rows // tm) * tm
    flat = jnp.pad(flat, ((0, rows_p - rows), (0, 0)))
    hidden = w1.shape[1]
    out = pl.pallas_call(
        _compress_kernel,
        grid=(rows_p // tm,),
        in_specs=[pl.BlockSpec((tm, CMP_BLOCK * d), lambda i: (i, 0)),
                  pl.BlockSpec((1, CMP_BLOCK * d), lambda i: (0, 0)),
                  pl.BlockSpec((CMP_BLOCK * d, hidden), lambda i: (0, 0)),
                  pl.BlockSpec((hidden, d), lambda i: (0, 0))],
        out_specs=pl.BlockSpec((tm, d), lambda i: (i, 0)),
        out_shape=jax.ShapeDtypeStruct((rows_p, d), BF16),
        compiler_params=_cparams(1),
        name="compress_mlp",
    )(flat, pe.reshape(1, CMP_BLOCK * d), w1.astype(BF16), w2.astype(BF16))
    out = out[:rows].reshape(batch, nc, g, d).transpose(2, 0, 1, 3)
    return jnp.pad(out, ((0, 0), (0, 0), (0, nch - nc), (0, 0)))


def _cmp_select_kernel(q_ref, kc_ref, vc_ref, ov_ref, gate_ref, o_ref, sel_ref, *, tq, k_sel):
    qi = pl.program_id(2)
    ncp = kc_ref.shape[0]
    nsp = sel_ref.shape[-1]
    q = q_ref[...].reshape(NSA_HPG * tq, NSA_DIM)
    q_pos = qi * tq + lax.broadcasted_iota(jnp.int32, (tq, 1), 0)
    c_end = lax.broadcasted_iota(jnp.int32, (1, ncp), 1) * CMP_STRIDE + (CMP_BLOCK - 1)
    cmask = (c_end <= q_pos)[None]
    s = _dot_nt(q, kc_ref[...]).reshape(NSA_HPG, tq, ncp)
    s = jnp.where(cmask, s, NEG)
    m = jnp.max(s, axis=-1, keepdims=True)
    p = jnp.where(cmask, jnp.exp(s - m), 0.0)
    p = p / jnp.maximum(jnp.sum(p, axis=-1, keepdims=True), 1e-30)
    o = jnp.dot(p.reshape(NSA_HPG * tq, ncp).astype(BF16), vc_ref[...], preferred_element_type=F32)
    o_ref[...] = o.reshape(NSA_HPG, tq, NSA_DIM) * _head_gates(gate_ref, pl.program_id(0), 0)

    psum = p[0] + p[1] + p[2] + p[3]
    imp = _dot_exact01(psum, ov_ref[...])
    blk = lax.broadcasted_iota(jnp.int32, (1, nsp), 1)
    cur = q_pos // SLC_BLOCK
    valid = blk <= cur
    forced = valid & ((blk == 0) | (blk > cur - N_LOCAL_BLOCKS))
    score = jnp.where(forced, FORCE_SCORE, jnp.where(valid, imp, NEG))
    blk_f = blk.astype(F32)
    sel = jnp.zeros((tq, nsp), F32)
    for _ in range(k_sel):
        mx = jnp.max(score, axis=-1, keepdims=True)
        idx = jnp.min(jnp.where(score == mx, blk_f, float(nsp)), axis=-1, keepdims=True)
        hit = blk_f == idx
        sel = jnp.where(hit, 1.0, sel)
        score = jnp.where(hit, -jnp.inf, score)
    sel_ref[...] = sel.astype(sel_ref.dtype)


def _cmp_select(q, kc, vc, overlap, gate, batch, seq, k_sel):
    tq = CMP_TQ
    nq = seq // tq
    ncp = kc.shape[2]
    nsp = overlap.shape[1]
    kern = functools.partial(_cmp_select_kernel, tq=tq, k_sel=k_sel)
    return pl.pallas_call(
        kern,
        grid=(NSA_GROUPS, batch, nq),
        in_specs=[pl.BlockSpec((NSA_HPG, tq, NSA_DIM), lambda g, b, i: (g, b * nq + i, 0)),
                  pl.BlockSpec((None, None, ncp, NSA_DIM), lambda g, b, i: (g, b, 0, 0)),
                  pl.BlockSpec((None, None, ncp, NSA_DIM), lambda g, b, i: (g, b, 0, 0)),
                  pl.BlockSpec((ncp, nsp), lambda g, b, i: (0, 0)),
                  pl.BlockSpec((tq, gate.shape[1]), lambda g, b, i: (b * nq + i, 0))],
        out_specs=[pl.BlockSpec((NSA_HPG, tq, NSA_DIM), lambda g, b, i: (g, b * nq + i, 0)),
                   pl.BlockSpec((None, tq, nsp), lambda g, b, i: (g, b * nq + i, 0))],
        out_shape=[jax.ShapeDtypeStruct((NSA_HEADS, batch * seq, NSA_DIM), F32),
                   jax.ShapeDtypeStruct((NSA_GROUPS, batch * seq, nsp), BF16)],
        compiler_params=_cparams(3),
        name="nsa_cmp_select",
    )(q, kc, vc, overlap, gate)


def _window_kernel(q_ref, k_ref, v_ref, bias_ref, gate_ref, o_ref, *, tq, span):
    qi = pl.program_id(2)
    q0 = pl.multiple_of(qi * tq, tq)
    q = q_ref[...].reshape(NSA_HPG * tq, NSA_DIM)
    ks = k_ref[pl.ds(q0, span), :]
    vs = v_ref[pl.ds(q0, span), :]
    s = _dot_nt(q, ks).reshape(NSA_HPG, tq, span) + bias_ref[...]
    in_seq = (lax.broadcasted_iota(jnp.int32, (1, 1, span), 2) + q0) >= WINDOW
    s = jnp.where(in_seq, s, NEG)
    m = jnp.max(s, axis=-1, keepdims=True)
    p = jnp.exp(s - m)
    l = jnp.sum(p, axis=-1, keepdims=True)
    o = jnp.dot(p.reshape(NSA_HPG * tq, span).astype(BF16), vs, preferred_element_type=F32)
    o = o.reshape(NSA_HPG, tq, NSA_DIM) / l
    o_ref[...] = o * _head_gates(gate_ref, pl.program_id(0), 2)


def _window_attention(q, kp, vp, bias, gate, batch, seq):
    tq = WIN_TQ
    span = WINDOW + tq
    nq = seq // tq
    kern = functools.partial(_window_kernel, tq=tq, span=span)
    return pl.pallas_call(
        kern,
        grid=(NSA_GROUPS, batch, nq),
        in_specs=[pl.BlockSpec((NSA_HPG, tq, NSA_DIM), lambda g, b, i: (g, b * nq + i, 0)),
                  pl.BlockSpec((None, None, WINDOW + seq, NSA_DIM), lambda g, b, i: (g, b, 0, 0)),
                  pl.BlockSpec((None, None, WINDOW + seq, NSA_DIM), lambda g, b, i: (g, b, 0, 0)),
                  pl.BlockSpec((NSA_HPG, tq, span), lambda g, b, i: (g, 0, 0)),
                  pl.BlockSpec((tq, gate.shape[1]), lambda g, b, i: (b * nq + i, 0))],
        out_specs=pl.BlockSpec((NSA_HPG, tq, NSA_DIM), lambda g, b, i: (g, b * nq + i, 0)),
        out_shape=jax.ShapeDtypeStruct((NSA_HEADS, batch * seq, NSA_DIM), F32),
        compiler_params=_cparams(3),
        name="nsa_window",
    )(q, kp, vp, bias, gate)


def _slc_kernel(q_ref, k_ref, v_ref, sel_ref, bias_ref, gate_ref, oc_ref, ow_ref, o_ref,
                m_scr, acc_scr, s_scr, *, t):
    g = pl.program_id(0)
    qi = pl.program_id(2)
    nsp = sel_ref.shape[-1]
    rows = NSA_HPG * t
    q = q_ref[...].reshape(rows, NSA_DIM)
    sel = sel_ref[...]
    _flash_init(m_scr, acc_scr)
    blk_row = lax.broadcasted_iota(jnp.int32, (nsp, t), 0)
    key_blk = lax.broadcasted_iota(jnp.int32, (nsp, t), 1) // SLC_BLOCK

    def logits(j):
        return _dot_nt(q, k_ref[pl.ds(pl.multiple_of(j * t, t), t), :])

    def consume(j, slot):
        expand = jnp.where(blk_row == key_blk + j * (t // SLC_BLOCK), 1.0, 0.0).astype(BF16)
        off = (jnp.dot(sel, expand, preferred_element_type=F32) - 1.0) * (-NEG)
        s = s_scr[slot].reshape(NSA_HPG, t, t) + (off[None] + bias_ref[_near_kind(j, qi)])
        v1 = v_ref[pl.ds(pl.multiple_of(j * t, t), t), :]
        _flash_update(s.reshape(rows, t), v1, m_scr, acc_scr)

    _pipelined_sweep(qi + 1, logits, consume, s_scr)

    acc = acc_scr[...]
    o = (acc[:, :NSA_DIM] / acc[:, NSA_DIM:]).reshape(NSA_HPG, t, NSA_DIM)
    o = oc_ref[...] + ow_ref[...] + o * _head_gates(gate_ref, g, 1)
    o_ref[...] = o.astype(o_ref.dtype)


def _slc_attention(q, ks, vs1, sel, bias, gate, o_cmp, o_win, batch, seq):
    t = SLC_TILE
    nq = seq // t
    nsp = sel.shape[-1]
    kern = functools.partial(_slc_kernel, t=t)
    head_blk = lambda last: pl.BlockSpec((NSA_HPG, t, last), lambda g, b, i: (g, b * nq + i, 0))
    return pl.pallas_call(
        kern,
        grid=(NSA_GROUPS, batch, nq),
        in_specs=[head_blk(NSA_DIM),
                  pl.BlockSpec((None, None, seq, NSA_DIM), lambda g, b, i: (g, b, 0, 0)),
                  pl.BlockSpec((None, None, seq, 2 * NSA_DIM), lambda g, b, i: (g, b, 0, 0)),
                  pl.BlockSpec((None, t, nsp), lambda g, b, i: (g, b * nq + i, 0)),
                  pl.BlockSpec((3, NSA_HPG, t, t), lambda g, b, i: (0, g, 0, 0), pipeline_mode=pl.Buffered(1)),
                  pl.BlockSpec((t, gate.shape[1]), lambda g, b, i: (b * nq + i, 0)),
                  head_blk(NSA_DIM), head_blk(NSA_DIM)],
        out_specs=head_blk(NSA_DIM),
        out_shape=jax.ShapeDtypeStruct((NSA_HEADS, batch * seq, NSA_DIM), BF16),
        scratch_shapes=[pltpu.VMEM((NSA_HPG * t, 1), F32), pltpu.VMEM((NSA_HPG * t, 2 * NSA_DIM), F32),
                        pltpu.VMEM((2, NSA_HPG * t, t), F32)],
        compiler_params=_cparams(3),
        name="nsa_selected",
    )(q, ks, vs1, sel, bias, gate, o_cmp, o_win)


def _merge_kernel(oda_ref, onsa_ref, wda_ref, wnsa_ref, gda_ref, gnsa_ref, o_ref):
    a = jnp.dot(oda_ref[...], wda_ref[...], preferred_element_type=F32)
    n = jnp.dot(onsa_ref[...], wnsa_ref[...], preferred_element_type=F32)
    mixed = jax.nn.sigmoid(gda_ref[...]) * a + jax.nn.sigmoid(gnsa_ref[...]) * n
    o_ref[...] = mixed.astype(o_ref.dtype)


def _merge(o_da, o_nsa, w_da, w_nsa, gates):
    tokens, kd = o_da.shape
    d = w_da.shape[1]
    tm, tn = 512, 512
    nb = d // tn
    return pl.pallas_call(
        _merge_kernel,
        grid=(tokens // tm, nb),
        in_specs=[pl.BlockSpec((tm, kd), lambda i, j: (i, 0)),
                  pl.BlockSpec((tm, kd), lambda i, j: (i, 0)),
                  pl.BlockSpec((kd, tn), lambda i, j: (0, j)),
                  pl.BlockSpec((kd, tn), lambda i, j: (0, j)),
                  pl.BlockSpec((tm, tn), lambda i, j: (i, j)),
                  pl.BlockSpec((tm, tn), lambda i, j: (i, nb + j))],
        out_specs=pl.BlockSpec((tm, tn), lambda i, j: (i, j)),
        out_shape=jax.ShapeDtypeStruct((tokens, d), BF16),
        compiler_params=_cparams(2),
        name="gated_merge",
    )(o_da, o_nsa, w_da, w_nsa, gates, gates)


def _layer_norm(z, g, b):
    mu = jnp.mean(z, axis=-1, keepdims=True)
    zc = z - mu
    var = jnp.mean(zc * zc, axis=-1, keepdims=True)
    return (zc * lax.rsqrt(var + LN_EPS)) * g + b


def _outproj_ln_kernel(mixed_ref, w_ref, x_ref, g_ref, b_ref, h_ref, hb_ref):
    z = DN_ALPHA * x_ref[...] + jnp.dot(mixed_ref[...], w_ref[...], preferred_element_type=F32)
    h = _layer_norm(z, g_ref[...], b_ref[...])
    h_ref[...] = h
    hb_ref[...] = h.astype(hb_ref.dtype)


def _outproj_ln(mixed, w_out, x, g, b):
    tokens, d = x.shape
    tm = 256
    row = pl.BlockSpec((tm, d), lambda i: (i, 0))
    vec = pl.BlockSpec((1, d), lambda i: (0, 0))
    return pl.pallas_call(
        _outproj_ln_kernel,
        grid=(tokens // tm,),
        in_specs=[row, pl.BlockSpec((d, d), lambda i: (0, 0)), row, vec, vec],
        out_specs=[row, row],
        out_shape=[jax.ShapeDtypeStruct((tokens, d), F32), jax.ShapeDtypeStruct((tokens, d), BF16)],
        compiler_params=_cparams(1),
        name="outproj_ln1",
    )(mixed, w_out, x, g, b)


def _topk_rows(x, k):
    n, t = x.shape
    rows = lax.broadcasted_iota(jnp.int32, (n, t), 0).astype(F32)
    slot = lax.broadcasted_iota(jnp.int32, (k, t), 0)
    vals = jnp.zeros((k, t), F32)
    idxs = jnp.zeros((k, t), F32)
    for r in range(k):
        mx = jnp.max(x, axis=0, keepdims=True)
        idx = jnp.min(jnp.where(x == mx, rows, float(n)), axis=0, keepdims=True)
        vals = jnp.where(slot == r, mx, vals)
        idxs = jnp.where(slot == r, idx, idxs)
        x = jnp.where(rows == idx, -jnp.inf, x)
    return vals, idxs


def _peer_topk_kernel(q_ref, sk_ref, gate_ref, eid_ref, *, tm):
    k = PEER_TOPK
    nk = PEER_NKEYS
    for h in range(PEER_HEADS):
        qh = q_ref[:, h * LANES:(h + 1) * LANES]
        st = _dot_nt(sk_ref[...], qh)
        v1, i1 = _topk_rows(st[:nk], k)
        v2, i2 = _topk_rows(st[nk:], k)
        cand = jnp.concatenate([v1[a:a + 1] + v2 for a in range(k)], axis=0)
        cid = jnp.concatenate([i1[a:a + 1] * float(nk) + i2 for a in range(k)], axis=0)
        sc, j = _topk_rows(cand, k)
        rows = lax.broadcasted_iota(jnp.int32, (k * k, tm), 0).astype(F32)
        slot = lax.broadcasted_iota(jnp.int32, (k, tm), 0)
        eid = jnp.zeros((k, tm), F32)
        for r in range(k):
            picked = jnp.max(jnp.where(rows == j[r:r + 1], cid, -1.0), axis=0, keepdims=True)
            eid = jnp.where(slot == r, picked, eid)
        e = jnp.exp(sc - sc[0:1])
        gate_ref[h] = e / jnp.sum(e, axis=0, keepdims=True)
        eid_ref[h] = eid.astype(jnp.int32)


def _peer_topk(q, sk):
    tokens = q.shape[0]
    tm = 256
    kern = functools.partial(_peer_topk_kernel, tm=tm)
    out_blk = pl.BlockSpec((PEER_HEADS, PEER_TOPK, tm), lambda i: (0, 0, i))
    return pl.pallas_call(
        kern,
        grid=(tokens // tm,),
        in_specs=[pl.BlockSpec((tm, PEER_HEADS * LANES), lambda i: (i, 0)),
                  pl.BlockSpec((2 * PEER_NKEYS, LANES), lambda i: (0, 0))],
        out_specs=[out_blk, out_blk],
        out_shape=[jax.ShapeDtypeStruct((PEER_HEADS, PEER_TOPK, tokens), F32),
                   jax.ShapeDtypeStruct((PEER_HEADS, PEER_TOPK, tokens), jnp.int32)],
        compiler_params=_cparams(1),
        name="peer_topk",
    )(q, sk)


def _pack_bf16_pairs(t):
    bits = lax.bitcast_convert_type(t.astype(BF16), jnp.uint16).astype(jnp.uint32)
    half = t.shape[1] // 2
    return bits[:, :half] | (bits[:, half:] << 16)


def _unpack_bf16_pairs(w):
    return pltpu.bitcast(w << 16, F32), pltpu.bitcast(w & jnp.uint32(0xFFFF0000), F32)


def _peer_gather_kernel(eid_hbm, uv_hbm, gate_ref, h_ref, g_ref, b_ref, o_ref, idx_smem, acc, idx_sem, row_sem,
                        *rows, tt, d, tokens):
    i = pl.program_id(0)
    nsteps = pl.num_programs(0)
    nsel = PEER_HEADS * PEER_TOPK
    ns = len(rows)
    per = tt * nsel
    half = d // 2
    nchunk = half // LANES

    def idx_copy(step):
        dst = idx_smem.at[pl.ds(pl.multiple_of((step % 2) * per, per), per)]
        return pltpu.make_async_copy(eid_hbm.at[step], dst, idx_sem.at[step % 2])

    def issue(gtok, slot):
        base = (jnp.minimum(gtok, tokens - 1) % (2 * tt)) * nsel
        for j in range(nsel):
            e = idx_smem[base + j]
            pltpu.make_async_copy(uv_hbm.at[e], rows[slot].at[j], row_sem.at[slot]).start()

    def wait_slot(slot):
        pltpu.make_async_copy(rows[slot], rows[slot], row_sem.at[slot]).wait()

    @pl.when(i == 0)
    def _():
        first = idx_copy(0)
        first.start()
        first.wait()
        for s in range(ns - 1):
            issue(s, s)

    @pl.when(i + 1 < nsteps)
    def _():
        idx_copy(i + 1).start()

    lane = lax.broadcasted_iota(jnp.int32, (nsel, tt), 1)

    def compute(tok, slot):
        buf = rows[slot]
        y = h_ref[pl.ds(tok, 1), :]
        part = None
        for c in range(nchunk):
            lo, hi = _unpack_bf16_pairs(buf[:, c * LANES:(c + 1) * LANES])
            term = lo * y[:, c * LANES:(c + 1) * LANES] + hi * y[:, half + c * LANES:half + (c + 1) * LANES]
            part = term if part is None else part + term
        act = jax.nn.gelu(jnp.sum(part, axis=-1, keepdims=True))
        gcol = jnp.sum(jnp.where(lane == tok, gate_ref[...], 0.0), axis=-1, keepdims=True)
        w = gcol * act
        out_lo, out_hi = [], []
        for c in range(nchunk):
            lo, hi = _unpack_bf16_pairs(buf[:, half + c * LANES:half + (c + 1) * LANES])
            out_lo.append(jnp.sum(w * lo, axis=0, keepdims=True))
            out_hi.append(jnp.sum(w * hi, axis=0, keepdims=True))
        acc[pl.ds(tok, 1), :] = jnp.concatenate(out_lo + out_hi, axis=-1)

    n_groups = tt // ns

    def body(grp, carry):
        @pl.when((grp == n_groups - 1) & (i + 1 < nsteps))
        def _():
            idx_copy(i + 1).wait()

        for s in range(ns):
            tok = grp * ns + s
            issue(i * tt + tok + ns - 1, (s + ns - 1) % ns)
            wait_slot(s)
            compute(tok, s)
        return carry

    lax.fori_loop(0, n_groups, body, 0)

    @pl.when(i == nsteps - 1)
    def _():
        for s in range(ns - 1):
            wait_slot((tt + s) % ns)

    z = DN_ALPHA * h_ref[...] + acc[...]
    o_ref[...] = _layer_norm(z, g_ref[...], b_ref[...])


def _peer_gather(eid, uv, gate, h, g, b):
    tokens, d = h.shape
    tt = PEER_TT
    nsel = PEER_HEADS * PEER_TOPK
    kern = functools.partial(_peer_gather_kernel, tt=tt, d=d, tokens=tokens)
    vec = pl.BlockSpec((1, d), lambda i: (0, 0))
    return pl.pallas_call(
        kern,
        grid=(tokens // tt,),
        in_specs=[pl.BlockSpec(memory_space=pl.ANY),
                  pl.BlockSpec(memory_space=pl.ANY),
                  pl.BlockSpec((nsel, tt), lambda i: (0, i)),
                  pl.BlockSpec((tt, d), lambda i: (i, 0)),
                  vec, vec],
        out_specs=pl.BlockSpec((tt, d), lambda i: (i, 0)),
        out_shape=jax.ShapeDtypeStruct((tokens, d), F32),
        scratch_shapes=[pltpu.SMEM((2 * tt * nsel,), jnp.int32),
                        pltpu.VMEM((tt, d), F32),
                        pltpu.SemaphoreType.DMA((2,)),
                        pltpu.SemaphoreType.DMA((PEER_SLOTS,))]
                       + [pltpu.VMEM((nsel, d), jnp.uint32) for _ in range(PEER_SLOTS)],
        compiler_params=_cparams(1),
        name="peer_gather_ln2",
    )(eid, uv, gate, h, g, b)


def _heads_major(t, heads, width):
    return t.reshape(t.shape[0], heads, width).transpose(1, 0, 2)


def kernel(x, w_in, da_lam_q, da_lam_k, da_subln_g, cmp_pe_k, cmp_w1_k, cmp_w2_k, cmp_pe_v, cmp_w1_v,
           cmp_w2_v, w_branch_da, w_branch_nsa, w_out, ln1_g, ln1_b, peer_wq, peer_subkey1, peer_subkey2,
           peer_u, peer_v, ln2_g, ln2_b, rel_bias):
    batch, seq, d_model = x.shape
    tokens = batch * seq
    g, hd = NSA_GROUPS, NSA_DIM
    table = rel_bias.astype(F32)
    xs = x.reshape(tokens, d_model)
    for l in range(DEPTH):
        lam_init = 0.8 - 0.6 * math.exp(-0.3 * l)
        xb = xs.astype(BF16)
        w = w_in[l]
        scale = DA_HEAD_DIM ** -0.5
        c_daq, c_dak, c_dav, c_nq = 0, 1024, 2048, 3072
        c_kv, c_gate, c_mg, c_end = 4096, 5632, 5680, 9776
        w_att = jnp.concatenate([w[:, c_daq:c_dak] * scale, w[:, c_dak:c_nq], w[:, c_nq:c_kv] * scale],
                                axis=1).astype(BF16)
        att = _matmul(xb, w_att, BF16, 512, 512)
        kv = _matmul(xb, w[:, c_kv:c_gate].astype(BF16), F32, 512, 512)
        w_gate = jnp.pad(w[:, c_gate:c_mg], ((0, 0), (0, LANES - (c_mg - c_gate)))).astype(BF16)
        br_gate = _matmul(xb, w_gate, F32, 512, LANES)[:, :c_mg - c_gate]
        mg_gate = _matmul(xb, w[:, c_mg:c_end].astype(BF16), F32, 512, 512)

        lam_e = jnp.exp(jnp.sum(da_lam_q[l].astype(F32) * da_lam_k[l].astype(F32), -1))
        lam = (lam_e[0] - lam_e[1] + lam_init).reshape(1)
        table_rel = table - table[REL_BUCKETS - 1]
        o_da = _diff_attention(att, lam, _causal_bias_tiles(table_rel[:, :DA_HEADS], DA_TILE),
                               da_subln_g[l].reshape(1, DA_V_DIM), batch, seq, lam_init)

        q_n = _heads_major(att[:, 3072:4096], NSA_HEADS, hd)
        kv6 = kv.reshape(tokens, 6, g * hd)
        kc = _compress(kv6[:, 0], cmp_pe_k[l], cmp_w1_k[l], cmp_w2_k[l], batch, seq)
        vc = _compress(kv6[:, 1], cmp_pe_v[l], cmp_w1_v[l], cmp_w2_v[l], batch, seq)
        grp = lambda t: t.astype(BF16).reshape(batch, seq, g, hd).transpose(2, 0, 1, 3)
        k_s, v_s, k_w, v_w = grp(kv6[:, 2]), grp(kv6[:, 3]), grp(kv6[:, 4]), grp(kv6[:, 5])

        ncp = seq // CMP_STRIDE
        ns = seq // SLC_BLOCK
        nsp = -(-ns // LANES) * LANES
        k_sel = min(SLC_TOPK, ns)
        cmp_start = jnp.arange(ncp) * CMP_STRIDE
        slc_start = jnp.arange(nsp) * SLC_BLOCK
        overlap = ((cmp_start[:, None] <= slc_start[None, :] + SLC_BLOCK - 1)
                   & (cmp_start[:, None] + CMP_BLOCK - 1 >= slc_start[None, :])).astype(BF16)
        o_cmp, sel = _cmp_select(q_n, kc, vc, overlap, br_gate, batch, seq, k_sel)

        win_bias = _window_bias_tile(table[:, DA_HEADS:], WIN_TQ, WINDOW + WIN_TQ)
        pad_w = lambda t: jnp.pad(t, ((0, 0), (0, 0), (WINDOW, 0), (0, 0)))
        o_win = _window_attention(q_n, pad_w(k_w), pad_w(v_w), win_bias, br_gate, batch, seq)

        v_s1 = jnp.concatenate([v_s, jnp.ones_like(v_s)], axis=-1)
        o_nsa = _slc_attention(q_n, k_s, v_s1, sel, _causal_bias_tiles(table_rel[:, DA_HEADS:], SLC_TILE),
                               br_gate, o_cmp, o_win, batch, seq)
        o_nsa = o_nsa.transpose(1, 0, 2).reshape(tokens, NSA_HEADS * hd)

        mixed = _merge(o_da, o_nsa, w_branch_da[l].astype(BF16), w_branch_nsa[l].astype(BF16), mg_gate)
        h, hb = _outproj_ln(mixed, w_out[l].astype(BF16), xs, ln1_g[l].reshape(1, -1), ln1_b[l].reshape(1, -1))

        pq = _matmul(hb, peer_wq[l].astype(BF16), BF16, 512, 512)
        half = peer_subkey1.shape[-1]
        zeros = jnp.zeros((PEER_NKEYS, half), F32)
        sk = jnp.concatenate([jnp.concatenate([peer_subkey1[l], zeros], axis=1),
                              jnp.concatenate([zeros, peer_subkey2[l]], axis=1)], axis=0).astype(BF16)
        gate, eid = _peer_topk(pq, sk)
        nsel = PEER_HEADS * PEER_TOPK
        eid_tok = eid.reshape(nsel, tokens).T.reshape(tokens // PEER_TT, PEER_TT * nsel)
        uv = jnp.concatenate([_pack_bf16_pairs(peer_u[l]), _pack_bf16_pairs(peer_v[l])], axis=1)
        xs = _peer_gather(eid_tok, uv, gate.reshape(nsel, tokens), h,
                          ln2_g[l].reshape(1, -1), ln2_b[l].reshape(1, -1))
    return xs.reshape(batch, seq, d_model)
```

```python
import functools
import math

import jax
import jax.numpy as jnp
from jax import lax
from jax.experimental import pallas as pl
from jax.experimental.pallas import tpu as pltpu

F32 = jnp.float32
BF16 = jnp.bfloat16

DA_HEADS = 8
DA_HEAD_DIM = 64
DA_V_DIM = 128
NSA_HEADS = 16
NSA_GROUPS = 4
NSA_HPG = 4
NSA_DIM = 64
CMP_BLOCK = 32
CMP_STRIDE = 16
SLC_BLOCK = 64
SLC_TOPK = 16
N_LOCAL_BLOCKS = 2
WINDOW = 512
REL_BUCKETS = 32
REL_MAX_DIST = 128
PEER_HEADS = 8
PEER_NKEYS = 128
PEER_TOPK = 16
DEPTH = 1
DN_ALPHA = (2 * DEPTH) ** 0.25
LN_EPS = 1e-5
NEG = -1e30
FORCE_SCORE = 1e4

LANES = 128
VMEM_LIMIT = 48 * 1024 * 1024
DA_TILE = 512
SLC_TILE = 512
WIN_TQ = 256
CMP_TQ = 512
PEER_TT = 128
PEER_SLOTS = 8


def _cparams(n_axes):
    return pltpu.CompilerParams(dimension_semantics=("arbitrary",) * n_axes,
                                vmem_limit_bytes=VMEM_LIMIT)


def _dot_nt(a, b):
    return lax.dot_general(a, b, (((1,), (1,)), ((), ())), preferred_element_type=F32)


def _dot_exact01(x, onehot_bf16):
    hi = x.astype(BF16)
    r1 = x - hi.astype(F32)
    mid = r1.astype(BF16)
    lo = (r1 - mid.astype(F32)).astype(BF16)
    d = lambda a: jnp.dot(a, onehot_bf16, preferred_element_type=F32)
    return d(hi) + d(mid) + d(lo)


def _mm_kernel(a_ref, b_ref, o_ref):
    o_ref[...] = jnp.dot(a_ref[...], b_ref[...], preferred_element_type=F32).astype(o_ref.dtype)


def _matmul(a, b, out_dtype, tm, tn):
    m, k = a.shape
    n = b.shape[1]
    return pl.pallas_call(
        _mm_kernel,
        grid=(n // tn, m // tm),
        in_specs=[pl.BlockSpec((tm, k), lambda j, i: (i, 0)),
                  pl.BlockSpec((k, tn), lambda j, i: (0, j))],
        out_specs=pl.BlockSpec((tm, tn), lambda j, i: (i, j)),
        out_shape=jax.ShapeDtypeStruct((m, n), out_dtype),
        compiler_params=_cparams(2),
        name="matmul",
    )(a, b)


def _rel_bucket(dist):
    n = jnp.maximum(dist, 0)
    max_exact = REL_BUCKETS // 2
    nf = jnp.maximum(n, 1).astype(F32)
    large = max_exact + (jnp.log(nf / max_exact) / math.log(REL_MAX_DIST / max_exact)
                         * (REL_BUCKETS - max_exact)).astype(jnp.int32)
    large = jnp.minimum(large, REL_BUCKETS - 1)
    return jnp.where(n < max_exact, n, large)


def _toeplitz(rd, n_i, n_j):
    h, length = rd.shape
    rev = jnp.pad(rd[:, ::-1], ((0, 0), (0, 1)))
    flat = jnp.broadcast_to(rev[:, None, :], (h, n_i, length + 1)).reshape(h, n_i * (length + 1))
    skew = flat[:, :n_i * length].reshape(h, n_i, length)
    return skew[:, :, n_i - 1:n_i - 1 + n_j]


def _bias_by_distance(table, n):
    return table[_rel_bucket(jnp.arange(n))].T


def _causal_bias_tiles(table, t):
    bd = _bias_by_distance(table, 2 * t)
    neg = jnp.full((bd.shape[0], t - 1), NEG, F32)
    diag = _toeplitz(jnp.concatenate([neg, bd[:, :t]], axis=1), t, t)
    off = _toeplitz(bd[:, 1:], t, t)
    return jnp.stack([jnp.zeros_like(off), off, diag], axis=0)


def _window_bias_tile(table, tq, span):
    bd = _bias_by_distance(table, WINDOW)
    h = bd.shape[0]
    lo = jnp.full((h, span - 1 - WINDOW), NEG, F32)
    hi = jnp.full((h, tq), NEG, F32)
    return _toeplitz(jnp.concatenate([lo, bd, hi], axis=1), tq, span)


def _flash_init(m_scr, acc_scr):
    m_scr[...] = jnp.full(m_scr.shape, NEG, F32)
    acc_scr[...] = jnp.zeros(acc_scr.shape, F32)


def _flash_update(s, v1, m_scr, acc_scr):
    m_old = m_scr[...]
    m_new = jnp.maximum(m_old, jnp.max(s, axis=-1, keepdims=True))
    alpha = jnp.exp(m_old - m_new)
    p = jnp.exp(s - m_new).astype(BF16)
    acc_scr[...] = alpha * acc_scr[...] + jnp.dot(p, v1, preferred_element_type=F32)
    m_scr[...] = m_new


def _pipelined_sweep(n_tiles, logits_fn, consume_fn, s_scr):
    last = n_tiles - 1
    s_scr[0] = logits_fn(0)

    def body(pair, carry):
        a = 2 * pair
        s_scr[1] = logits_fn(a + 1)
        consume_fn(a, 0)
        s_scr[0] = logits_fn(jnp.minimum(a + 2, last))
        consume_fn(a + 1, 1)
        return carry

    lax.fori_loop(0, n_tiles // 2, body, 0)

    @pl.when(n_tiles % 2 == 1)
    def _():
        consume_fn(last, 0)


def _near_kind(j, qi):
    return jnp.maximum(j - (qi - 2), 0)


def _head_gates(gate_ref, group, branch):
    sig = jax.nn.sigmoid(gate_ref[...])
    lane = lax.broadcasted_iota(jnp.int32, (1, sig.shape[-1]), 1)
    cols = [jnp.sum(jnp.where(lane == (group * NSA_HPG + h) * 3 + branch, sig, 0.0), axis=-1, keepdims=True)
            for h in range(NSA_HPG)]
    return jnp.stack(cols)


def _da_kernel(lam_ref, q_ref, k_ref, v_ref, bias_ref, g_ref, o_ref, m_scr, acc_scr, s_scr, *, t, lam_init):
    qi = pl.program_id(2)
    q = q_ref[...]
    lane = lax.broadcasted_iota(jnp.int32, (1, LANES), 1)
    zero = jnp.zeros_like(q)
    q2 = jnp.concatenate([jnp.where(lane < DA_HEAD_DIM, q, zero), jnp.where(lane >= DA_HEAD_DIM, q, zero)], axis=0)
    _flash_init(m_scr, acc_scr)

    def logits(j):
        return _dot_nt(q2, k_ref[pl.ds(pl.multiple_of(j * t, t), t), :])

    def consume(j, slot):
        s = (s_scr[slot].reshape(2, t, t) + bias_ref[_near_kind(j, qi)][None]).reshape(2 * t, t)
        v = v_ref[pl.ds(pl.multiple_of(j * t, t), t), :]
        v1 = jnp.concatenate([v, jnp.ones((t, LANES), BF16)], axis=-1)
        _flash_update(s, v1, m_scr, acc_scr)

    _pipelined_sweep(qi + 1, logits, consume, s_scr)

    acc = acc_scr[...]
    o = acc[:t, :DA_V_DIM] / acc[:t, DA_V_DIM:] - lam_ref[0] * (acc[t:, :DA_V_DIM] / acc[t:, DA_V_DIM:])
    ms = jnp.mean(o * o, axis=-1, keepdims=True)
    o = (o * lax.rsqrt(ms + LN_EPS)) * g_ref[...] * (1.0 - lam_init)
    o_ref[...] = o.astype(o_ref.dtype)


def _diff_attention(att, lam, bias, subln_g, batch, seq, lam_init):
    t = DA_TILE
    nq = seq // t
    kern = functools.partial(_da_kernel, t=t, lam_init=lam_init)
    return pl.pallas_call(
        kern,
        grid=(batch, DA_HEADS, nq),
        in_specs=[pl.BlockSpec(memory_space=pltpu.SMEM),
                  pl.BlockSpec((t, LANES), lambda b, h, i: (b * nq + i, h)),
                  pl.BlockSpec((seq, LANES), lambda b, h, i: (b, DA_HEADS + h)),
                  pl.BlockSpec((seq, LANES), lambda b, h, i: (b, 2 * DA_HEADS + h)),
                  pl.BlockSpec((3, None, t, t), lambda b, h, i: (0, h, 0, 0)),
                  pl.BlockSpec((1, DA_V_DIM), lambda b, h, i: (0, 0))],
        out_specs=pl.BlockSpec((t, DA_V_DIM), lambda b, h, i: (b * nq + i, h)),
        out_shape=jax.ShapeDtypeStruct((batch * seq, DA_HEADS * DA_V_DIM), BF16),
        scratch_shapes=[pltpu.VMEM((2 * t, 1), F32), pltpu.VMEM((2 * t, 2 * DA_V_DIM), F32),
                        pltpu.VMEM((2, 2 * t, t), F32)],
        compiler_params=_cparams(3),
        name="diff_attention",
    )(lam, att, att, att, bias, subln_g)


def _compress_kernel(x_ref, pe_ref, w1_ref, w2_ref, o_ref):
    x = (x_ref[...].astype(F32) + pe_ref[...]).astype(BF16)
    hid = jax.nn.gelu(jnp.dot(x, w1_ref[...], preferred_element_type=F32))
    o_ref[...] = jnp.dot(hid.astype(BF16), w2_ref[...], preferred_element_type=F32).astype(o_ref.dtype)


def _compress(t, pe, w1, w2, batch, seq):
    g, d = NSA_GROUPS, NSA_DIM
    r = CMP_BLOCK // CMP_STRIDE
    nch = seq // CMP_STRIDE
    nc = nch - r + 1
    ch = t.reshape(batch, nch, CMP_STRIDE, g, d)
    blocks = jnp.concatenate([ch[:, j:j + nc] for j in range(r)], axis=2)
    flat = blocks.transpose(0, 1, 3, 2, 4).reshape(batch * nc * g, CMP_BLOCK * d)
    rows = flat.shape[0]
    tm = 512
    rows_p = -(-rows // tm) * tm
    flat = jnp.pad(flat, ((0, rows_p - rows), (0, 0)))
    hidden = w1.shape[1]
    out = pl.pallas_call(
        _compress_kernel,
        grid=(rows_p // tm,),
        in_specs=[pl.BlockSpec((tm, CMP_BLOCK * d), lambda i: (i, 0)),
                  pl.BlockSpec((1, CMP_BLOCK * d), lambda i: (0, 0)),
                  pl.BlockSpec((CMP_BLOCK * d, hidden), lambda i: (0, 0)),
                  pl.BlockSpec((hidden, d), lambda i: (0, 0))],
        out_specs=pl.BlockSpec((tm, d), lambda i: (i, 0)),
        out_shape=jax.ShapeDtypeStruct((rows_p, d), BF16),
        compiler_params=_cparams(1),
        name="compress_mlp",
    )(flat, pe.reshape(1, CMP_BLOCK * d), w1.astype(BF16), w2.astype(BF16))
    out = out[:rows].reshape(batch, nc, g, d).transpose(2, 0, 1, 3)
    return jnp.pad(out, ((0, 0), (0, 0), (0, nch - nc), (0, 0)))


def _cmp_select_kernel(q_ref, kc_ref, vc_ref, ov_ref, gate_ref, o_ref, sel_ref, *, tq, k_sel):
    qi = pl.program_id(2)
    ncp = kc_ref.shape[0]
    nsp = sel_ref.shape[-1]
    q = q_ref[...].reshape(NSA_HPG * tq, NSA_DIM)
    q_pos = qi * tq + lax.broadcasted_iota(jnp.int32, (tq, 1), 0)
    c_end = lax.broadcasted_iota(jnp.int32, (1, ncp), 1) * CMP_STRIDE + (CMP_BLOCK - 1)
    cmask = (c_end <= q_pos)[None]
    s = _dot_nt(q, kc_ref[...]).reshape(NSA_HPG, tq, ncp)
    s = jnp.where(cmask, s, NEG)
    m = jnp.max(s, axis=-1, keepdims=True)
    p = jnp.where(cmask, jnp.exp(s - m), 0.0)
    p = p / jnp.maximum(jnp.sum(p, axis=-1, keepdims=True), 1e-30)
    o = jnp.dot(p.reshape(NSA_HPG * tq, ncp).astype(BF16), vc_ref[...], preferred_element_type=F32)
    o_ref[...] = o.reshape(NSA_HPG, tq, NSA_DIM) * _head_gates(gate_ref, pl.program_id(0), 0)

    psum = p[0] + p[1] + p[2] + p[3]
    imp = _dot_exact01(psum, ov_ref[...])
    blk = lax.broadcasted_iota(jnp.int32, (1, nsp), 1)
    cur = q_pos // SLC_BLOCK
    valid = blk <= cur
    forced = valid & ((blk == 0) | (blk > cur - N_LOCAL_BLOCKS))
    score = jnp.where(forced, FORCE_SCORE, jnp.where(valid, imp, NEG))
    blk_f = blk.astype(F32)
    sel = jnp.zeros((tq, nsp), F32)
    for _ in range(k_sel):
        mx = jnp.max(score, axis=-1, keepdims=True)
        idx = jnp.min(jnp.where(score == mx, blk_f, float(nsp)), axis=-1, keepdims=True)
        hit = blk_f == idx
        sel = jnp.where(hit, 1.0, sel)
        score = jnp.where(hit, -jnp.inf, score)
    sel_ref[...] = sel.astype(sel_ref.dtype)


def _cmp_select(q, kc, vc, overlap, gate, batch, seq, k_sel):
    tq = CMP_TQ
    nq = seq // tq
    ncp = kc.shape[2]
    nsp = overlap.shape[1]
    kern = functools.partial(_cmp_select_kernel, tq=tq, k_sel=k_sel)
    return pl.pallas_call(
        kern,
        grid=(NSA_GROUPS, batch, nq),
        in_specs=[pl.BlockSpec((NSA_HPG, tq, NSA_DIM), lambda g, b, i: (g, b * nq + i, 0)),
                  pl.BlockSpec((None, None, ncp, NSA_DIM), lambda g, b, i: (g, b, 0, 0)),
                  pl.BlockSpec((None, None, ncp, NSA_DIM), lambda g, b, i: (g, b, 0, 0)),
                  pl.BlockSpec((ncp, nsp), lambda g, b, i: (0, 0)),
                  pl.BlockSpec((tq, gate.shape[1]), lambda g, b, i: (b * nq + i, 0))],
        out_specs=[pl.BlockSpec((NSA_HPG, tq, NSA_DIM), lambda g, b, i: (g, b * nq + i, 0)),
                   pl.BlockSpec((None, tq, nsp), lambda g, b, i: (g, b * nq + i, 0))],
        out_shape=[jax.ShapeDtypeStruct((NSA_HEADS, batch * seq, NSA_DIM), F32),
                   jax.ShapeDtypeStruct((NSA_GROUPS, batch * seq, nsp), BF16)],
        compiler_params=_cparams(3),
        name="nsa_cmp_select",
    )(q, kc, vc, overlap, gate)


def _window_kernel(q_ref, k_ref, v_ref, bias_ref, gate_ref, o_ref, *, tq, span):
    qi = pl.program_id(2)
    q0 = pl.multiple_of(qi * tq, tq)
    q = q_ref[...].reshape(NSA_HPG * tq, NSA_DIM)
    ks = k_ref[pl.ds(q0, span), :]
    vs = v_ref[pl.ds(q0, span), :]
    s = _dot_nt(q, ks).reshape(NSA_HPG, tq, span) + bias_ref[...]
    in_seq = (lax.broadcasted_iota(jnp.int32, (1, 1, span), 2) + q0) >= WINDOW
    s = jnp.where(in_seq, s, NEG)
    m = jnp.max(s, axis=-1, keepdims=True)
    p = jnp.exp(s - m)
    l = jnp.sum(p, axis=-1, keepdims=True)
    o = jnp.dot(p.reshape(NSA_HPG * tq, span).astype(BF16), vs, preferred_element_type=F32)
    o = o.reshape(NSA_HPG, tq, NSA_DIM) / l
    o_ref[...] = o * _head_gates(gate_ref, pl.program_id(0), 2)


def _window_attention(q, kp, vp, bias, gate, batch, seq):
    tq = WIN_TQ
    span = WINDOW + tq
    nq = seq // tq
    kern = functools.partial(_window_kernel, tq=tq, span=span)
    return pl.pallas_call(
        kern,
        grid=(NSA_GROUPS, batch, nq),
        in_specs=[pl.BlockSpec((NSA_HPG, tq, NSA_DIM), lambda g, b, i: (g, b * nq + i, 0)),
                  pl.BlockSpec((None, None, WINDOW + seq, NSA_DIM), lambda g, b, i: (g, b, 0, 0)),
                  pl.BlockSpec((None, None, WINDOW + seq, NSA_DIM), lambda g, b, i: (g, b, 0, 0)),
                  pl.BlockSpec((NSA_HPG, tq, span), lambda g, b, i: (g, 0, 0)),
                  pl.BlockSpec((tq, gate.shape[1]), lambda g, b, i: (b * nq + i, 0))],
        out_specs=pl.BlockSpec((NSA_HPG, tq, NSA_DIM), lambda g, b, i: (g, b * nq + i, 0)),
        out_shape=jax.ShapeDtypeStruct((NSA_HEADS, batch * seq, NSA_DIM), F32),
        compiler_params=_cparams(3),
        name="nsa_window",
    )(q, kp, vp, bias, gate)


def _slc_kernel(q_ref, k_ref, v_ref, sel_ref, bias_ref, gate_ref, oc_ref, ow_ref, o_ref,
                m_scr, acc_scr, s_scr, *, t):
    g = pl.program_id(0)
    qi = pl.program_id(2)
    nsp = sel_ref.shape[-1]
    rows = NSA_HPG * t
    q = q_ref[...].reshape(rows, NSA_DIM)
    sel = sel_ref[...]
    _flash_init(m_scr, acc_scr)
    blk_row = lax.broadcasted_iota(jnp.int32, (nsp, t), 0)
    key_blk = lax.broadcasted_iota(jnp.int32, (nsp, t), 1) // SLC_BLOCK

    def logits(j):
        return _dot_nt(q, k_ref[pl.ds(pl.multiple_of(j * t, t), t), :])

    def consume(j, slot):
        expand = jnp.where(blk_row == key_blk + j * (t // SLC_BLOCK), 1.0, 0.0).astype(BF16)
        off = (jnp.dot(sel, expand, preferred_element_type=F32) - 1.0) * (-NEG)
        s = s_scr[slot].reshape(NSA_HPG, t, t) + (off[None] + bias_ref[_near_kind(j, qi)])
        v1 = v_ref[pl.ds(pl.multiple_of(j * t, t), t), :]
        _flash_update(s.reshape(rows, t), v1, m_scr, acc_scr)

    _pipelined_sweep(qi + 1, logits, consume, s_scr)

    acc = acc_scr[...]
    o = (acc[:, :NSA_DIM] / acc[:, NSA_DIM:]).reshape(NSA_HPG, t, NSA_DIM)
    o = oc_ref[...] + ow_ref[...] + o * _head_gates(gate_ref, g, 1)
    o_ref[...] = o.astype(o_ref.dtype)


def _slc_attention(q, ks, vs1, sel, bias, gate, o_cmp, o_win, batch, seq):
    t = SLC_TILE
    nq = seq // t
    nsp = sel.shape[-1]
    kern = functools.partial(_slc_kernel, t=t)
    head_blk = lambda last: pl.BlockSpec((NSA_HPG, t, last), lambda g, b, i: (g, b * nq + i, 0))
    return pl.pallas_call(
        kern,
        grid=(NSA_GROUPS, batch, nq),
        in_specs=[head_blk(NSA_DIM),
                  pl.BlockSpec((None, None, seq, NSA_DIM), lambda g, b, i: (g, b, 0, 0)),
                  pl.BlockSpec((None, None, seq, 2 * NSA_DIM), lambda g, b, i: (g, b, 0, 0)),
                  pl.BlockSpec((None, t, nsp), lambda g, b, i: (g, b * nq + i, 0)),
                  pl.BlockSpec((3, NSA_HPG, t, t), lambda g, b, i: (0, g, 0, 0), pipeline_mode=pl.Buffered(1)),
                  pl.BlockSpec((t, gate.shape[1]), lambda g, b, i: (b * nq + i, 0)),
                  head_blk(NSA_DIM), head_blk(NSA_DIM)],
        out_specs=head_blk(NSA_DIM),
        out_shape=jax.ShapeDtypeStruct((NSA_HEADS, batch * seq, NSA_DIM), BF16),
        scratch_shapes=[pltpu.VMEM((NSA_HPG * t, 1), F32), pltpu.VMEM((NSA_HPG * t, 2 * NSA_DIM), F32),
                        pltpu.VMEM((2, NSA_HPG * t, t), F32)],
        compiler_params=_cparams(3),
        name="nsa_selected",
    )(q, ks, vs1, sel, bias, gate, o_cmp, o_win)


def _merge_kernel(oda_ref, onsa_ref, wda_ref, wnsa_ref, gda_ref, gnsa_ref, o_ref):
    a = jnp.dot(oda_ref[...], wda_ref[...], preferred_element_type=F32)
    n = jnp.dot(onsa_ref[...], wnsa_ref[...], preferred_element_type=F32)
    mixed = jax.nn.sigmoid(gda_ref[...]) * a + jax.nn.sigmoid(gnsa_ref[...]) * n
    o_ref[...] = mixed.astype(o_ref.dtype)


def _merge(o_da, o_nsa, w_da, w_nsa, gates):
    tokens, kd = o_da.shape
    d = w_da.shape[1]
    tm, tn = 512, 512
    nb = d // tn
    return pl.pallas_call(
        _merge_kernel,
        grid=(tokens // tm, nb),
        in_specs=[pl.BlockSpec((tm, kd), lambda i, j: (i, 0)),
                  pl.BlockSpec((tm, kd), lambda i, j: (i, 0)),
                  pl.BlockSpec((kd, tn), lambda i, j: (0, j)),
                  pl.BlockSpec((kd, tn), lambda i, j: (0, j)),
                  pl.BlockSpec((tm, tn), lambda i, j: (i, j)),
                  pl.BlockSpec((tm, tn), lambda i, j: (i, nb + j))],
        out_specs=pl.BlockSpec((tm, tn), lambda i, j: (i, j)),
        out_shape=jax.ShapeDtypeStruct((tokens, d), BF16),
        compiler_params=_cparams(2),
        name="gated_merge",
    )(o_da, o_nsa, w_da, w_nsa, gates, gates)


def _layer_norm(z, g, b):
    mu = jnp.mean(z, axis=-1, keepdims=True)
    zc = z - mu
    var = jnp.mean(zc * zc, axis=-1, keepdims=True)
    return (zc * lax.rsqrt(var + LN_EPS)) * g + b


def _outproj_ln_kernel(mixed_ref, w_ref, x_ref, g_ref, b_ref, h_ref, hb_ref):
    z = DN_ALPHA * x_ref[...] + jnp.dot(mixed_ref[...], w_ref[...], preferred_element_type=F32)
    h = _layer_norm(z, g_ref[...], b_ref[...])
    h_ref[...] = h
    hb_ref[...] = h.astype(hb_ref.dtype)


def _outproj_ln(mixed, w_out, x, g, b):
    tokens, d = x.shape
    tm = 256
    row = pl.BlockSpec((tm, d), lambda i: (i, 0))
    vec = pl.BlockSpec((1, d), lambda i: (0, 0))
    return pl.pallas_call(
        _outproj_ln_kernel,
        grid=(tokens // tm,),
        in_specs=[row, pl.BlockSpec((d, d), lambda i: (0, 0)), row, vec, vec],
        out_specs=[row, row],
        out_shape=[jax.ShapeDtypeStruct((tokens, d), F32), jax.ShapeDtypeStruct((tokens, d), BF16)],
        compiler_params=_cparams(1),
        name="outproj_ln1",
    )(mixed, w_out, x, g, b)


def _topk_rows(x, k):
    n, t = x.shape
    rows = lax.broadcasted_iota(jnp.int32, (n, t), 0).astype(F32)
    slot = lax.broadcasted_iota(jnp.int32, (k, t), 0)
    vals = jnp.zeros((k, t), F32)
    idxs = jnp.zeros((k, t), F32)
    for r in range(k):
        mx = jnp.max(x, axis=0, keepdims=True)
        idx = jnp.min(jnp.where(x == mx, rows, float(n)), axis=0, keepdims=True)
        vals = jnp.where(slot == r, mx, vals)
        idxs = jnp.where(slot == r, idx, idxs)
        x = jnp.where(rows == idx, -jnp.inf, x)
    return vals, idxs


def _peer_topk_kernel(q_ref, sk_ref, gate_ref, eid_ref, *, tm):
    k = PEER_TOPK
    nk = PEER_NKEYS
    for h in range(PEER_HEADS):
        qh = q_ref[:, h * LANES:(h + 1) * LANES]
        st = _dot_nt(sk_ref[...], qh)
        v1, i1 = _topk_rows(st[:nk], k)
        v2, i2 = _topk_rows(st[nk:], k)
        cand = jnp.concatenate([v1[a:a + 1] + v2 for a in range(k)], axis=0)
        cid = jnp.concatenate([i1[a:a + 1] * float(nk) + i2 for a in range(k)], axis=0)
        sc, j = _topk_rows(cand, k)
        rows = lax.broadcasted_iota(jnp.int32, (k * k, tm), 0).astype(F32)
        slot = lax.broadcasted_iota(jnp.int32, (k, tm), 0)
        eid = jnp.zeros((k, tm), F32)
        for r in range(k):
            picked = jnp.max(jnp.where(rows == j[r:r + 1], cid, -1.0), axis=0, keepdims=True)
            eid = jnp.where(slot == r, picked, eid)
        e = jnp.exp(sc - sc[0:1])
        gate_ref[h] = e / jnp.sum(e, axis=0, keepdims=True)
        eid_ref[h] = eid.astype(jnp.int32)


def _peer_topk(q, sk):
    tokens = q.shape[0]
    tm = 256
    kern = functools.partial(_peer_topk_kernel, tm=tm)
    out_blk = pl.BlockSpec((PEER_HEADS, PEER_TOPK, tm), lambda i: (0, 0, i))
    return pl.pallas_call(
        kern,
        grid=(tokens // tm,),
        in_specs=[pl.BlockSpec((tm, PEER_HEADS * LANES), lambda i: (i, 0)),
                  pl.BlockSpec((2 * PEER_NKEYS, LANES), lambda i: (0, 0))],
        out_specs=[out_blk, out_blk],
        out_shape=[jax.ShapeDtypeStruct((PEER_HEADS, PEER_TOPK, tokens), F32),
                   jax.ShapeDtypeStruct((PEER_HEADS, PEER_TOPK, tokens), jnp.int32)],
        compiler_params=_cparams(1),
        name="peer_topk",
    )(q, sk)


def _pack_bf16_pairs(t):
    bits = lax.bitcast_convert_type(t.astype(BF16), jnp.uint16).astype(jnp.uint32)
    half = t.shape[1] // 2
    return bits[:, :half] | (bits[:, half:] << 16)


def _unpack_bf16_pairs(w):
    return pltpu.bitcast(w << 16, F32), pltpu.bitcast(w & jnp.uint32(0xFFFF0000), F32)


def _peer_gather_kernel(eid_hbm, uv_hbm, gate_ref, h_ref, g_ref, b_ref, o_ref, idx_smem, acc, idx_sem, row_sem,
                        *rows, tt, d, tokens):
    i = pl.program_id(0)
    nsteps = pl.num_programs(0)
    nsel = PEER_HEADS * PEER_TOPK
    ns = len(rows)
    per = tt * nsel
    half = d // 2
    nchunk = half // LANES

    def idx_copy(step):
        dst = idx_smem.at[pl.ds(pl.multiple_of((step % 2) * per, per), per)]
        return pltpu.make_async_copy(eid_hbm.at[step], dst, idx_sem.at[step % 2])

    def issue(gtok, slot, first=0, count=None):
        base = (jnp.minimum(gtok, tokens - 1) % (2 * tt)) * nsel
        for j in range(first, first + (nsel if count is None else count)):
            e = idx_smem[base + j]
            pltpu.make_async_copy(uv_hbm.at[e], rows[slot].at[j], row_sem.at[slot]).start()

    def wait_slot(slot):
        pltpu.make_async_copy(rows[slot], rows[slot], row_sem.at[slot]).wait()

    @pl.when(i == 0)
    def _():
        first = idx_copy(0)
        first.start()
        first.wait()
        for s in range(ns - 1):
            issue(s, s)

    @pl.when(i + 1 < nsteps)
    def _():
        idx_copy(i + 1).start()

    lane = lax.broadcasted_iota(jnp.int32, (nsel, tt), 1)

    per_chunk = nsel // (2 * nchunk)

    def compute_and_issue(tok, slot, ahead, ahead_slot):
        buf = rows[slot]
        y = h_ref[pl.ds(tok, 1), :]
        part = None
        for c in range(nchunk):
            issue(ahead, ahead_slot, c * per_chunk, per_chunk)
            lo, hi = _unpack_bf16_pairs(buf[:, c * LANES:(c + 1) * LANES])
            term = lo * y[:, c * LANES:(c + 1) * LANES] + hi * y[:, half + c * LANES:half + (c + 1) * LANES]
            part = term if part is None else part + term
        act = jax.nn.gelu(jnp.sum(part, axis=-1, keepdims=True))
        gcol = jnp.sum(jnp.where(lane == tok, gate_ref[...], 0.0), axis=-1, keepdims=True)
        w = gcol * act
        out_lo, out_hi = [], []
        for c in range(nchunk):
            issue(ahead, ahead_slot, (nchunk + c) * per_chunk, per_chunk)
            lo, hi = _unpack_bf16_pairs(buf[:, half + c * LANES:half + (c + 1) * LANES])
            out_lo.append(jnp.sum(w * lo, axis=0, keepdims=True))
            out_hi.append(jnp.sum(w * hi, axis=0, keepdims=True))
        acc[pl.ds(tok, 1), :] = jnp.concatenate(out_lo + out_hi, axis=-1)

    n_groups = tt // ns

    def body(grp, carry):
        @pl.when((grp == n_groups - 1) & (i + 1 < nsteps))
        def _():
            idx_copy(i + 1).wait()

        for s in range(ns):
            tok = grp * ns + s
            wait_slot(s)
            compute_and_issue(tok, s, i * tt + tok + ns - 1, (s + ns - 1) % ns)
        return carry

    lax.fori_loop(0, n_groups, body, 0)

    @pl.when(i == nsteps - 1)
    def _():
        for s in range(ns - 1):
            wait_slot((tt + s) % ns)

    z = DN_ALPHA * h_ref[...] + acc[...]
    o_ref[...] = _layer_norm(z, g_ref[...], b_ref[...])


def _peer_gather(eid, uv, gate, h, g, b):
    tokens, d = h.shape
    tt = PEER_TT
    nsel = PEER_HEADS * PEER_TOPK
    kern = functools.partial(_peer_gather_kernel, tt=tt, d=d, tokens=tokens)
    vec = pl.BlockSpec((1, d), lambda i: (0, 0))
    return pl.pallas_call(
        kern,
        grid=(tokens // tt,),
        in_specs=[pl.BlockSpec(memory_space=pl.ANY),
                  pl.BlockSpec(memory_space=pl.ANY),
                  pl.BlockSpec((nsel, tt), lambda i: (0, i)),
                  pl.BlockSpec((tt, d), lambda i: (i, 0)),
                  vec, vec],
        out_specs=pl.BlockSpec((tt, d), lambda i: (i, 0)),
        out_shape=jax.ShapeDtypeStruct((tokens, d), F32),
        scratch_shapes=[pltpu.SMEM((2 * tt * nsel,), jnp.int32),
                        pltpu.VMEM((tt, d), F32),
                        pltpu.SemaphoreType.DMA((2,)),
                        pltpu.SemaphoreType.DMA((PEER_SLOTS,))]
                       + [pltpu.VMEM((nsel, d), jnp.uint32) for _ in range(PEER_SLOTS)],
        compiler_params=_cparams(1),
        name="peer_gather_ln2",
    )(eid, uv, gate, h, g, b)


def _heads_major(t, heads, width):
    return t.reshape(t.shape[0], heads, width).transpose(1, 0, 2)


def kernel(x, w_in, da_lam_q, da_lam_k, da_subln_g, cmp_pe_k, cmp_w1_k, cmp_w2_k, cmp_pe_v, cmp_w1_v,
           cmp_w2_v, w_branch_da, w_branch_nsa, w_out, ln1_g, ln1_b, peer_wq, peer_subkey1, peer_subkey2,
           peer_u, peer_v, ln2_g, ln2_b, rel_bias):
    batch, seq, d_model = x.shape
    tokens = batch * seq
    g, hd = NSA_GROUPS, NSA_DIM
    table = rel_bias.astype(F32)
    xs = x.reshape(tokens, d_model)
    for l in range(DEPTH):
        lam_init = 0.8 - 0.6 * math.exp(-0.3 * l)
        xb = xs.astype(BF16)
        w = w_in[l]
        scale = DA_HEAD_DIM ** -0.5
        c_daq, c_dak, c_dav, c_nq = 0, 1024, 2048, 3072
        c_kv, c_gate, c_mg, c_end = 4096, 5632, 5680, 9776
        w_att = jnp.concatenate([w[:, c_daq:c_dak] * scale, w[:, c_dak:c_nq], w[:, c_nq:c_kv] * scale],
                                axis=1).astype(BF16)
        att = _matmul(xb, w_att, BF16, 512, 512)
        kv = _matmul(xb, w[:, c_kv:c_gate].astype(BF16), F32, 512, 512)
        w_gate = jnp.pad(w[:, c_gate:c_mg], ((0, 0), (0, LANES - (c_mg - c_gate)))).astype(BF16)
        br_gate = _matmul(xb, w_gate, F32, 512, LANES)[:, :c_mg - c_gate]
        mg_gate = _matmul(xb, w[:, c_mg:c_end].astype(BF16), F32, 512, 512)

        lam_e = jnp.exp(jnp.sum(da_lam_q[l].astype(F32) * da_lam_k[l].astype(F32), -1))
        lam = (lam_e[0] - lam_e[1] + lam_init).reshape(1)
        table_rel = table - table[REL_BUCKETS - 1]
        o_da = _diff_attention(att, lam, _causal_bias_tiles(table_rel[:, :DA_HEADS], DA_TILE),
                               da_subln_g[l].reshape(1, DA_V_DIM), batch, seq, lam_init)

        q_n = _heads_major(att[:, 3072:4096], NSA_HEADS, hd)
        kv6 = kv.reshape(tokens, 6, g * hd)
        kc = _compress(kv6[:, 0], cmp_pe_k[l], cmp_w1_k[l], cmp_w2_k[l], batch, seq)
        vc = _compress(kv6[:, 1], cmp_pe_v[l], cmp_w1_v[l], cmp_w2_v[l], batch, seq)
        grp = lambda t: t.astype(BF16).reshape(batch, seq, g, hd).transpose(2, 0, 1, 3)
        k_s, v_s, k_w, v_w = grp(kv6[:, 2]), grp(kv6[:, 3]), grp(kv6[:, 4]), grp(kv6[:, 5])

        ncp = seq // CMP_STRIDE
        ns = seq // SLC_BLOCK
        nsp = -(-ns // LANES) * LANES
        k_sel = min(SLC_TOPK, ns)
        cmp_start = jnp.arange(ncp) * CMP_STRIDE
        slc_start = jnp.arange(nsp) * SLC_BLOCK
        overlap = ((cmp_start[:, None] <= slc_start[None, :] + SLC_BLOCK - 1)
                   & (cmp_start[:, None] + CMP_BLOCK - 1 >= slc_start[None, :])).astype(BF16)
        o_cmp, sel = _cmp_select(q_n, kc, vc, overlap, br_gate, batch, seq, k_sel)

        win_bias = _window_bias_tile(table[:, DA_HEADS:], WIN_TQ, WINDOW + WIN_TQ)
        pad_w = lambda t: jnp.pad(t, ((0, 0), (0, 0), (WINDOW, 0), (0, 0)))
        o_win = _window_attention(q_n, pad_w(k_w), pad_w(v_w), win_bias, br_gate, batch, seq)

        v_s1 = jnp.concatenate([v_s, jnp.ones_like(v_s)], axis=-1)
        o_nsa = _slc_attention(q_n, k_s, v_s1, sel, _causal_bias_tiles(table_rel[:, DA_HEADS:], SLC_TILE),
                               br_gate, o_cmp, o_win, batch, seq)
        o_nsa = o_nsa.transpose(1, 0, 2).reshape(tokens, NSA_HEADS * hd)

        mixed = _merge(o_da, o_nsa, w_branch_da[l].astype(BF16), w_branch_nsa[l].astype(BF16), mg_gate)
        h, hb = _outproj_ln(mixed, w_out[l].astype(BF16), xs, ln1_g[l].reshape(1, -1), ln1_b[l].reshape(1, -1))

        pq = _matmul(hb, peer_wq[l].astype(BF16), BF16, 512, 512)
        half = peer_subkey1.shape[-1]
        zeros = jnp.zeros((PEER_NKEYS, half), F32)
        sk = jnp.concatenate([jnp.concatenate([peer_subkey1[l], zeros], axis=1),
                              jnp.concatenate([zeros, peer_subkey2[l]], axis=1)], axis=0).astype(BF16)
        gate, eid = _peer_topk(pq, sk)
        nsel = PEER_HEADS * PEER_TOPK
        eid_tok = eid.reshape(nsel, tokens).T.reshape(tokens // PEER_TT, PEER_TT * nsel)
        uv = jnp.concatenate([_pack_bf16_pairs(peer_u[l]), _pack_bf16_pairs(peer_v[l])], axis=1)
        xs = _peer_gather(eid_tok, uv, gate.reshape(nsel, tokens), h,
                          ln2_g[l].reshape(1, -1), ln2_b[l].reshape(1, -1))
    return xs.reshape(batch, seq, d_model)
```

```python
import functools
import math

import jax
import jax.numpy as jnp
from jax import lax
from jax.experimental import pallas as pl
from jax.experimental.pallas import tpu as pltpu

F32 = jnp.float32
BF16 = jnp.bfloat16

DA_HEADS = 8
DA_HEAD_DIM = 64
DA_V_DIM = 128
NSA_HEADS = 16
NSA_GROUPS = 4
NSA_HPG = 4
NSA_DIM = 64
CMP_BLOCK = 32
CMP_STRIDE = 16
SLC_BLOCK = 64
SLC_TOPK = 16
N_LOCAL_BLOCKS = 2
WINDOW = 512
REL_BUCKETS = 32
REL_MAX_DIST = 128
PEER_HEADS = 8
PEER_NKEYS = 128
PEER_TOPK = 16
DEPTH = 1
DN_ALPHA = (2 * DEPTH) ** 0.25
LN_EPS = 1e-5
NEG = -1e30
FORCE_SCORE = 1e4

LANES = 128
VMEM_LIMIT = 48 * 1024 * 1024
DA_TILE = 512
SLC_TILE = 512
WIN_TQ = 256
CMP_TQ = 512
PEER_TT = 128
PEER_SLOTS = 8


def _cparams(n_axes):
    return pltpu.CompilerParams(dimension_semantics=("arbitrary",) * n_axes,
                                vmem_limit_bytes=VMEM_LIMIT)


def _dot_nt(a, b):
    return lax.dot_general(a, b, (((1,), (1,)), ((), ())), preferred_element_type=F32)


def _dot_exact01(x, onehot_bf16):
    hi = x.astype(BF16)
    r1 = x - hi.astype(F32)
    mid = r1.astype(BF16)
    lo = (r1 - mid.astype(F32)).astype(BF16)
    d = lambda a: jnp.dot(a, onehot_bf16, preferred_element_type=F32)
    return d(hi) + d(mid) + d(lo)


def _mm_kernel(a_ref, b_ref, o_ref):
    o_ref[...] = jnp.dot(a_ref[...], b_ref[...], preferred_element_type=F32).astype(o_ref.dtype)


def _matmul(a, b, out_dtype, tm, tn):
    m, k = a.shape
    n = b.shape[1]
    return pl.pallas_call(
        _mm_kernel,
        grid=(n // tn, m // tm),
        in_specs=[pl.BlockSpec((tm, k), lambda j, i: (i, 0)),
                  pl.BlockSpec((k, tn), lambda j, i: (0, j))],
        out_specs=pl.BlockSpec((tm, tn), lambda j, i: (i, j)),
        out_shape=jax.ShapeDtypeStruct((m, n), out_dtype),
        compiler_params=_cparams(2),
        name="matmul",
    )(a, b)


def _rel_bucket(dist):
    n = jnp.maximum(dist, 0)
    max_exact = REL_BUCKETS // 2
    nf = jnp.maximum(n, 1).astype(F32)
    large = max_exact + (jnp.log(nf / max_exact) / math.log(REL_MAX_DIST / max_exact)
                         * (REL_BUCKETS - max_exact)).astype(jnp.int32)
    large = jnp.minimum(large, REL_BUCKETS - 1)
    return jnp.where(n < max_exact, n, large)


def _toeplitz(rd, n_i, n_j):
    h, length = rd.shape
    rev = jnp.pad(rd[:, ::-1], ((0, 0), (0, 1)))
    flat = jnp.broadcast_to(rev[:, None, :], (h, n_i, length + 1)).reshape(h, n_i * (length + 1))
    skew = flat[:, :n_i * length].reshape(h, n_i, length)
    return skew[:, :, n_i - 1:n_i - 1 + n_j]


def _bias_by_distance(table, n):
    return table[_rel_bucket(jnp.arange(n))].T


def _causal_bias_tiles(table, t):
    bd = _bias_by_distance(table, 2 * t)
    neg = jnp.full((bd.shape[0], t - 1), NEG, F32)
    diag = _toeplitz(jnp.concatenate([neg, bd[:, :t]], axis=1), t, t)
    off = _toeplitz(bd[:, 1:], t, t)
    return jnp.stack([jnp.zeros_like(off), off, diag], axis=0)


def _window_bias_tile(table, tq, span):
    bd = _bias_by_distance(table, WINDOW)
    h = bd.shape[0]
    lo = jnp.full((h, span - 1 - WINDOW), NEG, F32)
    hi = jnp.full((h, tq), NEG, F32)
    return _toeplitz(jnp.concatenate([lo, bd, hi], axis=1), tq, span)


def _flash_init(m_scr, acc_scr):
    m_scr[...] = jnp.full(m_scr.shape, NEG, F32)
    acc_scr[...] = jnp.zeros(acc_scr.shape, F32)


def _flash_update(s, v1, m_scr, acc_scr):
    m_old = m_scr[...]
    m_new = jnp.maximum(m_old, jnp.broadcast_to(jnp.max(s, axis=-1, keepdims=True), m_old.shape))
    alpha = jnp.exp(m_old - m_new)
    widen = lambda x, width: jnp.concatenate([x] * (width // LANES), axis=1)
    p = jnp.exp(s - widen(m_new, s.shape[1])).astype(BF16)
    acc = acc_scr[...]
    acc_scr[...] = widen(alpha, acc.shape[1]) * acc + jnp.dot(p, v1, preferred_element_type=F32)
    m_scr[...] = m_new


def _pipelined_sweep(n_tiles, logits_fn, consume_fn, s_scr):
    last = n_tiles - 1
    s_scr[0] = logits_fn(0)

    def body(pair, carry):
        a = 2 * pair
        s_scr[1] = logits_fn(a + 1)
        consume_fn(a, 0)
        s_scr[0] = logits_fn(jnp.minimum(a + 2, last))
        consume_fn(a + 1, 1)
        return carry

    lax.fori_loop(0, n_tiles // 2, body, 0)

    @pl.when(n_tiles % 2 == 1)
    def _():
        consume_fn(last, 0)


def _near_kind(j, qi):
    return jnp.maximum(j - (qi - 2), 0)


def _head_gates(gate_ref, group, branch):
    sig = jax.nn.sigmoid(gate_ref[...])
    lane = lax.broadcasted_iota(jnp.int32, (1, sig.shape[-1]), 1)
    cols = [jnp.sum(jnp.where(lane == (group * NSA_HPG + h) * 3 + branch, sig, 0.0), axis=-1, keepdims=True)
            for h in range(NSA_HPG)]
    return jnp.stack(cols)


def _da_kernel(lam_ref, q_ref, k_ref, v_ref, bias_ref, g_ref, o_ref, m_scr, acc_scr, s_scr, *, t, lam_init):
    qi = pl.program_id(2)
    q = q_ref[...]
    lane = lax.broadcasted_iota(jnp.int32, (1, LANES), 1)
    zero = jnp.zeros_like(q)
    q2 = jnp.concatenate([jnp.where(lane < DA_HEAD_DIM, q, zero), jnp.where(lane >= DA_HEAD_DIM, q, zero)], axis=0)
    _flash_init(m_scr, acc_scr)

    def logits(j):
        return _dot_nt(q2, k_ref[pl.ds(pl.multiple_of(j * t, t), t), :])

    def consume(j, slot):
        s = (s_scr[slot].reshape(2, t, t) + bias_ref[_near_kind(j, qi)][None]).reshape(2 * t, t)
        v = v_ref[pl.ds(pl.multiple_of(j * t, t), t), :]
        v1 = jnp.concatenate([v, jnp.ones((t, LANES), BF16)], axis=-1)
        _flash_update(s, v1, m_scr, acc_scr)

    _pipelined_sweep(qi + 1, logits, consume, s_scr)

    acc = acc_scr[...]
    o = acc[:t, :DA_V_DIM] / acc[:t, DA_V_DIM:] - lam_ref[0] * (acc[t:, :DA_V_DIM] / acc[t:, DA_V_DIM:])
    ms = jnp.mean(o * o, axis=-1, keepdims=True)
    o = (o * lax.rsqrt(ms + LN_EPS)) * g_ref[...] * (1.0 - lam_init)
    o_ref[...] = o.astype(o_ref.dtype)


def _diff_attention(att, lam, bias, subln_g, batch, seq, lam_init):
    t = DA_TILE
    nq = seq // t
    kern = functools.partial(_da_kernel, t=t, lam_init=lam_init)
    return pl.pallas_call(
        kern,
        grid=(batch, DA_HEADS, nq),
        in_specs=[pl.BlockSpec(memory_space=pltpu.SMEM),
                  pl.BlockSpec((t, LANES), lambda b, h, i: (b * nq + i, h)),
                  pl.BlockSpec((seq, LANES), lambda b, h, i: (b, DA_HEADS + h)),
                  pl.BlockSpec((seq, LANES), lambda b, h, i: (b, 2 * DA_HEADS + h)),
                  pl.BlockSpec((3, None, t, t), lambda b, h, i: (0, h, 0, 0)),
                  pl.BlockSpec((1, DA_V_DIM), lambda b, h, i: (0, 0))],
        out_specs=pl.BlockSpec((t, DA_V_DIM), lambda b, h, i: (b * nq + i, h)),
        out_shape=jax.ShapeDtypeStruct((batch * seq, DA_HEADS * DA_V_DIM), BF16),
        scratch_shapes=[pltpu.VMEM((2 * t, LANES), F32), pltpu.VMEM((2 * t, 2 * DA_V_DIM), F32),
                        pltpu.VMEM((2, 2 * t, t), F32)],
        compiler_params=_cparams(3),
        name="diff_attention",
    )(lam, att, att, att, bias, subln_g)


def _compress_kernel(x_ref, pe_ref, w1_ref, w2_ref, o_ref):
    x = (x_ref[...].astype(F32) + pe_ref[...]).astype(BF16)
    hid = jax.nn.gelu(jnp.dot(x, w1_ref[...], preferred_element_type=F32))
    o_ref[...] = jnp.dot(hid.astype(BF16), w2_ref[...], preferred_element_type=F32).astype(o_ref.dtype)


def _compress(t, pe, w1, w2, batch, seq):
    g, d = NSA_GROUPS, NSA_DIM
    r = CMP_BLOCK // CMP_STRIDE
    nch = seq // CMP_STRIDE
    nc = nch - r + 1
    ch = t.reshape(batch, nch, CMP_STRIDE, g, d)
    blocks = jnp.concatenate([ch[:, j:j + nc] for j in range(r)], axis=2)
    flat = blocks.transpose(0, 1, 3, 2, 4).reshape(batch * nc * g, CMP_BLOCK * d)
    rows = flat.shape[0]
    tm = 512
    rows_p = -(-rows // tm) * tm
    flat = jnp.pad(flat, ((0, rows_p - rows), (0, 0)))
    hidden = w1.shape[1]
    out = pl.pallas_call(
        _compress_kernel,
        grid=(rows_p // tm,),
        in_specs=[pl.BlockSpec((tm, CMP_BLOCK * d), lambda i: (i, 0)),
                  pl.BlockSpec((1, CMP_BLOCK * d), lambda i: (0, 0)),
                  pl.BlockSpec((CMP_BLOCK * d, hidden), lambda i: (0, 0)),
                  pl.BlockSpec((hidden, d), lambda i: (0, 0))],
        out_specs=pl.BlockSpec((tm, d), lambda i: (i, 0)),
        out_shape=jax.ShapeDtypeStruct((rows_p, d), BF16),
        compiler_params=_cparams(1),
        name="compress_mlp",
    )(flat, pe.reshape(1, CMP_BLOCK * d), w1.astype(BF16), w2.astype(BF16))
    out = out[:rows].reshape(batch, nc, g, d).transpose(2, 0, 1, 3)
    return jnp.pad(out, ((0, 0), (0, 0), (0, nch - nc), (0, 0)))


def _cmp_select_kernel(q_ref, kc_ref, vc_ref, ov_ref, gate_ref, o_ref, sel_ref, *, tq, k_sel):
    qi = pl.program_id(2)
    ncp = kc_ref.shape[0]
    nsp = sel_ref.shape[-1]
    q = q_ref[...].reshape(NSA_HPG * tq, NSA_DIM)
    q_pos = qi * tq + lax.broadcasted_iota(jnp.int32, (tq, 1), 0)
    c_end = lax.broadcasted_iota(jnp.int32, (1, ncp), 1) * CMP_STRIDE + (CMP_BLOCK - 1)
    cmask = (c_end <= q_pos)[None]
    s = _dot_nt(q, kc_ref[...]).reshape(NSA_HPG, tq, ncp)
    s = jnp.where(cmask, s, NEG)
    m = jnp.max(s, axis=-1, keepdims=True)
    p = jnp.where(cmask, jnp.exp(s - m), 0.0)
    p = p / jnp.maximum(jnp.sum(p, axis=-1, keepdims=True), 1e-30)
    o = jnp.dot(p.reshape(NSA_HPG * tq, ncp).astype(BF16), vc_ref[...], preferred_element_type=F32)
    o_ref[...] = o.reshape(NSA_HPG, tq, NSA_DIM) * _head_gates(gate_ref, pl.program_id(0), 0)

    psum = p[0] + p[1] + p[2] + p[3]
    imp = _dot_exact01(psum, ov_ref[...])
    blk = lax.broadcasted_iota(jnp.int32, (1, nsp), 1)
    cur = q_pos // SLC_BLOCK
    valid = blk <= cur
    forced = valid & ((blk == 0) | (blk > cur - N_LOCAL_BLOCKS))
    score = jnp.where(forced, FORCE_SCORE, jnp.where(valid, imp, NEG))
    blk_f = blk.astype(F32)
    sel = jnp.zeros((tq, nsp), F32)
    for _ in range(k_sel):
        mx = jnp.max(score, axis=-1, keepdims=True)
        idx = jnp.min(jnp.where(score == mx, blk_f, float(nsp)), axis=-1, keepdims=True)
        hit = blk_f == idx
        sel = jnp.where(hit, 1.0, sel)
        score = jnp.where(hit, -jnp.inf, score)
    sel_ref[...] = sel.astype(sel_ref.dtype)


def _cmp_select(q, kc, vc, overlap, gate, batch, seq, k_sel):
    tq = CMP_TQ
    nq = seq // tq
    ncp = kc.shape[2]
    nsp = overlap.shape[1]
    kern = functools.partial(_cmp_select_kernel, tq=tq, k_sel=k_sel)
    return pl.pallas_call(
        kern,
        grid=(NSA_GROUPS, batch, nq),
        in_specs=[pl.BlockSpec((NSA_HPG, tq, NSA_DIM), lambda g, b, i: (g, b * nq + i, 0)),
                  pl.BlockSpec((None, None, ncp, NSA_DIM), lambda g, b, i: (g, b, 0, 0)),
                  pl.BlockSpec((None, None, ncp, NSA_DIM), lambda g, b, i: (g, b, 0, 0)),
                  pl.BlockSpec((ncp, nsp), lambda g, b, i: (0, 0)),
                  pl.BlockSpec((tq, gate.shape[1]), lambda g, b, i: (b * nq + i, 0))],
        out_specs=[pl.BlockSpec((NSA_HPG, tq, NSA_DIM), lambda g, b, i: (g, b * nq + i, 0)),
                   pl.BlockSpec((None, tq, nsp), lambda g, b, i: (g, b * nq + i, 0))],
        out_shape=[jax.ShapeDtypeStruct((NSA_HEADS, batch * seq, NSA_DIM), F32),
                   jax.ShapeDtypeStruct((NSA_GROUPS, batch * seq, nsp), BF16)],
        compiler_params=_cparams(3),
        name="nsa_cmp_select",
    )(q, kc, vc, overlap, gate)


def _window_kernel(q_ref, k_ref, v_ref, bias_ref, gate_ref, o_ref, *, tq, span):
    qi = pl.program_id(2)
    q0 = pl.multiple_of(qi * tq, tq)
    q = q_ref[...].reshape(NSA_HPG * tq, NSA_DIM)
    ks = k_ref[pl.ds(q0, span), :]
    vs = v_ref[pl.ds(q0, span), :]
    s = _dot_nt(q, ks).reshape(NSA_HPG, tq, span) + bias_ref[...]
    in_seq = (lax.broadcasted_iota(jnp.int32, (1, 1, span), 2) + q0) >= WINDOW
    s = jnp.where(in_seq, s, NEG)
    m = jnp.max(s, axis=-1, keepdims=True)
    p = jnp.exp(s - m)
    l = jnp.sum(p, axis=-1, keepdims=True)
    o = jnp.dot(p.reshape(NSA_HPG * tq, span).astype(BF16), vs, preferred_element_type=F32)
    o = o.reshape(NSA_HPG, tq, NSA_DIM) / l
    o_ref[...] = o * _head_gates(gate_ref, pl.program_id(0), 2)


def _window_attention(q, kp, vp, bias, gate, batch, seq):
    tq = WIN_TQ
    span = WINDOW + tq
    nq = seq // tq
    kern = functools.partial(_window_kernel, tq=tq, span=span)
    return pl.pallas_call(
        kern,
        grid=(NSA_GROUPS, batch, nq),
        in_specs=[pl.BlockSpec((NSA_HPG, tq, NSA_DIM), lambda g, b, i: (g, b * nq + i, 0)),
                  pl.BlockSpec((None, None, WINDOW + seq, NSA_DIM), lambda g, b, i: (g, b, 0, 0)),
                  pl.BlockSpec((None, None, WINDOW + seq, NSA_DIM), lambda g, b, i: (g, b, 0, 0)),
                  pl.BlockSpec((NSA_HPG, tq, span), lambda g, b, i: (g, 0, 0)),
                  pl.BlockSpec((tq, gate.shape[1]), lambda g, b, i: (b * nq + i, 0))],
        out_specs=pl.BlockSpec((NSA_HPG, tq, NSA_DIM), lambda g, b, i: (g, b * nq + i, 0)),
        out_shape=jax.ShapeDtypeStruct((NSA_HEADS, batch * seq, NSA_DIM), F32),
        compiler_params=_cparams(3),
        name="nsa_window",
    )(q, kp, vp, bias, gate)


def _slc_kernel(q_ref, k_ref, v_ref, sel_ref, bias_ref, gate_ref, oc_ref, ow_ref, o_ref,
                m_scr, acc_scr, s_scr, *, t):
    g = pl.program_id(0)
    qi = pl.program_id(2)
    nsp = sel_ref.shape[-1]
    rows = NSA_HPG * t
    q = q_ref[...].reshape(rows, NSA_DIM)
    sel = sel_ref[...]
    _flash_init(m_scr, acc_scr)
    blk_row = lax.broadcasted_iota(jnp.int32, (nsp, t), 0)
    key_blk = lax.broadcasted_iota(jnp.int32, (nsp, t), 1) // SLC_BLOCK

    def logits(j):
        return _dot_nt(q, k_ref[pl.ds(pl.multiple_of(j * t, t), t), :])

    def consume(j, slot):
        expand = jnp.where(blk_row == key_blk + j * (t // SLC_BLOCK), 1.0, 0.0).astype(BF16)
        off = (jnp.dot(sel, expand, preferred_element_type=F32) - 1.0) * (-NEG)
        s = s_scr[slot].reshape(NSA_HPG, t, t) + (off[None] + bias_ref[_near_kind(j, qi)])
        v1 = v_ref[pl.ds(pl.multiple_of(j * t, t), t), :]
        _flash_update(s.reshape(rows, t), v1, m_scr, acc_scr)

    _pipelined_sweep(qi + 1, logits, consume, s_scr)

    acc = acc_scr[...]
    o = (acc[:, :NSA_DIM] / acc[:, NSA_DIM:]).reshape(NSA_HPG, t, NSA_DIM)
    o = oc_ref[...] + ow_ref[...] + o * _head_gates(gate_ref, g, 1)
    o_ref[...] = o.astype(o_ref.dtype)


def _slc_attention(q, ks, vs1, sel, bias, gate, o_cmp, o_win, batch, seq):
    t = SLC_TILE
    nq = seq // t
    nsp = sel.shape[-1]
    kern = functools.partial(_slc_kernel, t=t)
    head_blk = lambda last: pl.BlockSpec((NSA_HPG, t, last), lambda g, b, i: (g, b * nq + i, 0))
    return pl.pallas_call(
        kern,
        grid=(NSA_GROUPS, batch, nq),
        in_specs=[head_blk(NSA_DIM),
                  pl.BlockSpec((None, None, seq, NSA_DIM), lambda g, b, i: (g, b, 0, 0)),
                  pl.BlockSpec((None, None, seq, 2 * NSA_DIM), lambda g, b, i: (g, b, 0, 0)),
                  pl.BlockSpec((None, t, nsp), lambda g, b, i: (g, b * nq + i, 0)),
                  pl.BlockSpec((3, NSA_HPG, t, t), lambda g, b, i: (0, g, 0, 0), pipeline_mode=pl.Buffered(1)),
                  pl.BlockSpec((t, gate.shape[1]), lambda g, b, i: (b * nq + i, 0)),
                  head_blk(NSA_DIM), head_blk(NSA_DIM)],
        out_specs=head_blk(NSA_DIM),
        out_shape=jax.ShapeDtypeStruct((NSA_HEADS, batch * seq, NSA_DIM), BF16),
        scratch_shapes=[pltpu.VMEM((NSA_HPG * t, LANES), F32), pltpu.VMEM((NSA_HPG * t, 2 * NSA_DIM), F32),
                        pltpu.VMEM((2, NSA_HPG * t, t), F32)],
        compiler_params=_cparams(3),
        name="nsa_selected",
    )(q, ks, vs1, sel, bias, gate, o_cmp, o_win)


def _merge_kernel(oda_ref, onsa_ref, wda_ref, wnsa_ref, gda_ref, gnsa_ref, o_ref):
    a = jnp.dot(oda_ref[...], wda_ref[...], preferred_element_type=F32)
    n = jnp.dot(onsa_ref[...], wnsa_ref[...], preferred_element_type=F32)
    mixed = jax.nn.sigmoid(gda_ref[...]) * a + jax.nn.sigmoid(gnsa_ref[...]) * n
    o_ref[...] = mixed.astype(o_ref.dtype)


def _merge(o_da, o_nsa, w_da, w_nsa, gates):
    tokens, kd = o_da.shape
    d = w_da.shape[1]
    tm, tn = 512, 512
    nb = d // tn
    return pl.pallas_call(
        _merge_kernel,
        grid=(tokens // tm, nb),
        in_specs=[pl.BlockSpec((tm, kd), lambda i, j: (i, 0)),
                  pl.BlockSpec((tm, kd), lambda i, j: (i, 0)),
                  pl.BlockSpec((kd, tn), lambda i, j: (0, j)),
                  pl.BlockSpec((kd, tn), lambda i, j: (0, j)),
                  pl.BlockSpec((tm, tn), lambda i, j: (i, j)),
                  pl.BlockSpec((tm, tn), lambda i, j: (i, nb + j))],
        out_specs=pl.BlockSpec((tm, tn), lambda i, j: (i, j)),
        out_shape=jax.ShapeDtypeStruct((tokens, d), BF16),
        compiler_params=_cparams(2),
        name="gated_merge",
    )(o_da, o_nsa, w_da, w_nsa, gates, gates)


def _layer_norm(z, g, b):
    mu = jnp.mean(z, axis=-1, keepdims=True)
    zc = z - mu
    var = jnp.mean(zc * zc, axis=-1, keepdims=True)
    return (zc * lax.rsqrt(var + LN_EPS)) * g + b


def _outproj_ln_kernel(mixed_ref, w_ref, x_ref, g_ref, b_ref, h_ref, hb_ref):
    z = DN_ALPHA * x_ref[...] + jnp.dot(mixed_ref[...], w_ref[...], preferred_element_type=F32)
    h = _layer_norm(z, g_ref[...], b_ref[...])
    h_ref[...] = h
    hb_ref[...] = h.astype(hb_ref.dtype)


def _outproj_ln(mixed, w_out, x, g, b):
    tokens, d = x.shape
    tm = 256
    row = pl.BlockSpec((tm, d), lambda i: (i, 0))
    vec = pl.BlockSpec((1, d), lambda i: (0, 0))
    return pl.pallas_call(
        _outproj_ln_kernel,
        grid=(tokens // tm,),
        in_specs=[row, pl.BlockSpec((d, d), lambda i: (0, 0)), row, vec, vec],
        out_specs=[row, row],
        out_shape=[jax.ShapeDtypeStruct((tokens, d), F32), jax.ShapeDtypeStruct((tokens, d), BF16)],
        compiler_params=_cparams(1),
        name="outproj_ln1",
    )(mixed, w_out, x, g, b)


def _topk_rows(x, k, ids=None):
    n, t = x.shape
    if ids is None:
        ids = lax.broadcasted_iota(jnp.int32, (n, t), 0).astype(F32)
    slot = lax.broadcasted_iota(jnp.int32, (k, t), 0)
    vals = jnp.zeros((k, t), F32)
    idxs = jnp.zeros((k, t), F32)
    for r in range(k):
        mx = jnp.max(x, axis=0, keepdims=True)
        idx = jnp.min(jnp.where(x == mx, ids, jnp.inf), axis=0, keepdims=True)
        vals = jnp.where(slot == r, mx, vals)
        idxs = jnp.where(slot == r, idx, idxs)
        x = jnp.where(ids == idx, -jnp.inf, x)
    return vals, idxs


def _pair_candidates(v1, i1, v2, i2, nk):
    k, t = v1.shape
    sums, ids, eids = [], [], []
    for c in range(k):
        top = k // (c + 1) - 1
        for fixed_second in (True, False):
            lo = c if fixed_second else c + 1
            if lo > top:
                continue
            n = 8 if top < 8 else k
            run = lax.broadcasted_iota(jnp.int32, (n, t), 0)
            keep = (run >= lo) & (run <= top)
            runf = run.astype(F32)
            if fixed_second:
                val, pid = v1[:n] + v2[c:c + 1], runf * float(k) + float(c)
                eid = i1[:n] * float(nk) + i2[c:c + 1]
            else:
                val, pid = v1[c:c + 1] + v2[:n], float(c * k) + runf
                eid = i1[c:c + 1] * float(nk) + i2[:n]
            sums.append(jnp.where(keep, val, -jnp.inf))
            ids.append(pid)
            eids.append(eid)
    return jnp.concatenate(sums, axis=0), jnp.concatenate(ids, axis=0), jnp.concatenate(eids, axis=0)


def _peer_topk_kernel(q_ref, sk_ref, gate_ref, eid_ref, *, tm):
    k = PEER_TOPK
    nk = PEER_NKEYS
    for h in range(PEER_HEADS):
        qh = q_ref[:, h * LANES:(h + 1) * LANES]
        st = _dot_nt(sk_ref[...], qh)
        v1, i1 = _topk_rows(st[:nk], k)
        v2, i2 = _topk_rows(st[nk:], k)
        cand, ids, cid = _pair_candidates(v1, i1, v2, i2, nk)
        sc, j = _topk_rows(cand, k, ids)
        slot = lax.broadcasted_iota(jnp.int32, (k, tm), 0)
        eid = jnp.zeros((k, tm), F32)
        for r in range(k):
            picked = jnp.max(jnp.where(ids == j[r:r + 1], cid, -1.0), axis=0, keepdims=True)
            eid = jnp.where(slot == r, picked, eid)
        e = jnp.exp(sc - sc[0:1])
        gate_ref[h] = e / jnp.sum(e, axis=0, keepdims=True)
        eid_ref[h] = eid.astype(jnp.int32)


def _peer_topk(q, sk):
    tokens = q.shape[0]
    tm = 256
    kern = functools.partial(_peer_topk_kernel, tm=tm)
    out_blk = pl.BlockSpec((PEER_HEADS, PEER_TOPK, tm), lambda i: (0, 0, i))
    return pl.pallas_call(
        kern,
        grid=(tokens // tm,),
        in_specs=[pl.BlockSpec((tm, PEER_HEADS * LANES), lambda i: (i, 0)),
                  pl.BlockSpec((2 * PEER_NKEYS, LANES), lambda i: (0, 0))],
        out_specs=[out_blk, out_blk],
        out_shape=[jax.ShapeDtypeStruct((PEER_HEADS, PEER_TOPK, tokens), F32),
                   jax.ShapeDtypeStruct((PEER_HEADS, PEER_TOPK, tokens), jnp.int32)],
        compiler_params=_cparams(1),
        name="peer_topk",
    )(q, sk)


def _pack_bf16_pairs(t):
    bits = lax.bitcast_convert_type(t.astype(BF16), jnp.uint16).astype(jnp.uint32)
    half = t.shape[1] // 2
    return bits[:, :half] | (bits[:, half:] << 16)


def _unpack_bf16_pairs(w):
    return pltpu.bitcast(w << 16, F32), pltpu.bitcast(w & jnp.uint32(0xFFFF0000), F32)


def _peer_gather_kernel(eid_hbm, uv_hbm, gate_ref, h_ref, g_ref, b_ref, o_ref, idx_smem, acc, idx_sem, row_sem,
                        *rows, tt, d, tokens):
    i = pl.program_id(0)
    nsteps = pl.num_programs(0)
    nsel = PEER_HEADS * PEER_TOPK
    ns = len(rows)
    per = tt * nsel
    half = d // 2
    nchunk = half // LANES

    def idx_copy(step):
        dst = idx_smem.at[pl.ds(pl.multiple_of((step % 2) * per, per), per)]
        return pltpu.make_async_copy(eid_hbm.at[step], dst, idx_sem.at[step % 2])

    def issue(gtok, slot, first=0, count=None):
        base = (jnp.minimum(gtok, tokens - 1) % (2 * tt)) * nsel
        for j in range(first, first + (nsel if count is None else count)):
            e = idx_smem[base + j]
            pltpu.make_async_copy(uv_hbm.at[e], rows[slot].at[j], row_sem.at[slot]).start(priority=j % 2)

    def wait_slot(slot):
        pltpu.make_async_copy(rows[slot], rows[slot], row_sem.at[slot]).wait()

    @pl.when(i == 0)
    def _():
        first = idx_copy(0)
        first.start()
        first.wait()
        for s in range(ns - 1):
            issue(s, s)

    @pl.when(i + 1 < nsteps)
    def _():
        idx_copy(i + 1).start()

    lane = lax.broadcasted_iota(jnp.int32, (nsel, tt), 1)

    per_chunk = nsel // (2 * nchunk)

    def compute_and_issue(tok, slot, ahead, ahead_slot):
        buf = rows[slot]
        y = h_ref[pl.ds(tok, 1), :]
        part = None
        for c in range(nchunk):
            issue(ahead, ahead_slot, c * per_chunk, per_chunk)
            lo, hi = _unpack_bf16_pairs(buf[:, c * LANES:(c + 1) * LANES])
            term = lo * y[:, c * LANES:(c + 1) * LANES] + hi * y[:, half + c * LANES:half + (c + 1) * LANES]
            part = term if part is None else part + term
        act = jax.nn.gelu(jnp.sum(part, axis=-1, keepdims=True))
        gcol = jnp.sum(jnp.where(lane == tok, gate_ref[...], 0.0), axis=-1, keepdims=True)
        w = gcol * act
        out_lo, out_hi = [], []
        for c in range(nchunk):
            issue(ahead, ahead_slot, (nchunk + c) * per_chunk, per_chunk)
            lo, hi = _unpack_bf16_pairs(buf[:, half + c * LANES:half + (c + 1) * LANES])
            out_lo.append(jnp.sum(w * lo, axis=0, keepdims=True))
            out_hi.append(jnp.sum(w * hi, axis=0, keepdims=True))
        acc[pl.ds(tok, 1), :] = jnp.concatenate(out_lo + out_hi, axis=-1)

    n_groups = tt // ns

    def body(grp, carry):
        @pl.when((grp == n_groups - 1) & (i + 1 < nsteps))
        def _():
            idx_copy(i + 1).wait()

        for s in range(ns):
            tok = grp * ns + s
            wait_slot(s)
            compute_and_issue(tok, s, i * tt + tok + ns - 1, (s + ns - 1) % ns)
        return carry

    lax.fori_loop(0, n_groups, body, 0)

    @pl.when(i == nsteps - 1)
    def _():
        for s in range(ns - 1):
            wait_slot((tt + s) % ns)

    z = DN_ALPHA * h_ref[...] + acc[...]
    o_ref[...] = _layer_norm(z, g_ref[...], b_ref[...])


def _peer_gather(eid, uv, gate, h, g, b):
    tokens, d = h.shape
    tt = PEER_TT
    nsel = PEER_HEADS * PEER_TOPK
    kern = functools.partial(_peer_gather_kernel, tt=tt, d=d, tokens=tokens)
    vec = pl.BlockSpec((1, d), lambda i: (0, 0))
    return pl.pallas_call(
        kern,
        grid=(tokens // tt,),
        in_specs=[pl.BlockSpec(memory_space=pl.ANY),
                  pl.BlockSpec(memory_space=pl.ANY),
                  pl.BlockSpec((nsel, tt), lambda i: (0, i)),
                  pl.BlockSpec((tt, d), lambda i: (i, 0)),
                  vec, vec],
        out_specs=pl.BlockSpec((tt, d), lambda i: (i, 0)),
        out_shape=jax.ShapeDtypeStruct((tokens, d), F32),
        scratch_shapes=[pltpu.SMEM((2 * tt * nsel,), jnp.int32),
                        pltpu.VMEM((tt, d), F32),
                        pltpu.SemaphoreType.DMA((2,)),
                        pltpu.SemaphoreType.DMA((PEER_SLOTS,))]
                       + [pltpu.VMEM((nsel, d), jnp.uint32) for _ in range(PEER_SLOTS)],
        compiler_params=_cparams(1),
        name="peer_gather_ln2",
    )(eid, uv, gate, h, g, b)


def _heads_major(t, heads, width):
    return t.reshape(t.shape[0], heads, width).transpose(1, 0, 2)


def kernel(x, w_in, da_lam_q, da_lam_k, da_subln_g, cmp_pe_k, cmp_w1_k, cmp_w2_k, cmp_pe_v, cmp_w1_v,
           cmp_w2_v, w_branch_da, w_branch_nsa, w_out, ln1_g, ln1_b, peer_wq, peer_subkey1, peer_subkey2,
           peer_u, peer_v, ln2_g, ln2_b, rel_bias):
    batch, seq, d_model = x.shape
    tokens = batch * seq
    g, hd = NSA_GROUPS, NSA_DIM
    table = rel_bias.astype(F32)
    xs = x.reshape(tokens, d_model)
    for l in range(DEPTH):
        lam_init = 0.8 - 0.6 * math.exp(-0.3 * l)
        xb = xs.astype(BF16)
        w = w_in[l]
        scale = DA_HEAD_DIM ** -0.5
        c_daq, c_dak, c_dav, c_nq = 0, 1024, 2048, 3072
        c_kv, c_gate, c_mg, c_end = 4096, 5632, 5680, 9776
        w_att = jnp.concatenate([w[:, c_daq:c_dak] * scale, w[:, c_dak:c_nq], w[:, c_nq:c_kv] * scale],
                                axis=1).astype(BF16)
        att = _matmul(xb, w_att, BF16, 512, 512)
        kv = _matmul(xb, w[:, c_kv:c_gate].astype(BF16), F32, 512, 512)
        w_gate = jnp.pad(w[:, c_gate:c_mg], ((0, 0), (0, LANES - (c_mg - c_gate)))).astype(BF16)
        br_gate = _matmul(xb, w_gate, F32, 512, LANES)[:, :c_mg - c_gate]
        mg_gate = _matmul(xb, w[:, c_mg:c_end].astype(BF16), F32, 512, 512)

        lam_e = jnp.exp(jnp.sum(da_lam_q[l].astype(F32) * da_lam_k[l].astype(F32), -1))
        lam = (lam_e[0] - lam_e[1] + lam_init).reshape(1)
        table_rel = table - table[REL_BUCKETS - 1]
        o_da = _diff_attention(att, lam, _causal_bias_tiles(table_rel[:, :DA_HEADS], DA_TILE),
                               da_subln_g[l].reshape(1, DA_V_DIM), batch, seq, lam_init)

        q_n = _heads_major(att[:, 3072:4096], NSA_HEADS, hd)
        kv6 = kv.reshape(tokens, 6, g * hd)
        kc = _compress(kv6[:, 0], cmp_pe_k[l], cmp_w1_k[l], cmp_w2_k[l], batch, seq)
        vc = _compress(kv6[:, 1], cmp_pe_v[l], cmp_w1_v[l], cmp_w2_v[l], batch, seq)
        grp = lambda t: t.astype(BF16).reshape(batch, seq, g, hd).transpose(2, 0, 1, 3)
        k_s, v_s, k_w, v_w = grp(kv6[:, 2]), grp(kv6[:, 3]), grp(kv6[:, 4]), grp(kv6[:, 5])

        ncp = seq // CMP_STRIDE
        ns = seq // SLC_BLOCK
        nsp = -(-ns // LANES) * LANES
        k_sel = min(SLC_TOPK, ns)
        cmp_start = jnp.arange(ncp) * CMP_STRIDE
        slc_start = jnp.arange(nsp) * SLC_BLOCK
        overlap = ((cmp_start[:, None] <= slc_start[None, :] + SLC_BLOCK - 1)
                   & (cmp_start[:, None] + CMP_BLOCK - 1 >= slc_start[None, :])).astype(BF16)
        o_cmp, sel = _cmp_select(q_n, kc, vc, overlap, br_gate, batch, seq, k_sel)

        win_bias = _window_bias_tile(table[:, DA_HEADS:], WIN_TQ, WINDOW + WIN_TQ)
        pad_w = lambda t: jnp.pad(t, ((0, 0), (0, 0), (WINDOW, 0), (0, 0)))
        o_win = _window_attention(q_n, pad_w(k_w), pad_w(v_w), win_bias, br_gate, batch, seq)

        v_s1 = jnp.concatenate([v_s, jnp.ones_like(v_s)], axis=-1)
        o_nsa = _slc_attention(q_n, k_s, v_s1, sel, _causal_bias_tiles(table_rel[:, DA_HEADS:], SLC_TILE),
                               br_gate, o_cmp, o_win, batch, seq)
        o_nsa = o_nsa.transpose(1, 0, 2).reshape(tokens, NSA_HEADS * hd)

        mixed = _merge(o_da, o_nsa, w_branch_da[l].astype(BF16), w_branch_nsa[l].astype(BF16), mg_gate)
        h, hb = _outproj_ln(mixed, w_out[l].astype(BF16), xs, ln1_g[l].reshape(1, -1), ln1_b[l].reshape(1, -1))

        pq = _matmul(hb, peer_wq[l].astype(BF16), BF16, 512, 512)
        half = peer_subkey1.shape[-1]
        zeros = jnp.zeros((PEER_NKEYS, half), F32)
        sk = jnp.concatenate([jnp.concatenate([peer_subkey1[l], zeros], axis=1),
                              jnp.concatenate([zeros, peer_subkey2[l]], axis=1)], axis=0).astype(BF16)
        gate, eid = _peer_topk(pq, sk)
        nsel = PEER_HEADS * PEER_TOPK
        eid_tok = eid.reshape(nsel, tokens).T.reshape(tokens // PEER_TT, PEER_TT * nsel)
        uv = jnp.concatenate([_pack_bf16_pairs(peer_u[l]), _pack_bf16_pairs(peer_v[l])], axis=1)
        xs = _peer_gather(eid_tok, uv, gate.reshape(nsel, tokens), h,
                          ln2_g[l].reshape(1, -1), ln2_b[l].reshape(1, -1))
    return xs.reshape(batch, seq, d_model)
```

```python
import functools
import math

import jax
import jax.numpy as jnp
from jax import lax
from jax.experimental import pallas as pl
from jax.experimental.pallas import tpu as pltpu

F32 = jnp.float32
BF16 = jnp.bfloat16

DA_HEADS = 8
DA_HEAD_DIM = 64
DA_V_DIM = 128
NSA_HEADS = 16
NSA_GROUPS = 4
NSA_HPG = 4
NSA_DIM = 64
CMP_BLOCK = 32
CMP_STRIDE = 16
SLC_BLOCK = 64
SLC_TOPK = 16
N_LOCAL_BLOCKS = 2
WINDOW = 512
REL_BUCKETS = 32
REL_MAX_DIST = 128
PEER_HEADS = 8
PEER_NKEYS = 128
PEER_TOPK = 16
DEPTH = 1
DN_ALPHA = (2 * DEPTH) ** 0.25
LN_EPS = 1e-5
NEG = -1e30
FORCE_SCORE = 1e4

LANES = 128
VMEM_LIMIT = 48 * 1024 * 1024
DA_TILE = 512
SLC_TILE = 512
WIN_TQ = 256
CMP_TQ = 512
PEER_TT = 128
PEER_SLOTS = 8


def _cparams(n_axes):
    return pltpu.CompilerParams(dimension_semantics=("arbitrary",) * n_axes,
                                vmem_limit_bytes=VMEM_LIMIT)


def _dot_nt(a, b):
    return lax.dot_general(a, b, (((1,), (1,)), ((), ())), preferred_element_type=F32)


def _dot_exact01(x, onehot_bf16):
    hi = x.astype(BF16)
    r1 = x - hi.astype(F32)
    mid = r1.astype(BF16)
    lo = (r1 - mid.astype(F32)).astype(BF16)
    d = lambda a: jnp.dot(a, onehot_bf16, preferred_element_type=F32)
    return d(hi) + d(mid) + d(lo)


def _mm_kernel(a_ref, b_ref, o_ref):
    o_ref[...] = jnp.dot(a_ref[...], b_ref[...], preferred_element_type=F32).astype(o_ref.dtype)


def _matmul(a, b, out_dtype, tm, tn):
    m, k = a.shape
    n = b.shape[1]
    return pl.pallas_call(
        _mm_kernel,
        grid=(n // tn, m // tm),
        in_specs=[pl.BlockSpec((tm, k), lambda j, i: (i, 0)),
                  pl.BlockSpec((k, tn), lambda j, i: (0, j))],
        out_specs=pl.BlockSpec((tm, tn), lambda j, i: (i, j)),
        out_shape=jax.ShapeDtypeStruct((m, n), out_dtype),
        compiler_params=_cparams(2),
        name="matmul",
    )(a, b)


def _rel_bucket(dist):
    n = jnp.maximum(dist, 0)
    max_exact = REL_BUCKETS // 2
    nf = jnp.maximum(n, 1).astype(F32)
    large = max_exact + (jnp.log(nf / max_exact) / math.log(REL_MAX_DIST / max_exact)
                         * (REL_BUCKETS - max_exact)).astype(jnp.int32)
    large = jnp.minimum(large, REL_BUCKETS - 1)
    return jnp.where(n < max_exact, n, large)


def _toeplitz(rd, n_i, n_j):
    h, length = rd.shape
    rev = jnp.pad(rd[:, ::-1], ((0, 0), (0, 1)))
    flat = jnp.broadcast_to(rev[:, None, :], (h, n_i, length + 1)).reshape(h, n_i * (length + 1))
    skew = flat[:, :n_i * length].reshape(h, n_i, length)
    return skew[:, :, n_i - 1:n_i - 1 + n_j]


def _bias_by_distance(table, n):
    return table[_rel_bucket(jnp.arange(n))].T


def _causal_bias_tiles(table, t):
    bd = _bias_by_distance(table, 2 * t)
    neg = jnp.full((bd.shape[0], t - 1), NEG, F32)
    diag = _toeplitz(jnp.concatenate([neg, bd[:, :t]], axis=1), t, t)
    off = _toeplitz(bd[:, 1:], t, t)
    return jnp.stack([jnp.zeros_like(off), off, diag], axis=0)


def _window_bias_tile(table, tq, span):
    bd = _bias_by_distance(table, WINDOW)
    h = bd.shape[0]
    lo = jnp.full((h, span - 1 - WINDOW), NEG, F32)
    hi = jnp.full((h, tq), NEG, F32)
    return _toeplitz(jnp.concatenate([lo, bd, hi], axis=1), tq, span)


def _flash_init(m_scr, acc_scr):
    m_scr[...] = jnp.full(m_scr.shape, NEG, F32)
    acc_scr[...] = jnp.zeros(acc_scr.shape, F32)


def _flash_update(s, v1, m_scr, acc_scr):
    m_old = m_scr[...]
    m_new = jnp.maximum(m_old, jnp.broadcast_to(jnp.max(s, axis=-1, keepdims=True), m_old.shape))
    alpha = jnp.exp(m_old - m_new)
    widen = lambda x, width: jnp.concatenate([x] * (width // LANES), axis=1)
    p = jnp.exp(s - widen(m_new, s.shape[1])).astype(BF16)
    acc = acc_scr[...]
    acc_scr[...] = widen(alpha, acc.shape[1]) * acc + jnp.dot(p, v1, preferred_element_type=F32)
    m_scr[...] = m_new


def _pipelined_sweep(n_tiles, logits_fn, consume_fn, s_scr):
    last = n_tiles - 1
    s_scr[0] = logits_fn(0)

    def body(pair, carry):
        a = 2 * pair
        s_scr[1] = logits_fn(a + 1)
        consume_fn(a, 0)
        s_scr[0] = logits_fn(jnp.minimum(a + 2, last))
        consume_fn(a + 1, 1)
        return carry

    lax.fori_loop(0, n_tiles // 2, body, 0)

    @pl.when(n_tiles % 2 == 1)
    def _():
        consume_fn(last, 0)


def _near_kind(j, qi):
    return jnp.maximum(j - (qi - 2), 0)


def _head_gates(gate_ref, group, branch):
    sig = jax.nn.sigmoid(gate_ref[...])
    lane = lax.broadcasted_iota(jnp.int32, (1, sig.shape[-1]), 1)
    cols = [jnp.sum(jnp.where(lane == (group * NSA_HPG + h) * 3 + branch, sig, 0.0), axis=-1, keepdims=True)
            for h in range(NSA_HPG)]
    return jnp.stack(cols)


def _da_kernel(lam_ref, q_ref, k_ref, v_ref, bias_ref, g_ref, o_ref, m_scr, acc_scr, s_scr, *, t, lam_init):
    qi = pl.program_id(2)
    q = q_ref[...]
    lane = lax.broadcasted_iota(jnp.int32, (1, LANES), 1)
    zero = jnp.zeros_like(q)
    q2 = jnp.concatenate([jnp.where(lane < DA_HEAD_DIM, q, zero), jnp.where(lane >= DA_HEAD_DIM, q, zero)], axis=0)
    _flash_init(m_scr, acc_scr)

    def logits(j):
        return _dot_nt(q2, k_ref[pl.ds(pl.multiple_of(j * t, t), t), :])

    def consume(j, slot):
        s = (s_scr[slot].reshape(2, t, t) + bias_ref[_near_kind(j, qi)][None]).reshape(2 * t, t)
        v = v_ref[pl.ds(pl.multiple_of(j * t, t), t), :]
        v1 = jnp.concatenate([v, jnp.ones((t, LANES), BF16)], axis=-1)
        _flash_update(s, v1, m_scr, acc_scr)

    _pipelined_sweep(qi + 1, logits, consume, s_scr)

    acc = acc_scr[...]
    o = acc[:t, :DA_V_DIM] / acc[:t, DA_V_DIM:] - lam_ref[0] * (acc[t:, :DA_V_DIM] / acc[t:, DA_V_DIM:])
    ms = jnp.mean(o * o, axis=-1, keepdims=True)
    o = (o * lax.rsqrt(ms + LN_EPS)) * g_ref[...] * (1.0 - lam_init)
    o_ref[...] = o.astype(o_ref.dtype)


def _diff_attention(att, lam, bias, subln_g, batch, seq, lam_init):
    t = DA_TILE
    nq = seq // t
    kern = functools.partial(_da_kernel, t=t, lam_init=lam_init)
    return pl.pallas_call(
        kern,
        grid=(batch, DA_HEADS, nq),
        in_specs=[pl.BlockSpec(memory_space=pltpu.SMEM),
                  pl.BlockSpec((t, LANES), lambda b, h, i: (b * nq + i, h)),
                  pl.BlockSpec((seq, LANES), lambda b, h, i: (b, DA_HEADS + h)),
                  pl.BlockSpec((seq, LANES), lambda b, h, i: (b, 2 * DA_HEADS + h)),
                  pl.BlockSpec((3, None, t, t), lambda b, h, i: (0, h, 0, 0)),
                  pl.BlockSpec((1, DA_V_DIM), lambda b, h, i: (0, 0))],
        out_specs=pl.BlockSpec((t, DA_V_DIM), lambda b, h, i: (b * nq + i, h)),
        out_shape=jax.ShapeDtypeStruct((batch * seq, DA_HEADS * DA_V_DIM), BF16),
        scratch_shapes=[pltpu.VMEM((2 * t, LANES), F32), pltpu.VMEM((2 * t, 2 * DA_V_DIM), F32),
                        pltpu.VMEM((2, 2 * t, t), F32)],
        compiler_params=_cparams(3),
        name="diff_attention",
    )(lam, att, att, att, bias, subln_g)


def _compress_kernel(x_ref, pe_ref, w1_ref, w2_ref, o_ref):
    x = (x_ref[...].astype(F32) + pe_ref[...]).astype(BF16)
    hid = jax.nn.gelu(jnp.dot(x, w1_ref[...], preferred_element_type=F32))
    o_ref[...] = jnp.dot(hid.astype(BF16), w2_ref[...], preferred_element_type=F32).astype(o_ref.dtype)


def _compress(t, pe, w1, w2, batch, seq):
    g, d = NSA_GROUPS, NSA_DIM
    r = CMP_BLOCK // CMP_STRIDE
    nch = seq // CMP_STRIDE
    nc = nch - r + 1
    ch = t.reshape(batch, nch, CMP_STRIDE, g, d)
    blocks = jnp.concatenate([ch[:, j:j + nc] for j in range(r)], axis=2)
    flat = blocks.transpose(0, 1, 3, 2, 4).reshape(batch * nc * g, CMP_BLOCK * d)
    rows = flat.shape[0]
    tm = 512
    rows_p = -(-rows // tm) * tm
    flat = jnp.pad(flat, ((0, rows_p - rows), (0, 0)))
    hidden = w1.shape[1]
    out = pl.pallas_call(
        _compress_kernel,
        grid=(rows_p // tm,),
        in_specs=[pl.BlockSpec((tm, CMP_BLOCK * d), lambda i: (i, 0)),
                  pl.BlockSpec((1, CMP_BLOCK * d), lambda i: (0, 0)),
                  pl.BlockSpec((CMP_BLOCK * d, hidden), lambda i: (0, 0)),
                  pl.BlockSpec((hidden, d), lambda i: (0, 0))],
        out_specs=pl.BlockSpec((tm, d), lambda i: (i, 0)),
        out_shape=jax.ShapeDtypeStruct((rows_p, d), BF16),
        compiler_params=_cparams(1),
        name="compress_mlp",
    )(flat, pe.reshape(1, CMP_BLOCK * d), w1.astype(BF16), w2.astype(BF16))
    out = out[:rows].reshape(batch, nc, g, d).transpose(2, 0, 1, 3)
    return jnp.pad(out, ((0, 0), (0, 0), (0, nch - nc), (0, 0)))


def _cmp_select_kernel(q_ref, kc_ref, vc_ref, ov_ref, gate_ref, o_ref, sel_ref, *, tq, k_sel):
    qi = pl.program_id(2)
    ncp = kc_ref.shape[0]
    nsp = sel_ref.shape[-1]
    q = q_ref[...].reshape(NSA_HPG * tq, NSA_DIM)
    q_pos = qi * tq + lax.broadcasted_iota(jnp.int32, (tq, 1), 0)
    c_end = lax.broadcasted_iota(jnp.int32, (1, ncp), 1) * CMP_STRIDE + (CMP_BLOCK - 1)
    cmask = (c_end <= q_pos)[None]
    s = _dot_nt(q, kc_ref[...]).reshape(NSA_HPG, tq, ncp)
    s = jnp.where(cmask, s, NEG)
    m = jnp.max(s, axis=-1, keepdims=True)
    p = jnp.where(cmask, jnp.exp(s - m), 0.0)
    p = p / jnp.maximum(jnp.sum(p, axis=-1, keepdims=True), 1e-30)
    o = jnp.dot(p.reshape(NSA_HPG * tq, ncp).astype(BF16), vc_ref[...], preferred_element_type=F32)
    o_ref[...] = o.reshape(NSA_HPG, tq, NSA_DIM) * _head_gates(gate_ref, pl.program_id(0), 0)

    psum = p[0] + p[1] + p[2] + p[3]
    imp = _dot_exact01(psum, ov_ref[...])
    blk = lax.broadcasted_iota(jnp.int32, (1, nsp), 1)
    cur = q_pos // SLC_BLOCK
    valid = blk <= cur
    forced = valid & ((blk == 0) | (blk > cur - N_LOCAL_BLOCKS))
    score = jnp.where(forced, FORCE_SCORE, jnp.where(valid, imp, NEG))
    blk_f = blk.astype(F32)
    sel = jnp.zeros((tq, nsp), F32)
    for _ in range(k_sel):
        mx = jnp.max(score, axis=-1, keepdims=True)
        idx = jnp.min(jnp.where(score == mx, blk_f, float(nsp)), axis=-1, keepdims=True)
        hit = blk_f == idx
        sel = jnp.where(hit, 1.0, sel)
        score = jnp.where(hit, -jnp.inf, score)
    sel_ref[...] = sel.astype(sel_ref.dtype)


def _cmp_select(q, kc, vc, overlap, gate, batch, seq, k_sel):
    tq = CMP_TQ
    nq = seq // tq
    ncp = kc.shape[2]
    nsp = overlap.shape[1]
    kern = functools.partial(_cmp_select_kernel, tq=tq, k_sel=k_sel)
    return pl.pallas_call(
        kern,
        grid=(NSA_GROUPS, batch, nq),
        in_specs=[pl.BlockSpec((NSA_HPG, tq, NSA_DIM), lambda g, b, i: (g, b * nq + i, 0)),
                  pl.BlockSpec((None, None, ncp, NSA_DIM), lambda g, b, i: (g, b, 0, 0)),
                  pl.BlockSpec((None, None, ncp, NSA_DIM), lambda g, b, i: (g, b, 0, 0)),
                  pl.BlockSpec((ncp, nsp), lambda g, b, i: (0, 0)),
                  pl.BlockSpec((tq, gate.shape[1]), lambda g, b, i: (b * nq + i, 0))],
        out_specs=[pl.BlockSpec((NSA_HPG, tq, NSA_DIM), lambda g, b, i: (g, b * nq + i, 0)),
                   pl.BlockSpec((None, tq, nsp), lambda g, b, i: (g, b * nq + i, 0))],
        out_shape=[jax.ShapeDtypeStruct((NSA_HEADS, batch * seq, NSA_DIM), F32),
                   jax.ShapeDtypeStruct((NSA_GROUPS, batch * seq, nsp), BF16)],
        compiler_params=_cparams(3),
        name="nsa_cmp_select",
    )(q, kc, vc, overlap, gate)


def _window_kernel(q_ref, k_ref, v_ref, bias_ref, gate_ref, o_ref, *, tq, span):
    qi = pl.program_id(2)
    q0 = pl.multiple_of(qi * tq, tq)
    q = q_ref[...].reshape(NSA_HPG * tq, NSA_DIM)
    ks = k_ref[pl.ds(q0, span), :]
    vs = v_ref[pl.ds(q0, span), :]
    s = _dot_nt(q, ks).reshape(NSA_HPG, tq, span) + bias_ref[...]
    in_seq = (lax.broadcasted_iota(jnp.int32, (1, 1, span), 2) + q0) >= WINDOW
    s = jnp.where(in_seq, s, NEG)
    m = jnp.max(s, axis=-1, keepdims=True)
    p = jnp.exp(s - m)
    l = jnp.sum(p, axis=-1, keepdims=True)
    o = jnp.dot(p.reshape(NSA_HPG * tq, span).astype(BF16), vs, preferred_element_type=F32)
    o = o.reshape(NSA_HPG, tq, NSA_DIM) / l
    o_ref[...] = o * _head_gates(gate_ref, pl.program_id(0), 2)


def _window_attention(q, kp, vp, bias, gate, batch, seq):
    tq = WIN_TQ
    span = WINDOW + tq
    nq = seq // tq
    kern = functools.partial(_window_kernel, tq=tq, span=span)
    return pl.pallas_call(
        kern,
        grid=(NSA_GROUPS, batch, nq),
        in_specs=[pl.BlockSpec((NSA_HPG, tq, NSA_DIM), lambda g, b, i: (g, b * nq + i, 0)),
                  pl.BlockSpec((None, None, WINDOW + seq, NSA_DIM), lambda g, b, i: (g, b, 0, 0)),
                  pl.BlockSpec((None, None, WINDOW + seq, NSA_DIM), lambda g, b, i: (g, b, 0, 0)),
                  pl.BlockSpec((NSA_HPG, tq, span), lambda g, b, i: (g, 0, 0)),
                  pl.BlockSpec((tq, gate.shape[1]), lambda g, b, i: (b * nq + i, 0))],
        out_specs=pl.BlockSpec((NSA_HPG, tq, NSA_DIM), lambda g, b, i: (g, b * nq + i, 0)),
        out_shape=jax.ShapeDtypeStruct((NSA_HEADS, batch * seq, NSA_DIM), F32),
        compiler_params=_cparams(3),
        name="nsa_window",
    )(q, kp, vp, bias, gate)


def _slc_kernel(q_ref, k_ref, v_ref, sel_ref, bias_ref, gate_ref, oc_ref, ow_ref, o_ref,
                m_scr, acc_scr, s_scr, *, t):
    g = pl.program_id(0)
    qi = pl.program_id(2)
    nsp = sel_ref.shape[-1]
    rows = NSA_HPG * t
    q = q_ref[...].reshape(rows, NSA_DIM)
    sel = sel_ref[...]
    _flash_init(m_scr, acc_scr)
    blk_row = lax.broadcasted_iota(jnp.int32, (nsp, t), 0)
    key_blk = lax.broadcasted_iota(jnp.int32, (nsp, t), 1) // SLC_BLOCK

    def logits(j):
        return _dot_nt(q, k_ref[pl.ds(pl.multiple_of(j * t, t), t), :])

    def consume(j, slot):
        expand = jnp.where(blk_row == key_blk + j * (t // SLC_BLOCK), 1.0, 0.0).astype(BF16)
        off = (jnp.dot(sel, expand, preferred_element_type=F32) - 1.0) * (-NEG)
        s = s_scr[slot].reshape(NSA_HPG, t, t) + (off[None] + bias_ref[_near_kind(j, qi)])
        v1 = v_ref[pl.ds(pl.multiple_of(j * t, t), t), :]
        _flash_update(s.reshape(rows, t), v1, m_scr, acc_scr)

    _pipelined_sweep(qi + 1, logits, consume, s_scr)

    acc = acc_scr[...]
    o = (acc[:, :NSA_DIM] / acc[:, NSA_DIM:]).reshape(NSA_HPG, t, NSA_DIM)
    o = oc_ref[...] + ow_ref[...] + o * _head_gates(gate_ref, g, 1)
    o_ref[...] = o.astype(o_ref.dtype)


def _slc_attention(q, ks, vs1, sel, bias, gate, o_cmp, o_win, batch, seq):
    t = SLC_TILE
    nq = seq // t
    nsp = sel.shape[-1]
    kern = functools.partial(_slc_kernel, t=t)
    head_blk = lambda last: pl.BlockSpec((NSA_HPG, t, last), lambda g, b, i: (g, b * nq + i, 0))
    return pl.pallas_call(
        kern,
        grid=(NSA_GROUPS, batch, nq),
        in_specs=[head_blk(NSA_DIM),
                  pl.BlockSpec((None, None, seq, NSA_DIM), lambda g, b, i: (g, b, 0, 0)),
                  pl.BlockSpec((None, None, seq, 2 * NSA_DIM), lambda g, b, i: (g, b, 0, 0)),
                  pl.BlockSpec((None, t, nsp), lambda g, b, i: (g, b * nq + i, 0)),
                  pl.BlockSpec((3, NSA_HPG, t, t), lambda g, b, i: (0, g, 0, 0), pipeline_mode=pl.Buffered(1)),
                  pl.BlockSpec((t, gate.shape[1]), lambda g, b, i: (b * nq + i, 0)),
                  head_blk(NSA_DIM), head_blk(NSA_DIM)],
        out_specs=head_blk(NSA_DIM),
        out_shape=jax.ShapeDtypeStruct((NSA_HEADS, batch * seq, NSA_DIM), BF16),
        scratch_shapes=[pltpu.VMEM((NSA_HPG * t, LANES), F32), pltpu.VMEM((NSA_HPG * t, 2 * NSA_DIM), F32),
                        pltpu.VMEM((2, NSA_HPG * t, t), F32)],
        compiler_params=_cparams(3),
        name="nsa_selected",
    )(q, ks, vs1, sel, bias, gate, o_cmp, o_win)


def _merge_kernel(oda_ref, onsa_ref, wda_ref, wnsa_ref, gda_ref, gnsa_ref, o_ref):
    a = jnp.dot(oda_ref[...], wda_ref[...], preferred_element_type=F32)
    n = jnp.dot(onsa_ref[...], wnsa_ref[...], preferred_element_type=F32)
    mixed = jax.nn.sigmoid(gda_ref[...]) * a + jax.nn.sigmoid(gnsa_ref[...]) * n
    o_ref[...] = mixed.astype(o_ref.dtype)


def _merge(o_da, o_nsa, w_da, w_nsa, gates):
    tokens, kd = o_da.shape
    d = w_da.shape[1]
    tm, tn = 512, 512
    nb = d // tn
    return pl.pallas_call(
        _merge_kernel,
        grid=(tokens // tm, nb),
        in_specs=[pl.BlockSpec((tm, kd), lambda i, j: (i, 0)),
                  pl.BlockSpec((tm, kd), lambda i, j: (i, 0)),
                  pl.BlockSpec((kd, tn), lambda i, j: (0, j)),
                  pl.BlockSpec((kd, tn), lambda i, j: (0, j)),
                  pl.BlockSpec((tm, tn), lambda i, j: (i, j)),
                  pl.BlockSpec((tm, tn), lambda i, j: (i, nb + j))],
        out_specs=pl.BlockSpec((tm, tn), lambda i, j: (i, j)),
        out_shape=jax.ShapeDtypeStruct((tokens, d), BF16),
        compiler_params=_cparams(2),
        name="gated_merge",
    )(o_da, o_nsa, w_da, w_nsa, gates, gates)


def _layer_norm(z, g, b):
    mu = jnp.mean(z, axis=-1, keepdims=True)
    zc = z - mu
    var = jnp.mean(zc * zc, axis=-1, keepdims=True)
    return (zc * lax.rsqrt(var + LN_EPS)) * g + b


def _outproj_ln_kernel(mixed_ref, w_ref, x_ref, g_ref, b_ref, h_ref, hb_ref):
    z = DN_ALPHA * x_ref[...] + jnp.dot(mixed_ref[...], w_ref[...], preferred_element_type=F32)
    h = _layer_norm(z, g_ref[...], b_ref[...])
    h_ref[...] = h
    hb_ref[...] = h.astype(hb_ref.dtype)


def _outproj_ln(mixed, w_out, x, g, b):
    tokens, d = x.shape
    tm = 256
    row = pl.BlockSpec((tm, d), lambda i: (i, 0))
    vec = pl.BlockSpec((1, d), lambda i: (0, 0))
    return pl.pallas_call(
        _outproj_ln_kernel,
        grid=(tokens // tm,),
        in_specs=[row, pl.BlockSpec((d, d), lambda i: (0, 0)), row, vec, vec],
        out_specs=[row, row],
        out_shape=[jax.ShapeDtypeStruct((tokens, d), F32), jax.ShapeDtypeStruct((tokens, d), BF16)],
        compiler_params=_cparams(1),
        name="outproj_ln1",
    )(mixed, w_out, x, g, b)


def _topk_rows(x, k, ids=None):
    n, t = x.shape
    if ids is None:
        ids = lax.broadcasted_iota(jnp.int32, (n, t), 0).astype(F32)
    slot = lax.broadcasted_iota(jnp.int32, (k, t), 0)
    vals = jnp.zeros((k, t), F32)
    idxs = jnp.zeros((k, t), F32)
    for r in range(k):
        mx = jnp.max(x, axis=0, keepdims=True)
        idx = jnp.min(jnp.where(x == mx, ids, jnp.inf), axis=0, keepdims=True)
        vals = jnp.where(slot == r, mx, vals)
        idxs = jnp.where(slot == r, idx, idxs)
        x = jnp.where(ids == idx, -jnp.inf, x)
    return vals, idxs


def _pair_candidates(v1, i1, v2, i2, nk):
    k, t = v1.shape
    sums, ids, eids = [], [], []
    for c in range(k):
        top = k // (c + 1) - 1
        for fixed_second in (True, False):
            lo = c if fixed_second else c + 1
            if lo > top:
                continue
            n = 8 if top < 8 else k
            run = lax.broadcasted_iota(jnp.int32, (n, t), 0)
            keep = (run >= lo) & (run <= top)
            runf = run.astype(F32)
            if fixed_second:
                val, pid = v1[:n] + v2[c:c + 1], runf * float(k) + float(c)
                eid = i1[:n] * float(nk) + i2[c:c + 1]
            else:
                val, pid = v1[c:c + 1] + v2[:n], float(c * k) + runf
                eid = i1[c:c + 1] * float(nk) + i2[:n]
            sums.append(jnp.where(keep, val, -jnp.inf))
            ids.append(pid)
            eids.append(eid)
    return jnp.concatenate(sums, axis=0), jnp.concatenate(ids, axis=0), jnp.concatenate(eids, axis=0)


def _peer_topk_kernel(q_ref, sk_ref, gate_ref, eid_ref, *, tm):
    k = PEER_TOPK
    nk = PEER_NKEYS
    for h in range(PEER_HEADS):
        qh = q_ref[:, h * LANES:(h + 1) * LANES]
        st = _dot_nt(sk_ref[...], qh)
        v1, i1 = _topk_rows(st[:nk], k)
        v2, i2 = _topk_rows(st[nk:], k)
        cand, ids, cid = _pair_candidates(v1, i1, v2, i2, nk)
        sc, j = _topk_rows(cand, k, ids)
        slot = lax.broadcasted_iota(jnp.int32, (k, tm), 0)
        eid = jnp.zeros((k, tm), F32)
        for r in range(k):
            picked = jnp.max(jnp.where(ids == j[r:r + 1], cid, -1.0), axis=0, keepdims=True)
            eid = jnp.where(slot == r, picked, eid)
        e = jnp.exp(sc - sc[0:1])
        gate_ref[h] = e / jnp.sum(e, axis=0, keepdims=True)
        eid_ref[h] = eid.astype(jnp.int32)


def _peer_topk(q, sk):
    tokens = q.shape[0]
    tm = 256
    kern = functools.partial(_peer_topk_kernel, tm=tm)
    out_blk = pl.BlockSpec((PEER_HEADS, PEER_TOPK, tm), lambda i: (0, 0, i))
    return pl.pallas_call(
        kern,
        grid=(tokens // tm,),
        in_specs=[pl.BlockSpec((tm, PEER_HEADS * LANES), lambda i: (i, 0)),
                  pl.BlockSpec((2 * PEER_NKEYS, LANES), lambda i: (0, 0))],
        out_specs=[out_blk, out_blk],
        out_shape=[jax.ShapeDtypeStruct((PEER_HEADS, PEER_TOPK, tokens), F32),
                   jax.ShapeDtypeStruct((PEER_HEADS, PEER_TOPK, tokens), jnp.int32)],
        compiler_params=_cparams(1),
        name="peer_topk",
    )(q, sk)


def _pack_bf16_pairs(t):
    bits = lax.bitcast_convert_type(t.astype(BF16), jnp.uint16).astype(jnp.uint32)
    half = t.shape[1] // 2
    return bits[:, :half] | (bits[:, half:] << 16)


def _unpack_bf16_pairs(w):
    return pltpu.bitcast(w << 16, F32), pltpu.bitcast(w & jnp.uint32(0xFFFF0000), F32)


def _peer_gather_kernel(eid_hbm, uv_hbm, gate_ref, h_ref, g_ref, b_ref, o_ref, idx_smem, acc, idx_sem, row_sem,
                        *rows, tt, d, tokens):
    i = pl.program_id(0)
    nsteps = pl.num_programs(0)
    nsel = PEER_HEADS * PEER_TOPK
    ns = len(rows)
    per = tt * nsel
    half = d // 2
    nchunk = half // LANES

    def idx_copy(step):
        dst = idx_smem.at[pl.ds(pl.multiple_of((step % 2) * per, per), per)]
        return pltpu.make_async_copy(eid_hbm.at[step], dst, idx_sem.at[step % 2])

    def issue(gtok, slot, first=0, count=None):
        base = (jnp.minimum(gtok, tokens - 1) % (2 * tt)) * nsel
        for j in range(first, first + (nsel if count is None else count)):
            e = idx_smem[base + j]
            pltpu.make_async_copy(uv_hbm.at[e], rows[slot].at[:, j, :], row_sem.at[slot]).start(priority=j % 2)

    def wait_slot(slot):
        pltpu.make_async_copy(rows[slot], rows[slot], row_sem.at[slot]).wait()

    @pl.when(i == 0)
    def _():
        first = idx_copy(0)
        first.start()
        first.wait()
        for s in range(ns - 1):
            issue(s, s)

    @pl.when(i + 1 < nsteps)
    def _():
        idx_copy(i + 1).start()

    lane = lax.broadcasted_iota(jnp.int32, (nsel, tt), 1)

    per_chunk = nsel // (2 * nchunk)

    def compute_and_issue(tok, slot, ahead, ahead_slot):
        buf = rows[slot]
        y = h_ref[pl.ds(tok, 1), :]
        part = None
        for c in range(nchunk):
            issue(ahead, ahead_slot, c * per_chunk, per_chunk)
            lo, hi = _unpack_bf16_pairs(buf[c])
            term = lo * y[:, c * LANES:(c + 1) * LANES] + hi * y[:, half + c * LANES:half + (c + 1) * LANES]
            part = term if part is None else part + term
        act = jax.nn.gelu(jnp.sum(part, axis=-1, keepdims=True))
        gcol = jnp.sum(jnp.where(lane == tok, gate_ref[...], 0.0), axis=-1, keepdims=True)
        w = gcol * act
        out_lo, out_hi = [], []
        for c in range(nchunk):
            issue(ahead, ahead_slot, (nchunk + c) * per_chunk, per_chunk)
            lo, hi = _unpack_bf16_pairs(buf[nchunk + c])
            out_lo.append(jnp.sum(w * lo, axis=0, keepdims=True))
            out_hi.append(jnp.sum(w * hi, axis=0, keepdims=True))
        acc[pl.ds(tok, 1), :] = jnp.concatenate(out_lo + out_hi, axis=-1)

    n_groups = tt // ns

    def body(grp, carry):
        @pl.when((grp == n_groups - 1) & (i + 1 < nsteps))
        def _():
            idx_copy(i + 1).wait()

        for s in range(ns):
            tok = grp * ns + s
            wait_slot(s)
            compute_and_issue(tok, s, i * tt + tok + ns - 1, (s + ns - 1) % ns)
        return carry

    lax.fori_loop(0, n_groups, body, 0)

    @pl.when(i == nsteps - 1)
    def _():
        for s in range(ns - 1):
            wait_slot((tt + s) % ns)

    z = DN_ALPHA * h_ref[...] + acc[...]
    o_ref[...] = _layer_norm(z, g_ref[...], b_ref[...])


def _peer_gather(eid, uv, gate, h, g, b):
    tokens, d = h.shape
    tt = PEER_TT
    nsel = PEER_HEADS * PEER_TOPK
    kern = functools.partial(_peer_gather_kernel, tt=tt, d=d, tokens=tokens)
    vec = pl.BlockSpec((1, d), lambda i: (0, 0))
    return pl.pallas_call(
        kern,
        grid=(tokens // tt,),
        in_specs=[pl.BlockSpec(memory_space=pl.ANY),
                  pl.BlockSpec(memory_space=pl.ANY),
                  pl.BlockSpec((nsel, tt), lambda i: (0, i)),
                  pl.BlockSpec((tt, d), lambda i: (i, 0)),
                  vec, vec],
        out_specs=pl.BlockSpec((tt, d), lambda i: (i, 0)),
        out_shape=jax.ShapeDtypeStruct((tokens, d), F32),
        scratch_shapes=[pltpu.SMEM((2 * tt * nsel,), jnp.int32),
                        pltpu.VMEM((tt, d), F32),
                        pltpu.SemaphoreType.DMA((2,)),
                        pltpu.SemaphoreType.DMA((PEER_SLOTS,))]
                       + [pltpu.VMEM((d // LANES, nsel, LANES), jnp.uint32) for _ in range(PEER_SLOTS)],
        compiler_params=_cparams(1),
        name="peer_gather_ln2",
    )(eid, uv, gate, h, g, b)


def _heads_major(t, heads, width):
    return t.reshape(t.shape[0], heads, width).transpose(1, 0, 2)


def kernel(x, w_in, da_lam_q, da_lam_k, da_subln_g, cmp_pe_k, cmp_w1_k, cmp_w2_k, cmp_pe_v, cmp_w1_v,
           cmp_w2_v, w_branch_da, w_branch_nsa, w_out, ln1_g, ln1_b, peer_wq, peer_subkey1, peer_subkey2,
           peer_u, peer_v, ln2_g, ln2_b, rel_bias):
    batch, seq, d_model = x.shape
    tokens = batch * seq
    g, hd = NSA_GROUPS, NSA_DIM
    table = rel_bias.astype(F32)
    xs = x.reshape(tokens, d_model)
    for l in range(DEPTH):
        lam_init = 0.8 - 0.6 * math.exp(-0.3 * l)
        xb = xs.astype(BF16)
        w = w_in[l]
        scale = DA_HEAD_DIM ** -0.5
        c_daq, c_dak, c_dav, c_nq = 0, 1024, 2048, 3072
        c_kv, c_gate, c_mg, c_end = 4096, 5632, 5680, 9776
        w_att = jnp.concatenate([w[:, c_daq:c_dak] * scale, w[:, c_dak:c_nq], w[:, c_nq:c_kv] * scale],
                                axis=1).astype(BF16)
        att = _matmul(xb, w_att, BF16, 512, 512)
        kv = _matmul(xb, w[:, c_kv:c_gate].astype(BF16), F32, 512, 512)
        w_gate = jnp.pad(w[:, c_gate:c_mg], ((0, 0), (0, LANES - (c_mg - c_gate)))).astype(BF16)
        br_gate = _matmul(xb, w_gate, F32, 512, LANES)[:, :c_mg - c_gate]
        mg_gate = _matmul(xb, w[:, c_mg:c_end].astype(BF16), F32, 512, 512)

        lam_e = jnp.exp(jnp.sum(da_lam_q[l].astype(F32) * da_lam_k[l].astype(F32), -1))
        lam = (lam_e[0] - lam_e[1] + lam_init).reshape(1)
        table_rel = table - table[REL_BUCKETS - 1]
        o_da = _diff_attention(att, lam, _causal_bias_tiles(table_rel[:, :DA_HEADS], DA_TILE),
                               da_subln_g[l].reshape(1, DA_V_DIM), batch, seq, lam_init)

        q_n = _heads_major(att[:, 3072:4096], NSA_HEADS, hd)
        kv6 = kv.reshape(tokens, 6, g * hd)
        kc = _compress(kv6[:, 0], cmp_pe_k[l], cmp_w1_k[l], cmp_w2_k[l], batch, seq)
        vc = _compress(kv6[:, 1], cmp_pe_v[l], cmp_w1_v[l], cmp_w2_v[l], batch, seq)
        grp = lambda t: t.astype(BF16).reshape(batch, seq, g, hd).transpose(2, 0, 1, 3)
        k_s, v_s, k_w, v_w = grp(kv6[:, 2]), grp(kv6[:, 3]), grp(kv6[:, 4]), grp(kv6[:, 5])

        ncp = seq // CMP_STRIDE
        ns = seq // SLC_BLOCK
        nsp = -(-ns // LANES) * LANES
        k_sel = min(SLC_TOPK, ns)
        cmp_start = jnp.arange(ncp) * CMP_STRIDE
        slc_start = jnp.arange(nsp) * SLC_BLOCK
        overlap = ((cmp_start[:, None] <= slc_start[None, :] + SLC_BLOCK - 1)
                   & (cmp_start[:, None] + CMP_BLOCK - 1 >= slc_start[None, :])).astype(BF16)
        o_cmp, sel = _cmp_select(q_n, kc, vc, overlap, br_gate, batch, seq, k_sel)

        win_bias = _window_bias_tile(table[:, DA_HEADS:], WIN_TQ, WINDOW + WIN_TQ)
        pad_w = lambda t: jnp.pad(t, ((0, 0), (0, 0), (WINDOW, 0), (0, 0)))
        o_win = _window_attention(q_n, pad_w(k_w), pad_w(v_w), win_bias, br_gate, batch, seq)

        v_s1 = jnp.concatenate([v_s, jnp.ones_like(v_s)], axis=-1)
        o_nsa = _slc_attention(q_n, k_s, v_s1, sel, _causal_bias_tiles(table_rel[:, DA_HEADS:], SLC_TILE),
                               br_gate, o_cmp, o_win, batch, seq)
        o_nsa = o_nsa.transpose(1, 0, 2).reshape(tokens, NSA_HEADS * hd)

        mixed = _merge(o_da, o_nsa, w_branch_da[l].astype(BF16), w_branch_nsa[l].astype(BF16), mg_gate)
        h, hb = _outproj_ln(mixed, w_out[l].astype(BF16), xs, ln1_g[l].reshape(1, -1), ln1_b[l].reshape(1, -1))

        pq = _matmul(hb, peer_wq[l].astype(BF16), BF16, 512, 512)
        half = peer_subkey1.shape[-1]
        zeros = jnp.zeros((PEER_NKEYS, half), F32)
        sk = jnp.concatenate([jnp.concatenate([peer_subkey1[l], zeros], axis=1),
                              jnp.concatenate([zeros, peer_subkey2[l]], axis=1)], axis=0).astype(BF16)
        gate, eid = _peer_topk(pq, sk)
        nsel = PEER_HEADS * PEER_TOPK
        eid_tok = eid.reshape(nsel, tokens).T.reshape(tokens // PEER_TT, PEER_TT * nsel)
        uv = jnp.concatenate([_pack_bf16_pairs(peer_u[l]), _pack_bf16_pairs(peer_v[l])], axis=1)
        uv = uv.reshape(uv.shape[0], d_model // LANES, LANES)
        xs = _peer_gather(eid_tok, uv, gate.reshape(nsel, tokens), h,
                          ln2_g[l].reshape(1, -1), ln2_b[l].reshape(1, -1))
    return xs.reshape(batch, seq, d_model)
```

```python
import functools
import math

import jax
import jax.numpy as jnp
from jax import lax
from jax.experimental import pallas as pl
from jax.experimental.pallas import tpu as pltpu

F32 = jnp.float32
BF16 = jnp.bfloat16

DA_HEADS = 8
DA_HEAD_DIM = 64
DA_V_DIM = 128
NSA_HEADS = 16
NSA_GROUPS = 4
NSA_HPG = 4
NSA_DIM = 64
CMP_BLOCK = 32
CMP_STRIDE = 16
SLC_BLOCK = 64
SLC_TOPK = 16
N_LOCAL_BLOCKS = 2
WINDOW = 512
REL_BUCKETS = 32
REL_MAX_DIST = 128
PEER_HEADS = 8
PEER_NKEYS = 128
PEER_TOPK = 16
DEPTH = 1
DN_ALPHA = (2 * DEPTH) ** 0.25
LN_EPS = 1e-5
NEG = -1e30
FORCE_SCORE = 1e4

LANES = 128
VMEM_LIMIT = 48 * 1024 * 1024
DA_TILE = 512
SLC_TILE = 512
WIN_TQ = 256
CMP_TQ = 512
PEER_TT = 128
PEER_SLOTS = 8
PEER_DMA_QUEUES = 2


def _cparams(n_axes):
    return pltpu.CompilerParams(dimension_semantics=("arbitrary",) * n_axes,
                                vmem_limit_bytes=VMEM_LIMIT)


def _dot_nt(a, b):
    return lax.dot_general(a, b, (((1,), (1,)), ((), ())), preferred_element_type=F32)


def _dot_exact01(x, onehot_bf16):
    hi = x.astype(BF16)
    r1 = x - hi.astype(F32)
    mid = r1.astype(BF16)
    lo = (r1 - mid.astype(F32)).astype(BF16)
    d = lambda a: jnp.dot(a, onehot_bf16, preferred_element_type=F32)
    return d(hi) + d(mid) + d(lo)


def _mm_kernel(a_ref, b_ref, o_ref):
    o_ref[...] = jnp.dot(a_ref[...], b_ref[...], preferred_element_type=F32).astype(o_ref.dtype)


def _matmul(a, b, out_dtype, tm, tn):
    m, k = a.shape
    n = b.shape[1]
    return pl.pallas_call(
        _mm_kernel,
        grid=(n // tn, m // tm),
        in_specs=[pl.BlockSpec((tm, k), lambda j, i: (i, 0)),
                  pl.BlockSpec((k, tn), lambda j, i: (0, j))],
        out_specs=pl.BlockSpec((tm, tn), lambda j, i: (i, j)),
        out_shape=jax.ShapeDtypeStruct((m, n), out_dtype),
        compiler_params=_cparams(2),
        name="matmul",
    )(a, b)


def _rel_bucket(dist):
    n = jnp.maximum(dist, 0)
    max_exact = REL_BUCKETS // 2
    nf = jnp.maximum(n, 1).astype(F32)
    large = max_exact + (jnp.log(nf / max_exact) / math.log(REL_MAX_DIST / max_exact)
                         * (REL_BUCKETS - max_exact)).astype(jnp.int32)
    large = jnp.minimum(large, REL_BUCKETS - 1)
    return jnp.where(n < max_exact, n, large)


def _toeplitz(rd, n_i, n_j):
    h, length = rd.shape
    rev = jnp.pad(rd[:, ::-1], ((0, 0), (0, 1)))
    flat = jnp.broadcast_to(rev[:, None, :], (h, n_i, length + 1)).reshape(h, n_i * (length + 1))
    skew = flat[:, :n_i * length].reshape(h, n_i, length)
    return skew[:, :, n_i - 1:n_i - 1 + n_j]


def _bias_by_distance(table, n):
    return table[_rel_bucket(jnp.arange(n))].T


def _causal_bias_tiles(table, t):
    bd = _bias_by_distance(table, 2 * t)
    neg = jnp.full((bd.shape[0], t - 1), NEG, F32)
    diag = _toeplitz(jnp.concatenate([neg, bd[:, :t]], axis=1), t, t)
    off = _toeplitz(bd[:, 1:], t, t)
    return jnp.stack([jnp.zeros_like(off), off, diag], axis=0)


def _window_bias_tile(table, tq, span):
    bd = _bias_by_distance(table, WINDOW)
    h = bd.shape[0]
    lo = jnp.full((h, span - 1 - WINDOW), NEG, F32)
    hi = jnp.full((h, tq), NEG, F32)
    return _toeplitz(jnp.concatenate([lo, bd, hi], axis=1), tq, span)


def _flash_init(m_scr, acc_scr):
    m_scr[...] = jnp.full(m_scr.shape, NEG, F32)
    acc_scr[...] = jnp.zeros(acc_scr.shape, F32)


def _flash_update(s, v1, m_scr, acc_scr):
    m_old = m_scr[...]
    m_new = jnp.maximum(m_old, jnp.broadcast_to(jnp.max(s, axis=-1, keepdims=True), m_old.shape))
    alpha = jnp.exp(m_old - m_new)
    widen = lambda x, width: jnp.concatenate([x] * (width // LANES), axis=1)
    p = jnp.exp(s - widen(m_new, s.shape[1])).astype(BF16)
    acc = acc_scr[...]
    acc_scr[...] = widen(alpha, acc.shape[1]) * acc + jnp.dot(p, v1, preferred_element_type=F32)
    m_scr[...] = m_new


def _pipelined_sweep(n_tiles, logits_fn, consume_fn, s_scr):
    last = n_tiles - 1
    s_scr[0] = logits_fn(0)

    def body(pair, carry):
        a = 2 * pair
        s_scr[1] = logits_fn(a + 1)
        consume_fn(a, 0)
        s_scr[0] = logits_fn(jnp.minimum(a + 2, last))
        consume_fn(a + 1, 1)
        return carry

    lax.fori_loop(0, n_tiles // 2, body, 0)

    @pl.when(n_tiles % 2 == 1)
    def _():
        consume_fn(last, 0)


def _near_kind(j, qi):
    return jnp.maximum(j - (qi - 2), 0)


def _head_gates(gate_ref, group, branch):
    sig = jax.nn.sigmoid(gate_ref[...])
    lane = lax.broadcasted_iota(jnp.int32, (1, sig.shape[-1]), 1)
    cols = [jnp.sum(jnp.where(lane == (group * NSA_HPG + h) * 3 + branch, sig, 0.0), axis=-1, keepdims=True)
            for h in range(NSA_HPG)]
    return jnp.stack(cols)


def _da_kernel(lam_ref, q_ref, k_ref, v_ref, bias_ref, g_ref, o_ref, m_scr, acc_scr, s_scr, *, t, lam_init):
    qi = pl.program_id(2)
    q = q_ref[...]
    lane = lax.broadcasted_iota(jnp.int32, (1, LANES), 1)
    zero = jnp.zeros_like(q)
    q2 = jnp.concatenate([jnp.where(lane < DA_HEAD_DIM, q, zero), jnp.where(lane >= DA_HEAD_DIM, q, zero)], axis=0)
    _flash_init(m_scr, acc_scr)

    def logits(j):
        return _dot_nt(q2, k_ref[pl.ds(pl.multiple_of(j * t, t), t), :])

    def consume(j, slot):
        s = (s_scr[slot].reshape(2, t, t) + bias_ref[_near_kind(j, qi)][None]).reshape(2 * t, t)
        v = v_ref[pl.ds(pl.multiple_of(j * t, t), t), :]
        v1 = jnp.concatenate([v, jnp.ones((t, LANES), BF16)], axis=-1)
        _flash_update(s, v1, m_scr, acc_scr)

    _pipelined_sweep(qi + 1, logits, consume, s_scr)

    acc = acc_scr[...]
    o = acc[:t, :DA_V_DIM] / acc[:t, DA_V_DIM:] - lam_ref[0] * (acc[t:, :DA_V_DIM] / acc[t:, DA_V_DIM:])
    ms = jnp.mean(o * o, axis=-1, keepdims=True)
    o = (o * lax.rsqrt(ms + LN_EPS)) * g_ref[...] * (1.0 - lam_init)
    o_ref[...] = o.astype(o_ref.dtype)


def _diff_attention(att, lam, bias, subln_g, batch, seq, lam_init):
    t = DA_TILE
    nq = seq // t
    kern = functools.partial(_da_kernel, t=t, lam_init=lam_init)
    return pl.pallas_call(
        kern,
        grid=(batch, DA_HEADS, nq),
        in_specs=[pl.BlockSpec(memory_space=pltpu.SMEM),
                  pl.BlockSpec((t, LANES), lambda b, h, i: (b * nq + i, h)),
                  pl.BlockSpec((seq, LANES), lambda b, h, i: (b, DA_HEADS + h)),
                  pl.BlockSpec((seq, LANES), lambda b, h, i: (b, 2 * DA_HEADS + h)),
                  pl.BlockSpec((3, None, t, t), lambda b, h, i: (0, h, 0, 0)),
                  pl.BlockSpec((1, DA_V_DIM), lambda b, h, i: (0, 0))],
        out_specs=pl.BlockSpec((t, DA_V_DIM), lambda b, h, i: (b * nq + i, h)),
        out_shape=jax.ShapeDtypeStruct((batch * seq, DA_HEADS * DA_V_DIM), BF16),
        scratch_shapes=[pltpu.VMEM((2 * t, LANES), F32), pltpu.VMEM((2 * t, 2 * DA_V_DIM), F32),
                        pltpu.VMEM((2, 2 * t, t), F32)],
        compiler_params=_cparams(3),
        name="diff_attention",
    )(lam, att, att, att, bias, subln_g)


def _compress_kernel(x_ref, pe_ref, w1_ref, w2_ref, o_ref):
    x = (x_ref[...].astype(F32) + pe_ref[...]).astype(BF16)
    hid = jax.nn.gelu(jnp.dot(x, w1_ref[...], preferred_element_type=F32))
    o_ref[...] = jnp.dot(hid.astype(BF16), w2_ref[...], preferred_element_type=F32).astype(o_ref.dtype)


def _compress(t, pe, w1, w2, batch, seq):
    g, d = NSA_GROUPS, NSA_DIM
    r = CMP_BLOCK // CMP_STRIDE
    nch = seq // CMP_STRIDE
    nc = nch - r + 1
    ch = t.reshape(batch, nch, CMP_STRIDE, g, d)
    blocks = jnp.concatenate([ch[:, j:j + nc] for j in range(r)], axis=2)
    flat = blocks.transpose(0, 1, 3, 2, 4).reshape(batch * nc * g, CMP_BLOCK * d)
    rows = flat.shape[0]
    tm = 512
    rows_p = -(-rows // tm) * tm
    flat = jnp.pad(flat, ((0, rows_p - rows), (0, 0)))
    hidden = w1.shape[1]
    out = pl.pallas_call(
        _compress_kernel,
        grid=(rows_p // tm,),
        in_specs=[pl.BlockSpec((tm, CMP_BLOCK * d), lambda i: (i, 0)),
                  pl.BlockSpec((1, CMP_BLOCK * d), lambda i: (0, 0)),
                  pl.BlockSpec((CMP_BLOCK * d, hidden), lambda i: (0, 0)),
                  pl.BlockSpec((hidden, d), lambda i: (0, 0))],
        out_specs=pl.BlockSpec((tm, d), lambda i: (i, 0)),
        out_shape=jax.ShapeDtypeStruct((rows_p, d), BF16),
        compiler_params=_cparams(1),
        name="compress_mlp",
    )(flat, pe.reshape(1, CMP_BLOCK * d), w1.astype(BF16), w2.astype(BF16))
    out = out[:rows].reshape(batch, nc, g, d).transpose(2, 0, 1, 3)
    return jnp.pad(out, ((0, 0), (0, 0), (0, nch - nc), (0, 0)))


def _cmp_select_kernel(q_ref, kc_ref, vc_ref, ov_ref, gate_ref, o_ref, sel_ref, *, tq, k_sel):
    qi = pl.program_id(2)
    ncp = kc_ref.shape[0]
    nsp = sel_ref.shape[-1]
    q = q_ref[...].reshape(NSA_HPG * tq, NSA_DIM)
    q_pos = qi * tq + lax.broadcasted_iota(jnp.int32, (tq, 1), 0)
    c_end = lax.broadcasted_iota(jnp.int32, (1, ncp), 1) * CMP_STRIDE + (CMP_BLOCK - 1)
    cmask = (c_end <= q_pos)[None]
    s = _dot_nt(q, kc_ref[...]).reshape(NSA_HPG, tq, ncp)
    s = jnp.where(cmask, s, NEG)
    m = jnp.max(s, axis=-1, keepdims=True)
    p = jnp.where(cmask, jnp.exp(s - m), 0.0)
    p = p / jnp.maximum(jnp.sum(p, axis=-1, keepdims=True), 1e-30)
    o = jnp.dot(p.reshape(NSA_HPG * tq, ncp).astype(BF16), vc_ref[...], preferred_element_type=F32)
    o_ref[...] = o.reshape(NSA_HPG, tq, NSA_DIM) * _head_gates(gate_ref, pl.program_id(0), 0)

    psum = p[0] + p[1] + p[2] + p[3]
    imp = _dot_exact01(psum, ov_ref[...])
    blk = lax.broadcasted_iota(jnp.int32, (1, nsp), 1)
    cur = q_pos // SLC_BLOCK
    valid = blk <= cur
    forced = valid & ((blk == 0) | (blk > cur - N_LOCAL_BLOCKS))
    score = jnp.where(forced, FORCE_SCORE, jnp.where(valid, imp, NEG))
    blk_f = blk.astype(F32)
    sel = jnp.zeros((tq, nsp), F32)
    for _ in range(k_sel):
        mx = jnp.max(score, axis=-1, keepdims=True)
        idx = jnp.min(jnp.where(score == mx, blk_f, float(nsp)), axis=-1, keepdims=True)
        hit = blk_f == idx
        sel = jnp.where(hit, 1.0, sel)
        score = jnp.where(hit, -jnp.inf, score)
    sel_ref[...] = sel.astype(sel_ref.dtype)


def _cmp_select(q, kc, vc, overlap, gate, batch, seq, k_sel):
    tq = CMP_TQ
    nq = seq // tq
    ncp = kc.shape[2]
    nsp = overlap.shape[1]
    kern = functools.partial(_cmp_select_kernel, tq=tq, k_sel=k_sel)
    return pl.pallas_call(
        kern,
        grid=(NSA_GROUPS, batch, nq),
        in_specs=[pl.BlockSpec((NSA_HPG, tq, NSA_DIM), lambda g, b, i: (g, b * nq + i, 0)),
                  pl.BlockSpec((None, None, ncp, NSA_DIM), lambda g, b, i: (g, b, 0, 0)),
                  pl.BlockSpec((None, None, ncp, NSA_DIM), lambda g, b, i: (g, b, 0, 0)),
                  pl.BlockSpec((ncp, nsp), lambda g, b, i: (0, 0)),
                  pl.BlockSpec((tq, gate.shape[1]), lambda g, b, i: (b * nq + i, 0))],
        out_specs=[pl.BlockSpec((NSA_HPG, tq, NSA_DIM), lambda g, b, i: (g, b * nq + i, 0)),
                   pl.BlockSpec((None, tq, nsp), lambda g, b, i: (g, b * nq + i, 0))],
        out_shape=[jax.ShapeDtypeStruct((NSA_HEADS, batch * seq, NSA_DIM), F32),
                   jax.ShapeDtypeStruct((NSA_GROUPS, batch * seq, nsp), BF16)],
        compiler_params=_cparams(3),
        name="nsa_cmp_select",
    )(q, kc, vc, overlap, gate)


def _window_kernel(q_ref, k_ref, v_ref, bias_ref, gate_ref, o_ref, *, tq, span):
    qi = pl.program_id(2)
    q0 = pl.multiple_of(qi * tq, tq)
    q = q_ref[...].reshape(NSA_HPG * tq, NSA_DIM)
    ks = k_ref[pl.ds(q0, span), :]
    vs = v_ref[pl.ds(q0, span), :]
    s = _dot_nt(q, ks).reshape(NSA_HPG, tq, span) + bias_ref[...]
    in_seq = (lax.broadcasted_iota(jnp.int32, (1, 1, span), 2) + q0) >= WINDOW
    s = jnp.where(in_seq, s, NEG)
    m = jnp.max(s, axis=-1, keepdims=True)
    p = jnp.exp(s - m)
    l = jnp.sum(p, axis=-1, keepdims=True)
    o = jnp.dot(p.reshape(NSA_HPG * tq, span).astype(BF16), vs, preferred_element_type=F32)
    o = o.reshape(NSA_HPG, tq, NSA_DIM) / l
    o_ref[...] = o * _head_gates(gate_ref, pl.program_id(0), 2)


def _window_attention(q, kp, vp, bias, gate, batch, seq):
    tq = WIN_TQ
    span = WINDOW + tq
    nq = seq // tq
    kern = functools.partial(_window_kernel, tq=tq, span=span)
    return pl.pallas_call(
        kern,
        grid=(NSA_GROUPS, batch, nq),
        in_specs=[pl.BlockSpec((NSA_HPG, tq, NSA_DIM), lambda g, b, i: (g, b * nq + i, 0)),
                  pl.BlockSpec((None, None, WINDOW + seq, NSA_DIM), lambda g, b, i: (g, b, 0, 0)),
                  pl.BlockSpec((None, None, WINDOW + seq, NSA_DIM), lambda g, b, i: (g, b, 0, 0)),
                  pl.BlockSpec((NSA_HPG, tq, span), lambda g, b, i: (g, 0, 0)),
                  pl.BlockSpec((tq, gate.shape[1]), lambda g, b, i: (b * nq + i, 0))],
        out_specs=pl.BlockSpec((NSA_HPG, tq, NSA_DIM), lambda g, b, i: (g, b * nq + i, 0)),
        out_shape=jax.ShapeDtypeStruct((NSA_HEADS, batch * seq, NSA_DIM), F32),
        compiler_params=_cparams(3),
        name="nsa_window",
    )(q, kp, vp, bias, gate)


def _slc_kernel(q_ref, k_ref, v_ref, sel_ref, bias_ref, gate_ref, oc_ref, ow_ref, o_ref,
                m_scr, acc_scr, s_scr, *, t):
    g = pl.program_id(0)
    qi = pl.program_id(2)
    nsp = sel_ref.shape[-1]
    rows = NSA_HPG * t
    q = q_ref[...].reshape(rows, NSA_DIM)
    sel = sel_ref[...]
    _flash_init(m_scr, acc_scr)
    blk_row = lax.broadcasted_iota(jnp.int32, (nsp, t), 0)
    key_blk = lax.broadcasted_iota(jnp.int32, (nsp, t), 1) // SLC_BLOCK

    def logits(j):
        return _dot_nt(q, k_ref[pl.ds(pl.multiple_of(j * t, t), t), :])

    def consume(j, slot):
        expand = jnp.where(blk_row == key_blk + j * (t // SLC_BLOCK), 1.0, 0.0).astype(BF16)
        off = (jnp.dot(sel, expand, preferred_element_type=F32) - 1.0) * (-NEG)
        s = s_scr[slot].reshape(NSA_HPG, t, t) + (off[None] + bias_ref[_near_kind(j, qi)])
        v1 = v_ref[pl.ds(pl.multiple_of(j * t, t), t), :]
        _flash_update(s.reshape(rows, t), v1, m_scr, acc_scr)

    _pipelined_sweep(qi + 1, logits, consume, s_scr)

    acc = acc_scr[...]
    o = (acc[:, :NSA_DIM] / acc[:, NSA_DIM:]).reshape(NSA_HPG, t, NSA_DIM)
    o = oc_ref[...] + ow_ref[...] + o * _head_gates(gate_ref, g, 1)
    o_ref[...] = o.astype(o_ref.dtype)


def _slc_attention(q, ks, vs1, sel, bias, gate, o_cmp, o_win, batch, seq):
    t = SLC_TILE
    nq = seq // t
    nsp = sel.shape[-1]
    kern = functools.partial(_slc_kernel, t=t)
    head_blk = lambda last: pl.BlockSpec((NSA_HPG, t, last), lambda g, b, i: (g, b * nq + i, 0))
    return pl.pallas_call(
        kern,
        grid=(NSA_GROUPS, batch, nq),
        in_specs=[head_blk(NSA_DIM),
                  pl.BlockSpec((None, None, seq, NSA_DIM), lambda g, b, i: (g, b, 0, 0)),
                  pl.BlockSpec((None, None, seq, 2 * NSA_DIM), lambda g, b, i: (g, b, 0, 0)),
                  pl.BlockSpec((None, t, nsp), lambda g, b, i: (g, b * nq + i, 0)),
                  pl.BlockSpec((3, NSA_HPG, t, t), lambda g, b, i: (0, g, 0, 0), pipeline_mode=pl.Buffered(1)),
                  pl.BlockSpec((t, gate.shape[1]), lambda g, b, i: (b * nq + i, 0)),
                  head_blk(NSA_DIM), head_blk(NSA_DIM)],
        out_specs=head_blk(NSA_DIM),
        out_shape=jax.ShapeDtypeStruct((NSA_HEADS, batch * seq, NSA_DIM), BF16),
        scratch_shapes=[pltpu.VMEM((NSA_HPG * t, LANES), F32), pltpu.VMEM((NSA_HPG * t, 2 * NSA_DIM), F32),
                        pltpu.VMEM((2, NSA_HPG * t, t), F32)],
        compiler_params=_cparams(3),
        name="nsa_selected",
    )(q, ks, vs1, sel, bias, gate, o_cmp, o_win)


def _merge_kernel(x_ref, oda_ref, onsa_ref, wgda_ref, wgnsa_ref, wda_ref, wnsa_ref, o_ref):
    x = x_ref[...]
    gate_da = jnp.dot(x, wgda_ref[...], preferred_element_type=F32)
    gate_nsa = jnp.dot(x, wgnsa_ref[...], preferred_element_type=F32)
    a = jnp.dot(oda_ref[...], wda_ref[...], preferred_element_type=F32)
    n = jnp.dot(onsa_ref[...], wnsa_ref[...], preferred_element_type=F32)
    mixed = jax.nn.sigmoid(gate_da) * a + jax.nn.sigmoid(gate_nsa) * n
    o_ref[...] = mixed.astype(o_ref.dtype)


def _merge(xb, o_da, o_nsa, w_gates, w_da, w_nsa):
    tokens, kd = o_da.shape
    d = w_da.shape[1]
    tm, tn = 512, 512
    nb = d // tn
    return pl.pallas_call(
        _merge_kernel,
        grid=(tokens // tm, nb),
        in_specs=[pl.BlockSpec((tm, d), lambda i, j: (i, 0)),
                  pl.BlockSpec((tm, kd), lambda i, j: (i, 0)),
                  pl.BlockSpec((tm, kd), lambda i, j: (i, 0)),
                  pl.BlockSpec((d, tn), lambda i, j: (0, j)),
                  pl.BlockSpec((d, tn), lambda i, j: (0, nb + j)),
                  pl.BlockSpec((kd, tn), lambda i, j: (0, j)),
                  pl.BlockSpec((kd, tn), lambda i, j: (0, j))],
        out_specs=pl.BlockSpec((tm, tn), lambda i, j: (i, j)),
        out_shape=jax.ShapeDtypeStruct((tokens, d), BF16),
        compiler_params=_cparams(2),
        name="gated_merge",
    )(xb, o_da, o_nsa, w_gates, w_gates, w_da, w_nsa)


def _layer_norm(z, g, b):
    mu = jnp.mean(z, axis=-1, keepdims=True)
    zc = z - mu
    var = jnp.mean(zc * zc, axis=-1, keepdims=True)
    return (zc * lax.rsqrt(var + LN_EPS)) * g + b


def _outproj_ln_kernel(mixed_ref, w_ref, x_ref, g_ref, b_ref, wq_ref, h_ref, pq_ref):
    z = DN_ALPHA * x_ref[...] + jnp.dot(mixed_ref[...], w_ref[...], preferred_element_type=F32)
    h = _layer_norm(z, g_ref[...], b_ref[...])
    h_ref[...] = h
    pq_ref[...] = jnp.dot(h.astype(BF16), wq_ref[...], preferred_element_type=F32).astype(pq_ref.dtype)


def _outproj_ln(mixed, w_out, x, g, b, w_q):
    tokens, d = x.shape
    nq = w_q.shape[1]
    tm = 256
    row = pl.BlockSpec((tm, d), lambda i: (i, 0))
    vec = pl.BlockSpec((1, d), lambda i: (0, 0))
    resident = lambda shape: pl.BlockSpec(shape, lambda i: (0, 0), pipeline_mode=pl.Buffered(1))
    return pl.pallas_call(
        _outproj_ln_kernel,
        grid=(tokens // tm,),
        in_specs=[row, resident((d, d)), row, vec, vec, resident((d, nq))],
        out_specs=[row, pl.BlockSpec((tm, nq), lambda i: (i, 0))],
        out_shape=[jax.ShapeDtypeStruct((tokens, d), F32), jax.ShapeDtypeStruct((tokens, nq), BF16)],
        compiler_params=_cparams(1),
        name="outproj_ln1",
    )(mixed, w_out, x, g, b, w_q)


def _topk_rows(x, k, ids=None):
    n, t = x.shape
    if ids is None:
        ids = lax.broadcasted_iota(jnp.int32, (n, t), 0).astype(F32)
    slot = lax.broadcasted_iota(jnp.int32, (k, t), 0)
    vals = jnp.zeros((k, t), F32)
    idxs = jnp.zeros((k, t), F32)
    for r in range(k):
        mx = jnp.max(x, axis=0, keepdims=True)
        idx = jnp.min(jnp.where(x == mx, ids, jnp.inf), axis=0, keepdims=True)
        vals = jnp.where(slot == r, mx, vals)
        idxs = jnp.where(slot == r, idx, idxs)
        x = jnp.where(ids == idx, -jnp.inf, x)
    return vals, idxs


def _pair_candidates(v1, i1, v2, i2, nk):
    k, t = v1.shape
    sums, ids, eids = [], [], []
    for c in range(k):
        top = k // (c + 1) - 1
        for fixed_second in (True, False):
            lo = c if fixed_second else c + 1
            if lo > top:
                continue
            n = 8 if top < 8 else k
            run = lax.broadcasted_iota(jnp.int32, (n, t), 0)
            keep = (run >= lo) & (run <= top)
            runf = run.astype(F32)
            if fixed_second:
                val, pid = v1[:n] + v2[c:c + 1], runf * float(k) + float(c)
                eid = i1[:n] * float(nk) + i2[c:c + 1]
            else:
                val, pid = v1[c:c + 1] + v2[:n], float(c * k) + runf
                eid = i1[c:c + 1] * float(nk) + i2[:n]
            sums.append(jnp.where(keep, val, -jnp.inf))
            ids.append(pid)
            eids.append(eid)
    return jnp.concatenate(sums, axis=0), jnp.concatenate(ids, axis=0), jnp.concatenate(eids, axis=0)


def _peer_topk_kernel(q_ref, sk_ref, gate_ref, eid_ref, *, tm):
    k = PEER_TOPK
    nk = PEER_NKEYS
    for h in range(PEER_HEADS):
        qh = q_ref[:, h * LANES:(h + 1) * LANES]
        st = _dot_nt(sk_ref[...], qh)
        v1, i1 = _topk_rows(st[:nk], k)
        v2, i2 = _topk_rows(st[nk:], k)
        cand, ids, cid = _pair_candidates(v1, i1, v2, i2, nk)
        sc, j = _topk_rows(cand, k, ids)
        slot = lax.broadcasted_iota(jnp.int32, (k, tm), 0)
        eid = jnp.zeros((k, tm), F32)
        for r in range(k):
            picked = jnp.max(jnp.where(ids == j[r:r + 1], cid, -1.0), axis=0, keepdims=True)
            eid = jnp.where(slot == r, picked, eid)
        e = jnp.exp(sc - sc[0:1])
        gate_ref[h] = e / jnp.sum(e, axis=0, keepdims=True)
        eid_ref[h] = eid.astype(jnp.int32)


def _peer_topk(q, sk):
    tokens = q.shape[0]
    tm = 256
    kern = functools.partial(_peer_topk_kernel, tm=tm)
    out_blk = pl.BlockSpec((PEER_HEADS, PEER_TOPK, tm), lambda i: (0, 0, i))
    return pl.pallas_call(
        kern,
        grid=(tokens // tm,),
        in_specs=[pl.BlockSpec((tm, PEER_HEADS * LANES), lambda i: (i, 0)),
                  pl.BlockSpec((2 * PEER_NKEYS, LANES), lambda i: (0, 0))],
        out_specs=[out_blk, out_blk],
        out_shape=[jax.ShapeDtypeStruct((PEER_HEADS, PEER_TOPK, tokens), F32),
                   jax.ShapeDtypeStruct((PEER_HEADS, PEER_TOPK, tokens), jnp.int32)],
        compiler_params=_cparams(1),
        name="peer_topk",
    )(q, sk)


def _pack_bf16_pairs(t):
    bits = lax.bitcast_convert_type(t.astype(BF16), jnp.uint16).astype(jnp.uint32)
    half = t.shape[1] // 2
    return bits[:, :half] | (bits[:, half:] << 16)


def _unpack_bf16_pairs(w):
    return pltpu.bitcast(w << 16, F32), pltpu.bitcast(w & jnp.uint32(0xFFFF0000), F32)


def _peer_gather_kernel(eid_hbm, uv_hbm, gate_ref, h_ref, g_ref, b_ref, o_ref, idx_smem, acc, idx_sem, row_sem,
                        *rows, tt, d, tokens):
    i = pl.program_id(0)
    nsteps = pl.num_programs(0)
    nsel = PEER_HEADS * PEER_TOPK
    ns = len(rows)
    per = tt * nsel
    half = d // 2
    nchunk = half // LANES

    def idx_copy(step):
        dst = idx_smem.at[pl.ds(pl.multiple_of((step % 2) * per, per), per)]
        return pltpu.make_async_copy(eid_hbm.at[step], dst, idx_sem.at[step % 2])

    def issue(gtok, slot, first=0, count=None):
        base = (jnp.minimum(gtok, tokens - 1) % (2 * tt)) * nsel
        for j in range(first, first + (nsel if count is None else count)):
            e = idx_smem[base + j]
            pltpu.make_async_copy(uv_hbm.at[e], rows[slot].at[:, j, :], row_sem.at[slot]).start(priority=j % PEER_DMA_QUEUES)

    def wait_slot(slot):
        pltpu.make_async_copy(rows[slot], rows[slot], row_sem.at[slot]).wait()

    @pl.when(i == 0)
    def _():
        first = idx_copy(0)
        first.start()
        first.wait()
        for s in range(ns - 1):
            issue(s, s)

    @pl.when(i + 1 < nsteps)
    def _():
        idx_copy(i + 1).start()

    lane = lax.broadcasted_iota(jnp.int32, (nsel, tt), 1)

    per_chunk = nsel // (2 * nchunk)

    def compute_and_issue(tok, slot, ahead, ahead_slot):
        buf = rows[slot]
        y = h_ref[pl.ds(tok, 1), :]
        part = None
        for c in range(nchunk):
            issue(ahead, ahead_slot, c * per_chunk, per_chunk)
            lo, hi = _unpack_bf16_pairs(buf[c])
            term = lo * y[:, c * LANES:(c + 1) * LANES] + hi * y[:, half + c * LANES:half + (c + 1) * LANES]
            part = term if part is None else part + term
        act = jax.nn.gelu(jnp.sum(part, axis=-1, keepdims=True))
        gcol = jnp.sum(jnp.where(lane == tok, gate_ref[...], 0.0), axis=-1, keepdims=True)
        w = gcol * act
        out_lo, out_hi = [], []
        for c in range(nchunk):
            issue(ahead, ahead_slot, (nchunk + c) * per_chunk, per_chunk)
            lo, hi = _unpack_bf16_pairs(buf[nchunk + c])
            out_lo.append(jnp.sum(w * lo, axis=0, keepdims=True))
            out_hi.append(jnp.sum(w * hi, axis=0, keepdims=True))
        acc[pl.ds(tok, 1), :] = jnp.concatenate(out_lo + out_hi, axis=-1)

    n_groups = tt // ns

    def body(grp, carry):
        @pl.when((grp == n_groups - 1) & (i + 1 < nsteps))
        def _():
            idx_copy(i + 1).wait()

        for s in range(ns):
            tok = grp * ns + s
            wait_slot(s)
            compute_and_issue(tok, s, i * tt + tok + ns - 1, (s + ns - 1) % ns)
        return carry

    lax.fori_loop(0, n_groups, body, 0)

    @pl.when(i == nsteps - 1)
    def _():
        for s in range(ns - 1):
            wait_slot((tt + s) % ns)

    z = DN_ALPHA * h_ref[...] + acc[...]
    o_ref[...] = _layer_norm(z, g_ref[...], b_ref[...])


def _peer_gather(eid, uv, gate, h, g, b):
    tokens, d = h.shape
    tt = PEER_TT
    nsel = PEER_HEADS * PEER_TOPK
    kern = functools.partial(_peer_gather_kernel, tt=tt, d=d, tokens=tokens)
    vec = pl.BlockSpec((1, d), lambda i: (0, 0))
    return pl.pallas_call(
        kern,
        grid=(tokens // tt,),
        in_specs=[pl.BlockSpec(memory_space=pl.ANY),
                  pl.BlockSpec(memory_space=pl.ANY),
                  pl.BlockSpec((nsel, tt), lambda i: (0, i)),
                  pl.BlockSpec((tt, d), lambda i: (i, 0)),
                  vec, vec],
        out_specs=pl.BlockSpec((tt, d), lambda i: (i, 0)),
        out_shape=jax.ShapeDtypeStruct((tokens, d), F32),
        scratch_shapes=[pltpu.SMEM((2 * tt * nsel,), jnp.int32),
                        pltpu.VMEM((tt, d), F32),
                        pltpu.SemaphoreType.DMA((2,)),
                        pltpu.SemaphoreType.DMA((PEER_SLOTS,))]
                       + [pltpu.VMEM((d // LANES, nsel, LANES), jnp.uint32) for _ in range(PEER_SLOTS)],
        compiler_params=_cparams(1),
        name="peer_gather_ln2",
    )(eid, uv, gate, h, g, b)


def _heads_major(t, heads, width):
    return t.reshape(t.shape[0], heads, width).transpose(1, 0, 2)


def kernel(x, w_in, da_lam_q, da_lam_k, da_subln_g, cmp_pe_k, cmp_w1_k, cmp_w2_k, cmp_pe_v, cmp_w1_v,
           cmp_w2_v, w_branch_da, w_branch_nsa, w_out, ln1_g, ln1_b, peer_wq, peer_subkey1, peer_subkey2,
           peer_u, peer_v, ln2_g, ln2_b, rel_bias):
    batch, seq, d_model = x.shape
    tokens = batch * seq
    g, hd = NSA_GROUPS, NSA_DIM
    table = rel_bias.astype(F32)
    xs = x.reshape(tokens, d_model)
    for l in range(DEPTH):
        lam_init = 0.8 - 0.6 * math.exp(-0.3 * l)
        xb = xs.astype(BF16)
        w = w_in[l]
        scale = DA_HEAD_DIM ** -0.5
        c_daq, c_dak, c_dav, c_nq = 0, 1024, 2048, 3072
        c_kv, c_gate, c_mg, c_end = 4096, 5632, 5680, 9776
        w_att = jnp.concatenate([w[:, c_daq:c_dak] * scale, w[:, c_dak:c_nq], w[:, c_nq:c_kv] * scale],
                                axis=1).astype(BF16)
        att = _matmul(xb, w_att, BF16, 1024, 512)
        kv = _matmul(xb, w[:, c_kv:c_gate].astype(BF16), F32, 512, 512)
        w_gate = jnp.pad(w[:, c_gate:c_mg], ((0, 0), (0, LANES - (c_mg - c_gate)))).astype(BF16)
        br_gate = _matmul(xb, w_gate, F32, 512, LANES)[:, :c_mg - c_gate]

        lam_e = jnp.exp(jnp.sum(da_lam_q[l].astype(F32) * da_lam_k[l].astype(F32), -1))
        lam = (lam_e[0] - lam_e[1] + lam_init).reshape(1)
        table_rel = table - table[REL_BUCKETS - 1]
        o_da = _diff_attention(att, lam, _causal_bias_tiles(table_rel[:, :DA_HEADS], DA_TILE),
                               da_subln_g[l].reshape(1, DA_V_DIM), batch, seq, lam_init)

        q_n = _heads_major(att[:, 3072:4096], NSA_HEADS, hd)
        kv6 = kv.reshape(tokens, 6, g * hd)
        kc = _compress(kv6[:, 0], cmp_pe_k[l], cmp_w1_k[l], cmp_w2_k[l], batch, seq)
        vc = _compress(kv6[:, 1], cmp_pe_v[l], cmp_w1_v[l], cmp_w2_v[l], batch, seq)
        grp = lambda t: t.astype(BF16).reshape(batch, seq, g, hd).transpose(2, 0, 1, 3)
        k_s, v_s, k_w, v_w = grp(kv6[:, 2]), grp(kv6[:, 3]), grp(kv6[:, 4]), grp(kv6[:, 5])

        ncp = seq // CMP_STRIDE
        ns = seq // SLC_BLOCK
        nsp = -(-ns // LANES) * LANES
        k_sel = min(SLC_TOPK, ns)
        cmp_start = jnp.arange(ncp) * CMP_STRIDE
        slc_start = jnp.arange(nsp) * SLC_BLOCK
        overlap = ((cmp_start[:, None] <= slc_start[None, :] + SLC_BLOCK - 1)
                   & (cmp_start[:, None] + CMP_BLOCK - 1 >= slc_start[None, :])).astype(BF16)
        o_cmp, sel = _cmp_select(q_n, kc, vc, overlap, br_gate, batch, seq, k_sel)

        win_bias = _window_bias_tile(table[:, DA_HEADS:], WIN_TQ, WINDOW + WIN_TQ)
        pad_w = lambda t: jnp.pad(t, ((0, 0), (0, 0), (WINDOW, 0), (0, 0)))
        o_win = _window_attention(q_n, pad_w(k_w), pad_w(v_w), win_bias, br_gate, batch, seq)

        v_s1 = jnp.concatenate([v_s, jnp.ones_like(v_s)], axis=-1)
        o_nsa = _slc_attention(q_n, k_s, v_s1, sel, _causal_bias_tiles(table_rel[:, DA_HEADS:], SLC_TILE),
                               br_gate, o_cmp, o_win, batch, seq)
        o_nsa = o_nsa.transpose(1, 0, 2).reshape(tokens, NSA_HEADS * hd)

        mixed = _merge(xb, o_da, o_nsa, w[:, c_mg:c_end].astype(BF16),
                       w_branch_da[l].astype(BF16), w_branch_nsa[l].astype(BF16))
        h, pq = _outproj_ln(mixed, w_out[l].astype(BF16), xs, ln1_g[l].reshape(1, -1), ln1_b[l].reshape(1, -1),
                            peer_wq[l].astype(BF16))

        half = peer_subkey1.shape[-1]
        zeros = jnp.zeros((PEER_NKEYS, half), F32)
        sk = jnp.concatenate([jnp.concatenate([peer_subkey1[l], zeros], axis=1),
                              jnp.concatenate([zeros, peer_subkey2[l]], axis=1)], axis=0).astype(BF16)
        gate, eid = _peer_topk(pq, sk)
        nsel = PEER_HEADS * PEER_TOPK
        eid_tok = eid.reshape(nsel, tokens).T.reshape(tokens // PEER_TT, PEER_TT * nsel)
        uv = jnp.concatenate([_pack_bf16_pairs(peer_u[l]), _pack_bf16_pairs(peer_v[l])], axis=1)
        uv = uv.reshape(uv.shape[0], d_model // LANES, LANES)
        xs = _peer_gather(eid_tok, uv, gate.reshape(nsel, tokens), h,
                          ln2_g[l].reshape(1, -1), ln2_b[l].reshape(1, -1))
    return xs.reshape(batch, seq, d_model)
```

```python
import functools
import math

import jax
import jax.numpy as jnp
from jax import lax
from jax.experimental import pallas as pl
from jax.experimental.pallas import tpu as pltpu

F32 = jnp.float32
BF16 = jnp.bfloat16

DA_HEADS = 8
DA_HEAD_DIM = 64
DA_V_DIM = 128
NSA_HEADS = 16
NSA_GROUPS = 4
NSA_HPG = 4
NSA_DIM = 64
CMP_BLOCK = 32
CMP_STRIDE = 16
SLC_BLOCK = 64
SLC_TOPK = 16
N_LOCAL_BLOCKS = 2
WINDOW = 512
REL_BUCKETS = 32
REL_MAX_DIST = 128
PEER_HEADS = 8
PEER_NKEYS = 128
PEER_TOPK = 16
DEPTH = 1
DN_ALPHA = (2 * DEPTH) ** 0.25
LN_EPS = 1e-5
NEG = -1e30
FORCE_SCORE = 1e4

LANES = 128
VMEM_LIMIT = 48 * 1024 * 1024
DA_TILE = 512
SLC_TILE = 512
WIN_TQ = 256
CMP_TQ = 512
PEER_TT = 128
PEER_SLOTS = 8
PEER_DMA_QUEUES = 2


def _cparams(n_axes):
    return pltpu.CompilerParams(dimension_semantics=("arbitrary",) * n_axes,
                                vmem_limit_bytes=VMEM_LIMIT)


def _dot_nt(a, b):
    return lax.dot_general(a, b, (((1,), (1,)), ((), ())), preferred_element_type=F32)


def _dot_exact01(x, onehot_bf16):
    hi = x.astype(BF16)
    r1 = x - hi.astype(F32)
    mid = r1.astype(BF16)
    lo = (r1 - mid.astype(F32)).astype(BF16)
    d = lambda a: jnp.dot(a, onehot_bf16, preferred_element_type=F32)
    return d(hi) + d(mid) + d(lo)


def _mm_kernel(a_ref, b_ref, o_ref):
    o_ref[...] = jnp.dot(a_ref[...], b_ref[...], preferred_element_type=F32).astype(o_ref.dtype)


def _matmul(a, b, out_dtype, tm, tn):
    m, k = a.shape
    n = b.shape[1]
    return pl.pallas_call(
        _mm_kernel,
        grid=(n // tn, m // tm),
        in_specs=[pl.BlockSpec((tm, k), lambda j, i: (i, 0)),
                  pl.BlockSpec((k, tn), lambda j, i: (0, j))],
        out_specs=pl.BlockSpec((tm, tn), lambda j, i: (i, j)),
        out_shape=jax.ShapeDtypeStruct((m, n), out_dtype),
        compiler_params=_cparams(2),
        name="matmul",
    )(a, b)


def _rel_bucket(dist):
    n = jnp.maximum(dist, 0)
    max_exact = REL_BUCKETS // 2
    nf = jnp.maximum(n, 1).astype(F32)
    large = max_exact + (jnp.log(nf / max_exact) / math.log(REL_MAX_DIST / max_exact)
                         * (REL_BUCKETS - max_exact)).astype(jnp.int32)
    large = jnp.minimum(large, REL_BUCKETS - 1)
    return jnp.where(n < max_exact, n, large)


def _toeplitz(rd, n_i, n_j):
    h, length = rd.shape
    rev = jnp.pad(rd[:, ::-1], ((0, 0), (0, 1)))
    flat = jnp.broadcast_to(rev[:, None, :], (h, n_i, length + 1)).reshape(h, n_i * (length + 1))
    skew = flat[:, :n_i * length].reshape(h, n_i, length)
    return skew[:, :, n_i - 1:n_i - 1 + n_j]


def _bias_by_distance(table, n):
    return table[_rel_bucket(jnp.arange(n))].T


def _causal_bias_tiles(table, t):
    bd = _bias_by_distance(table, 2 * t)
    neg = jnp.full((bd.shape[0], t - 1), NEG, F32)
    diag = _toeplitz(jnp.concatenate([neg, bd[:, :t]], axis=1), t, t)
    off = _toeplitz(bd[:, 1:], t, t)
    return jnp.stack([jnp.zeros_like(off), off, diag], axis=0)


def _window_bias_tile(table, tq, span):
    bd = _bias_by_distance(table, WINDOW)
    h = bd.shape[0]
    lo = jnp.full((h, span - 1 - WINDOW), NEG, F32)
    hi = jnp.full((h, tq), NEG, F32)
    return _toeplitz(jnp.concatenate([lo, bd, hi], axis=1), tq, span)


def _flash_init(m_scr, acc_scr):
    m_scr[...] = jnp.full(m_scr.shape, NEG, F32)
    acc_scr[...] = jnp.zeros(acc_scr.shape, F32)


def _flash_update(s, v1, m_scr, acc_scr):
    m_old = m_scr[...]
    m_new = jnp.maximum(m_old, jnp.broadcast_to(jnp.max(s, axis=-1, keepdims=True), m_old.shape))
    alpha = jnp.exp(m_old - m_new)
    widen = lambda x, width: jnp.concatenate([x] * (width // LANES), axis=1)
    p = jnp.exp(s - widen(m_new, s.shape[1])).astype(BF16)
    acc = acc_scr[...]
    acc_scr[...] = widen(alpha, acc.shape[1]) * acc + jnp.dot(p, v1, preferred_element_type=F32)
    m_scr[...] = m_new


def _pipelined_sweep(n_tiles, n_far, logits_fn, consume_far_fn, consume_fn, s_scr):
    last = n_tiles - 1
    s_scr[0] = logits_fn(0)

    def pair_body(consume):
        def body(pair, carry):
            a = 2 * pair
            s_scr[1] = logits_fn(a + 1)
            consume(a, 0)
            s_scr[0] = logits_fn(jnp.minimum(a + 2, last))
            consume(a + 1, 1)
            return carry
        return body

    far_pairs = n_far // 2
    lax.fori_loop(0, far_pairs, pair_body(consume_far_fn), 0)
    lax.fori_loop(far_pairs, n_tiles // 2, pair_body(consume_fn), 0)

    @pl.when(n_tiles % 2 == 1)
    def _():
        consume_fn(last, 0)


def _near_kind(j, qi):
    return jnp.maximum(j - (qi - 2), 0)


def _head_gates(gate_ref, group, branch):
    sig = jax.nn.sigmoid(gate_ref[...])
    lane = lax.broadcasted_iota(jnp.int32, (1, sig.shape[-1]), 1)
    cols = [jnp.sum(jnp.where(lane == (group * NSA_HPG + h) * 3 + branch, sig, 0.0), axis=-1, keepdims=True)
            for h in range(NSA_HPG)]
    return jnp.stack(cols)


def _da_kernel(lam_ref, q_ref, k_ref, v_ref, bias_ref, g_ref, o_ref, m_scr, acc_scr, s_scr, *, t, lam_init):
    qi = pl.program_id(2)
    q = q_ref[...]
    lane = lax.broadcasted_iota(jnp.int32, (1, LANES), 1)
    zero = jnp.zeros_like(q)
    q2 = jnp.concatenate([jnp.where(lane < DA_HEAD_DIM, q, zero), jnp.where(lane >= DA_HEAD_DIM, q, zero)], axis=0)
    _flash_init(m_scr, acc_scr)

    def logits(j):
        return _dot_nt(q2, k_ref[pl.ds(pl.multiple_of(j * t, t), t), :])

    def values(j):
        v = v_ref[pl.ds(pl.multiple_of(j * t, t), t), :]
        return jnp.concatenate([v, jnp.ones((t, LANES), BF16)], axis=-1)

    def consume_far(j, slot):
        _flash_update(s_scr[slot], values(j), m_scr, acc_scr)

    def consume(j, slot):
        s = (s_scr[slot].reshape(2, t, t) + bias_ref[_near_kind(j, qi)][None]).reshape(2 * t, t)
        _flash_update(s, values(j), m_scr, acc_scr)

    _pipelined_sweep(qi + 1, jnp.maximum(qi - 1, 0), logits, consume_far, consume, s_scr)

    acc = acc_scr[...]
    o = acc[:t, :DA_V_DIM] / acc[:t, DA_V_DIM:] - lam_ref[0] * (acc[t:, :DA_V_DIM] / acc[t:, DA_V_DIM:])
    ms = jnp.mean(o * o, axis=-1, keepdims=True)
    o = (o * lax.rsqrt(ms + LN_EPS)) * g_ref[...] * (1.0 - lam_init)
    o_ref[...] = o.astype(o_ref.dtype)


def _diff_attention(att, lam, bias, subln_g, batch, seq, lam_init):
    t = DA_TILE
    assert seq % t == 0 and t >= REL_MAX_DIST
    nq = seq // t
    kern = functools.partial(_da_kernel, t=t, lam_init=lam_init)
    return pl.pallas_call(
        kern,
        grid=(batch, DA_HEADS, nq),
        in_specs=[pl.BlockSpec(memory_space=pltpu.SMEM),
                  pl.BlockSpec((t, LANES), lambda b, h, i: (b * nq + i, h)),
                  pl.BlockSpec((seq, LANES), lambda b, h, i: (b, DA_HEADS + h)),
                  pl.BlockSpec((seq, LANES), lambda b, h, i: (b, 2 * DA_HEADS + h)),
                  pl.BlockSpec((3, None, t, t), lambda b, h, i: (0, h, 0, 0)),
                  pl.BlockSpec((1, DA_V_DIM), lambda b, h, i: (0, 0))],
        out_specs=pl.BlockSpec((t, DA_V_DIM), lambda b, h, i: (b * nq + i, h)),
        out_shape=jax.ShapeDtypeStruct((batch * seq, DA_HEADS * DA_V_DIM), BF16),
        scratch_shapes=[pltpu.VMEM((2 * t, LANES), F32), pltpu.VMEM((2 * t, 2 * DA_V_DIM), F32),
                        pltpu.VMEM((2, 2 * t, t), F32)],
        compiler_params=_cparams(3),
        name="diff_attention",
    )(lam, att, att, att, bias, subln_g)


def _compress_kernel(x_ref, pe_ref, w1_ref, w2_ref, o_ref):
    x = (x_ref[...].astype(F32) + pe_ref[...]).astype(BF16)
    hid = jax.nn.gelu(jnp.dot(x, w1_ref[...], preferred_element_type=F32))
    o_ref[...] = jnp.dot(hid.astype(BF16), w2_ref[...], preferred_element_type=F32).astype(o_ref.dtype)


def _compress(t, pe, w1, w2, batch, seq):
    g, d = NSA_GROUPS, NSA_DIM
    r = CMP_BLOCK // CMP_STRIDE
    nch = seq // CMP_STRIDE
    nc = nch - r + 1
    ch = t.reshape(batch, nch, CMP_STRIDE, g, d)
    blocks = jnp.concatenate([ch[:, j:j + nc] for j in range(r)], axis=2)
    flat = blocks.transpose(0, 1, 3, 2, 4).reshape(batch * nc * g, CMP_BLOCK * d)
    rows = flat.shape[0]
    tm = 512
    rows_p = -(-rows // tm) * tm
    flat = jnp.pad(flat, ((0, rows_p - rows), (0, 0)))
    hidden = w1.shape[1]
    out = pl.pallas_call(
        _compress_kernel,
        grid=(rows_p // tm,),
        in_specs=[pl.BlockSpec((tm, CMP_BLOCK * d), lambda i: (i, 0)),
                  pl.BlockSpec((1, CMP_BLOCK * d), lambda i: (0, 0)),
                  pl.BlockSpec((CMP_BLOCK * d, hidden), lambda i: (0, 0)),
                  pl.BlockSpec((hidden, d), lambda i: (0, 0))],
        out_specs=pl.BlockSpec((tm, d), lambda i: (i, 0)),
        out_shape=jax.ShapeDtypeStruct((rows_p, d), BF16),
        compiler_params=_cparams(1),
        name="compress_mlp",
    )(flat, pe.reshape(1, CMP_BLOCK * d), w1.astype(BF16), w2.astype(BF16))
    out = out[:rows].reshape(batch, nc, g, d).transpose(2, 0, 1, 3)
    return jnp.pad(out, ((0, 0), (0, 0), (0, nch - nc), (0, 0)))


def _cmp_select_kernel(q_ref, kc_ref, vc_ref, ov_ref, gate_ref, o_ref, sel_ref, *, tq, k_sel):
    qi = pl.program_id(2)
    ncp = kc_ref.shape[0]
    nsp = sel_ref.shape[-1]
    q = q_ref[...].reshape(NSA_HPG * tq, NSA_DIM)
    q_pos = qi * tq + lax.broadcasted_iota(jnp.int32, (tq, 1), 0)
    c_end = lax.broadcasted_iota(jnp.int32, (1, ncp), 1) * CMP_STRIDE + (CMP_BLOCK - 1)
    cmask = (c_end <= q_pos)[None]
    s = _dot_nt(q, kc_ref[...]).reshape(NSA_HPG, tq, ncp)
    s = jnp.where(cmask, s, NEG)
    m = jnp.max(s, axis=-1, keepdims=True)
    p = jnp.where(cmask, jnp.exp(s - m), 0.0)
    p = p / jnp.maximum(jnp.sum(p, axis=-1, keepdims=True), 1e-30)
    o = jnp.dot(p.reshape(NSA_HPG * tq, ncp).astype(BF16), vc_ref[...], preferred_element_type=F32)
    o_ref[...] = o.reshape(NSA_HPG, tq, NSA_DIM) * _head_gates(gate_ref, pl.program_id(0), 0)

    psum = p[0]
    for h in range(1, NSA_HPG):
        psum = psum + p[h]
    imp = _dot_exact01(psum, ov_ref[...])
    blk = lax.broadcasted_iota(jnp.int32, (1, nsp), 1)
    cur = q_pos // SLC_BLOCK
    valid = blk <= cur
    forced = valid & ((blk == 0) | (blk > cur - N_LOCAL_BLOCKS))
    score = jnp.where(forced, FORCE_SCORE, jnp.where(valid, imp, NEG))
    blk_f = blk.astype(F32)
    sel = jnp.zeros((tq, nsp), F32)
    for _ in range(k_sel):
        mx = jnp.max(score, axis=-1, keepdims=True)
        idx = jnp.min(jnp.where(score == mx, blk_f, float(nsp)), axis=-1, keepdims=True)
        hit = blk_f == idx
        sel = jnp.where(hit, 1.0, sel)
        score = jnp.where(hit, -jnp.inf, score)
    sel_ref[...] = sel.astype(sel_ref.dtype)


def _cmp_select(q, kc, vc, overlap, gate, batch, seq, k_sel):
    tq = CMP_TQ
    assert seq % tq == 0
    nq = seq // tq
    ncp = kc.shape[2]
    nsp = overlap.shape[1]
    kern = functools.partial(_cmp_select_kernel, tq=tq, k_sel=k_sel)
    return pl.pallas_call(
        kern,
        grid=(NSA_GROUPS, batch, nq),
        in_specs=[pl.BlockSpec((NSA_HPG, tq, NSA_DIM), lambda g, b, i: (g, b * nq + i, 0)),
                  pl.BlockSpec((None, None, ncp, NSA_DIM), lambda g, b, i: (g, b, 0, 0)),
                  pl.BlockSpec((None, None, ncp, NSA_DIM), lambda g, b, i: (g, b, 0, 0)),
                  pl.BlockSpec((ncp, nsp), lambda g, b, i: (0, 0)),
                  pl.BlockSpec((tq, gate.shape[1]), lambda g, b, i: (b * nq + i, 0))],
        out_specs=[pl.BlockSpec((NSA_HPG, tq, NSA_DIM), lambda g, b, i: (g, b * nq + i, 0)),
                   pl.BlockSpec((None, tq, nsp), lambda g, b, i: (g, b * nq + i, 0))],
        out_shape=[jax.ShapeDtypeStruct((NSA_HEADS, batch * seq, NSA_DIM), F32),
                   jax.ShapeDtypeStruct((NSA_GROUPS, batch * seq, nsp), BF16)],
        compiler_params=_cparams(3),
        name="nsa_cmp_select",
    )(q, kc, vc, overlap, gate)


def _window_kernel(q_ref, k_ref, v_ref, bias_ref, gate_ref, o_ref, *, tq, span):
    qi = pl.program_id(2)
    q0 = pl.multiple_of(qi * tq, tq)
    q = q_ref[...].reshape(NSA_HPG * tq, NSA_DIM)
    ks = k_ref[pl.ds(q0, span), :]
    v1 = v_ref[pl.ds(q0, span), :]
    s = _dot_nt(q, ks).reshape(NSA_HPG, tq, span) + bias_ref[...]
    in_seq = (lax.broadcasted_iota(jnp.int32, (1, 1, span), 2) + q0) >= WINDOW
    s = jnp.where(in_seq, s, NEG)
    m = jnp.max(s, axis=-1, keepdims=True)
    p = jnp.exp(s - m).reshape(NSA_HPG * tq, span).astype(BF16)
    o = jnp.dot(p, v1, preferred_element_type=F32)
    o = (o[:, :NSA_DIM] / o[:, NSA_DIM:]).reshape(NSA_HPG, tq, NSA_DIM)
    o_ref[...] = o * _head_gates(gate_ref, pl.program_id(0), 2)


def _window_attention(q, kp, vp, bias, gate, batch, seq):
    tq = WIN_TQ
    assert seq % tq == 0
    span = WINDOW + tq
    nq = seq // tq
    kern = functools.partial(_window_kernel, tq=tq, span=span)
    return pl.pallas_call(
        kern,
        grid=(NSA_GROUPS, batch, nq),
        in_specs=[pl.BlockSpec((NSA_HPG, tq, NSA_DIM), lambda g, b, i: (g, b * nq + i, 0)),
                  pl.BlockSpec((None, None, WINDOW + seq, NSA_DIM), lambda g, b, i: (g, b, 0, 0)),
                  pl.BlockSpec((None, None, WINDOW + seq, 2 * NSA_DIM), lambda g, b, i: (g, b, 0, 0)),
                  pl.BlockSpec((NSA_HPG, tq, span), lambda g, b, i: (g, 0, 0)),
                  pl.BlockSpec((tq, gate.shape[1]), lambda g, b, i: (b * nq + i, 0))],
        out_specs=pl.BlockSpec((NSA_HPG, tq, NSA_DIM), lambda g, b, i: (g, b * nq + i, 0)),
        out_shape=jax.ShapeDtypeStruct((NSA_HEADS, batch * seq, NSA_DIM), F32),
        compiler_params=_cparams(3),
        name="nsa_window",
    )(q, kp, vp, bias, gate)


def _slc_kernel(q_ref, k_ref, v_ref, sel_ref, bias_ref, gate_ref, oc_ref, ow_ref, o_ref,
                m_scr, acc_scr, s_scr, *, t):
    g = pl.program_id(0)
    qi = pl.program_id(2)
    nsp = sel_ref.shape[-1]
    rows = NSA_HPG * t
    q = q_ref[...].reshape(rows, NSA_DIM)
    sel = sel_ref[...]
    _flash_init(m_scr, acc_scr)
    blk_row = lax.broadcasted_iota(jnp.int32, (nsp, t), 0)
    key_blk = lax.broadcasted_iota(jnp.int32, (nsp, t), 1) // SLC_BLOCK

    def logits(j):
        return _dot_nt(q, k_ref[pl.ds(pl.multiple_of(j * t, t), t), :])

    def selection_mask(j):
        expand = jnp.where(blk_row == key_blk + j * (t // SLC_BLOCK), 1.0, 0.0).astype(BF16)
        return (jnp.dot(sel, expand, preferred_element_type=F32) - 1.0) * (-NEG)

    def values(j):
        return v_ref[pl.ds(pl.multiple_of(j * t, t), t), :]

    def consume_far(j, slot):
        s = s_scr[slot].reshape(NSA_HPG, t, t) + selection_mask(j)[None]
        _flash_update(s.reshape(rows, t), values(j), m_scr, acc_scr)

    def consume(j, slot):
        s = s_scr[slot].reshape(NSA_HPG, t, t) + (selection_mask(j)[None] + bias_ref[_near_kind(j, qi)])
        _flash_update(s.reshape(rows, t), values(j), m_scr, acc_scr)

    _pipelined_sweep(qi + 1, jnp.maximum(qi - 1, 0), logits, consume_far, consume, s_scr)

    acc = acc_scr[...]
    o = (acc[:, :NSA_DIM] / acc[:, NSA_DIM:]).reshape(NSA_HPG, t, NSA_DIM)
    o = oc_ref[...] + ow_ref[...] + o * _head_gates(gate_ref, g, 1)
    o_ref[...] = o.astype(o_ref.dtype)


def _slc_attention(q, ks, vs1, sel, bias, gate, o_cmp, o_win, batch, seq):
    t = SLC_TILE
    assert seq % t == 0 and t >= REL_MAX_DIST and t % SLC_BLOCK == 0
    nq = seq // t
    nsp = sel.shape[-1]
    kern = functools.partial(_slc_kernel, t=t)
    head_blk = lambda last: pl.BlockSpec((NSA_HPG, t, last), lambda g, b, i: (g, b * nq + i, 0))
    return pl.pallas_call(
        kern,
        grid=(NSA_GROUPS, batch, nq),
        in_specs=[head_blk(NSA_DIM),
                  pl.BlockSpec((None, None, seq, NSA_DIM), lambda g, b, i: (g, b, 0, 0)),
                  pl.BlockSpec((None, None, seq, 2 * NSA_DIM), lambda g, b, i: (g, b, 0, 0)),
                  pl.BlockSpec((None, t, nsp), lambda g, b, i: (g, b * nq + i, 0)),
                  pl.BlockSpec((3, NSA_HPG, t, t), lambda g, b, i: (0, g, 0, 0), pipeline_mode=pl.Buffered(1)),
                  pl.BlockSpec((t, gate.shape[1]), lambda g, b, i: (b * nq + i, 0)),
                  head_blk(NSA_DIM), head_blk(NSA_DIM)],
        out_specs=head_blk(NSA_DIM),
        out_shape=jax.ShapeDtypeStruct((NSA_HEADS, batch * seq, NSA_DIM), BF16),
        scratch_shapes=[pltpu.VMEM((NSA_HPG * t, LANES), F32), pltpu.VMEM((NSA_HPG * t, 2 * NSA_DIM), F32),
                        pltpu.VMEM((2, NSA_HPG * t, t), F32)],
        compiler_params=_cparams(3),
        name="nsa_selected",
    )(q, ks, vs1, sel, bias, gate, o_cmp, o_win)


def _merge_kernel(x_ref, oda_ref, onsa_ref, wgda_ref, wgnsa_ref, wda_ref, wnsa_ref, o_ref):
    x = x_ref[...]
    gate_da = jnp.dot(x, wgda_ref[...], preferred_element_type=F32)
    gate_nsa = jnp.dot(x, wgnsa_ref[...], preferred_element_type=F32)
    a = jnp.dot(oda_ref[...], wda_ref[...], preferred_element_type=F32)
    n = jnp.dot(onsa_ref[...], wnsa_ref[...], preferred_element_type=F32)
    mixed = jax.nn.sigmoid(gate_da) * a + jax.nn.sigmoid(gate_nsa) * n
    o_ref[...] = mixed.astype(o_ref.dtype)


def _merge(xb, o_da, o_nsa, w_gates, w_da, w_nsa):
    tokens, kd = o_da.shape
    d = w_da.shape[1]
    tm, tn = 512, 512
    nb = d // tn
    return pl.pallas_call(
        _merge_kernel,
        grid=(tokens // tm, nb),
        in_specs=[pl.BlockSpec((tm, d), lambda i, j: (i, 0)),
                  pl.BlockSpec((tm, kd), lambda i, j: (i, 0)),
                  pl.BlockSpec((tm, kd), lambda i, j: (i, 0)),
                  pl.BlockSpec((d, tn), lambda i, j: (0, j)),
                  pl.BlockSpec((d, tn), lambda i, j: (0, nb + j)),
                  pl.BlockSpec((kd, tn), lambda i, j: (0, j)),
                  pl.BlockSpec((kd, tn), lambda i, j: (0, j))],
        out_specs=pl.BlockSpec((tm, tn), lambda i, j: (i, j)),
        out_shape=jax.ShapeDtypeStruct((tokens, d), BF16),
        compiler_params=_cparams(2),
        name="gated_merge",
    )(xb, o_da, o_nsa, w_gates, w_gates, w_da, w_nsa)


def _layer_norm(z, g, b):
    mu = jnp.mean(z, axis=-1, keepdims=True)
    zc = z - mu
    var = jnp.mean(zc * zc, axis=-1, keepdims=True)
    return (zc * lax.rsqrt(var + LN_EPS)) * g + b


def _outproj_ln_kernel(mixed_ref, w_ref, x_ref, g_ref, b_ref, wq_ref, h_ref, pq_ref):
    z = DN_ALPHA * x_ref[...] + jnp.dot(mixed_ref[...], w_ref[...], preferred_element_type=F32)
    h = _layer_norm(z, g_ref[...], b_ref[...])
    h_ref[...] = h
    pq_ref[...] = jnp.dot(h.astype(BF16), wq_ref[...], preferred_element_type=F32).astype(pq_ref.dtype)


def _outproj_ln(mixed, w_out, x, g, b, w_q):
    tokens, d = x.shape
    nq = w_q.shape[1]
    tm = 256
    row = pl.BlockSpec((tm, d), lambda i: (i, 0))
    vec = pl.BlockSpec((1, d), lambda i: (0, 0))
    resident = lambda shape: pl.BlockSpec(shape, lambda i: (0, 0), pipeline_mode=pl.Buffered(1))
    return pl.pallas_call(
        _outproj_ln_kernel,
        grid=(tokens // tm,),
        in_specs=[row, resident((d, d)), row, vec, vec, resident((d, nq))],
        out_specs=[row, pl.BlockSpec((tm, nq), lambda i: (i, 0))],
        out_shape=[jax.ShapeDtypeStruct((tokens, d), F32), jax.ShapeDtypeStruct((tokens, nq), BF16)],
        compiler_params=_cparams(1),
        name="outproj_ln1",
    )(mixed, w_out, x, g, b, w_q)


def _topk_rows(x, k, ids=None):
    n, t = x.shape
    if ids is None:
        ids = lax.broadcasted_iota(jnp.int32, (n, t), 0).astype(F32)
    slot = lax.broadcasted_iota(jnp.int32, (k, t), 0)
    vals = jnp.zeros((k, t), F32)
    idxs = jnp.zeros((k, t), F32)
    for r in range(k):
        mx = jnp.max(x, axis=0, keepdims=True)
        idx = jnp.min(jnp.where(x == mx, ids, jnp.inf), axis=0, keepdims=True)
        vals = jnp.where(slot == r, mx, vals)
        idxs = jnp.where(slot == r, idx, idxs)
        x = jnp.where(ids == idx, -jnp.inf, x)
    return vals, idxs


def _pair_candidates(v1, i1, v2, i2, nk):
    k, t = v1.shape
    sums, ids, eids = [], [], []
    for c in range(k):
        top = k // (c + 1) - 1
        for fixed_second in (True, False):
            lo = c if fixed_second else c + 1
            if lo > top:
                continue
            n = 8 if top < 8 else k
            run = lax.broadcasted_iota(jnp.int32, (n, t), 0)
            keep = (run >= lo) & (run <= top)
            runf = run.astype(F32)
            if fixed_second:
                val, pid = v1[:n] + v2[c:c + 1], runf * float(k) + float(c)
                eid = i1[:n] * float(nk) + i2[c:c + 1]
            else:
                val, pid = v1[c:c + 1] + v2[:n], float(c * k) + runf
                eid = i1[c:c + 1] * float(nk) + i2[:n]
            sums.append(jnp.where(keep, val, -jnp.inf))
            ids.append(pid)
            eids.append(eid)
    return jnp.concatenate(sums, axis=0), jnp.concatenate(ids, axis=0), jnp.concatenate(eids, axis=0)


def _peer_topk_kernel(q_ref, sk_ref, gate_ref, eid_ref, *, tm):
    k = PEER_TOPK
    nk = PEER_NKEYS
    for h in range(PEER_HEADS):
        qh = q_ref[:, h * LANES:(h + 1) * LANES]
        st = _dot_nt(sk_ref[...], qh)
        v1, i1 = _topk_rows(st[:nk], k)
        v2, i2 = _topk_rows(st[nk:], k)
        cand, ids, cid = _pair_candidates(v1, i1, v2, i2, nk)
        sc, j = _topk_rows(cand, k, ids)
        slot = lax.broadcasted_iota(jnp.int32, (k, tm), 0)
        eid = jnp.zeros((k, tm), F32)
        for r in range(k):
            picked = jnp.max(jnp.where(ids == j[r:r + 1], cid, -1.0), axis=0, keepdims=True)
            eid = jnp.where(slot == r, picked, eid)
        e = jnp.exp(sc - sc[0:1])
        gate_ref[h] = e / jnp.sum(e, axis=0, keepdims=True)
        eid_ref[h] = eid.astype(jnp.int32)


def _peer_topk(q, sk):
    tokens = q.shape[0]
    tm = 256
    kern = functools.partial(_peer_topk_kernel, tm=tm)
    out_blk = pl.BlockSpec((PEER_HEADS, PEER_TOPK, tm), lambda i: (0, 0, i))
    return pl.pallas_call(
        kern,
        grid=(tokens // tm,),
        in_specs=[pl.BlockSpec((tm, PEER_HEADS * LANES), lambda i: (i, 0)),
                  pl.BlockSpec((2 * PEER_NKEYS, LANES), lambda i: (0, 0))],
        out_specs=[out_blk, out_blk],
        out_shape=[jax.ShapeDtypeStruct((PEER_HEADS, PEER_TOPK, tokens), F32),
                   jax.ShapeDtypeStruct((PEER_HEADS, PEER_TOPK, tokens), jnp.int32)],
        compiler_params=_cparams(1),
        name="peer_topk",
    )(q, sk)


def _pack_bf16_pairs(t):
    bits = lax.bitcast_convert_type(t.astype(BF16), jnp.uint16).astype(jnp.uint32)
    half = t.shape[1] // 2
    return bits[:, :half] | (bits[:, half:] << 16)


def _unpack_bf16_pairs(w):
    return pltpu.bitcast(w << 16, F32), pltpu.bitcast(w & jnp.uint32(0xFFFF0000), F32)


def _peer_gather_kernel(eid_hbm, uv_hbm, gate_ref, h_ref, g_ref, b_ref, o_ref, idx_smem, acc, idx_sem, row_sem,
                        *rows, tt, d, tokens):
    i = pl.program_id(0)
    nsteps = pl.num_programs(0)
    nsel = PEER_HEADS * PEER_TOPK
    ns = len(rows)
    per = tt * nsel
    half = d // 2
    nchunk = half // LANES

    def idx_copy(step):
        dst = idx_smem.at[pl.ds(pl.multiple_of((step % 2) * per, per), per)]
        return pltpu.make_async_copy(eid_hbm.at[step], dst, idx_sem.at[step % 2])

    def issue(gtok, slot, first=0, count=None):
        base = (jnp.minimum(gtok, tokens - 1) % (2 * tt)) * nsel
        for j in range(first, first + (nsel if count is None else count)):
            e = idx_smem[base + j]
            pltpu.make_async_copy(uv_hbm.at[e], rows[slot].at[:, j, :], row_sem.at[slot]).start(priority=j % PEER_DMA_QUEUES)

    def wait_slot(slot):
        pltpu.make_async_copy(rows[slot], rows[slot], row_sem.at[slot]).wait()

    @pl.when(i == 0)
    def _():
        first = idx_copy(0)
        first.start()
        first.wait()
        for s in range(ns - 1):
            issue(s, s)

    @pl.when(i + 1 < nsteps)
    def _():
        idx_copy(i + 1).start()

    lane = lax.broadcasted_iota(jnp.int32, (nsel, tt), 1)

    per_chunk = nsel // (2 * nchunk)

    def compute_and_issue(tok, slot, ahead, ahead_slot):
        buf = rows[slot]
        y = h_ref[pl.ds(tok, 1), :]
        part = None
        for c in range(nchunk):
            issue(ahead, ahead_slot, c * per_chunk, per_chunk)
            lo, hi = _unpack_bf16_pairs(buf[c])
            term = lo * y[:, c * LANES:(c + 1) * LANES] + hi * y[:, half + c * LANES:half + (c + 1) * LANES]
            part = term if part is None else part + term
        act = jax.nn.gelu(jnp.sum(part, axis=-1, keepdims=True))
        gcol = jnp.sum(jnp.where(lane == tok, gate_ref[...], 0.0), axis=-1, keepdims=True)
        w = gcol * act
        out_lo, out_hi = [], []
        for c in range(nchunk):
            issue(ahead, ahead_slot, (nchunk + c) * per_chunk, per_chunk)
            lo, hi = _unpack_bf16_pairs(buf[nchunk + c])
            out_lo.append(jnp.sum(w * lo, axis=0, keepdims=True))
            out_hi.append(jnp.sum(w * hi, axis=0, keepdims=True))
        acc[pl.ds(tok, 1), :] = jnp.concatenate(out_lo + out_hi, axis=-1)

    n_groups = tt // ns

    def body(grp, carry):
        @pl.when((grp == n_groups - 1) & (i + 1 < nsteps))
        def _():
            idx_copy(i + 1).wait()

        for s in range(ns):
            tok = grp * ns + s
            wait_slot(s)
            compute_and_issue(tok, s, i * tt + tok + ns - 1, (s + ns - 1) % ns)
        return carry

    lax.fori_loop(0, n_groups, body, 0)

    @pl.when(i == nsteps - 1)
    def _():
        for s in range(ns - 1):
            wait_slot((tt + s) % ns)

    z = DN_ALPHA * h_ref[...] + acc[...]
    o_ref[...] = _layer_norm(z, g_ref[...], b_ref[...])


def _peer_gather(eid, uv, gate, h, g, b):
    tokens, d = h.shape
    tt = PEER_TT
    nsel = PEER_HEADS * PEER_TOPK
    assert tokens % tt == 0 and tt % PEER_SLOTS == 0 and nsel % (d // LANES) == 0
    assert uv.shape[1:] == (d // LANES, LANES) and eid.shape == (tokens // tt, tt * nsel)
    kern = functools.partial(_peer_gather_kernel, tt=tt, d=d, tokens=tokens)
    vec = pl.BlockSpec((1, d), lambda i: (0, 0))
    return pl.pallas_call(
        kern,
        grid=(tokens // tt,),
        in_specs=[pl.BlockSpec(memory_space=pl.ANY),
                  pl.BlockSpec(memory_space=pl.ANY),
                  pl.BlockSpec((nsel, tt), lambda i: (0, i)),
                  pl.BlockSpec((tt, d), lambda i: (i, 0)),
                  vec, vec],
        out_specs=pl.BlockSpec((tt, d), lambda i: (i, 0)),
        out_shape=jax.ShapeDtypeStruct((tokens, d), F32),
        scratch_shapes=[pltpu.SMEM((2 * tt * nsel,), jnp.int32),
                        pltpu.VMEM((tt, d), F32),
                        pltpu.SemaphoreType.DMA((2,)),
                        pltpu.SemaphoreType.DMA((PEER_SLOTS,))]
                       + [pltpu.VMEM((d // LANES, nsel, LANES), jnp.uint32) for _ in range(PEER_SLOTS)],
        compiler_params=_cparams(1),
        name="peer_gather_ln2",
    )(eid, uv, gate, h, g, b)


def _heads_major(t, heads, width):
    return t.reshape(t.shape[0], heads, width).transpose(1, 0, 2)


def kernel(x, w_in, da_lam_q, da_lam_k, da_subln_g, cmp_pe_k, cmp_w1_k, cmp_w2_k, cmp_pe_v, cmp_w1_v,
           cmp_w2_v, w_branch_da, w_branch_nsa, w_out, ln1_g, ln1_b, peer_wq, peer_subkey1, peer_subkey2,
           peer_u, peer_v, ln2_g, ln2_b, rel_bias):
    batch, seq, d_model = x.shape
    tokens = batch * seq
    g, hd = NSA_GROUPS, NSA_DIM
    table = rel_bias.astype(F32)
    xs = x.reshape(tokens, d_model)
    for l in range(DEPTH):
        lam_init = 0.8 - 0.6 * math.exp(-0.3 * l)
        xb = xs.astype(BF16)
        w = w_in[l]
        scale = DA_HEAD_DIM ** -0.5
        c_daq, c_dak, c_dav, c_nq = 0, 1024, 2048, 3072
        c_kv, c_gate, c_mg, c_end = 4096, 5632, 5680, 9776
        w_att = jnp.concatenate([w[:, c_daq:c_dak] * scale, w[:, c_dak:c_nq], w[:, c_nq:c_kv] * scale],
                                axis=1).astype(BF16)
        att = _matmul(xb, w_att, BF16, 1024, 512)
        kv = _matmul(xb, w[:, c_kv:c_gate].astype(BF16), F32, 512, 512)
        w_gate = jnp.pad(w[:, c_gate:c_mg], ((0, 0), (0, LANES - (c_mg - c_gate)))).astype(BF16)
        br_gate = _matmul(xb, w_gate, F32, 512, LANES)[:, :c_mg - c_gate]

        lam_e = jnp.exp(jnp.sum(da_lam_q[l].astype(F32) * da_lam_k[l].astype(F32), -1))
        lam = (lam_e[0] - lam_e[1] + lam_init).reshape(1)
        table_rel = table - table[REL_BUCKETS - 1]
        o_da = _diff_attention(att, lam, _causal_bias_tiles(table_rel[:, :DA_HEADS], DA_TILE),
                               da_subln_g[l].reshape(1, DA_V_DIM), batch, seq, lam_init)

        q_n = _heads_major(att[:, 3072:4096], NSA_HEADS, hd)
        kv6 = kv.reshape(tokens, 6, g * hd)
        kc = _compress(kv6[:, 0], cmp_pe_k[l], cmp_w1_k[l], cmp_w2_k[l], batch, seq)
        vc = _compress(kv6[:, 1], cmp_pe_v[l], cmp_w1_v[l], cmp_w2_v[l], batch, seq)
        grp = lambda t: t.astype(BF16).reshape(batch, seq, g, hd).transpose(2, 0, 1, 3)
        k_s, v_s, k_w, v_w = grp(kv6[:, 2]), grp(kv6[:, 3]), grp(kv6[:, 4]), grp(kv6[:, 5])

        ncp = seq // CMP_STRIDE
        ns = seq // SLC_BLOCK
        nsp = -(-ns // LANES) * LANES
        k_sel = min(SLC_TOPK, ns)
        cmp_start = jnp.arange(ncp) * CMP_STRIDE
        slc_start = jnp.arange(nsp) * SLC_BLOCK
        overlap = ((cmp_start[:, None] <= slc_start[None, :] + SLC_BLOCK - 1)
                   & (cmp_start[:, None] + CMP_BLOCK - 1 >= slc_start[None, :])).astype(BF16)
        o_cmp, sel = _cmp_select(q_n, kc, vc, overlap, br_gate, batch, seq, k_sel)

        win_bias = _window_bias_tile(table[:, DA_HEADS:], WIN_TQ, WINDOW + WIN_TQ)
        pad_w = lambda t: jnp.pad(t, ((0, 0), (0, 0), (WINDOW, 0), (0, 0)))
        with_ones = lambda t: jnp.concatenate([t, jnp.ones_like(t)], axis=-1)
        o_win = _window_attention(q_n, pad_w(k_w), pad_w(with_ones(v_w)), win_bias, br_gate, batch, seq)

        v_s1 = jnp.concatenate([v_s, jnp.ones_like(v_s)], axis=-1)
        o_nsa = _slc_attention(q_n, k_s, v_s1, sel, _causal_bias_tiles(table_rel[:, DA_HEADS:], SLC_TILE),
                               br_gate, o_cmp, o_win, batch, seq)
        o_nsa = o_nsa.transpose(1, 0, 2).reshape(tokens, NSA_HEADS * hd)

        mixed = _merge(xb, o_da, o_nsa, w[:, c_mg:c_end].astype(BF16),
                       w_branch_da[l].astype(BF16), w_branch_nsa[l].astype(BF16))
        h, pq = _outproj_ln(mixed, w_out[l].astype(BF16), xs, ln1_g[l].reshape(1, -1), ln1_b[l].reshape(1, -1),
                            peer_wq[l].astype(BF16))

        half = peer_subkey1.shape[-1]
        zeros = jnp.zeros((PEER_NKEYS, half), F32)
        sk = jnp.concatenate([jnp.concatenate([peer_subkey1[l], zeros], axis=1),
                              jnp.concatenate([zeros, peer_subkey2[l]], axis=1)], axis=0).astype(BF16)
        gate, eid = _peer_topk(pq, sk)
        nsel = PEER_HEADS * PEER_TOPK
        eid_tok = eid.reshape(nsel, tokens).T.reshape(tokens // PEER_TT, PEER_TT * nsel)
        uv = jnp.concatenate([_pack_bf16_pairs(peer_u[l]), _pack_bf16_pairs(peer_v[l])], axis=1)
        uv = uv.reshape(uv.shape[0], d_model // LANES, LANES)
        xs = _peer_gather(eid_tok, uv, gate.reshape(nsel, tokens), h,
                          ln2_g[l].reshape(1, -1), ln2_b[l].reshape(1, -1))
    return xs.reshape(batch, seq, d_model)
```

```python
import functools
import math

import jax
import jax.numpy as jnp
from jax import lax
from jax.experimental import pallas as pl
from jax.experimental.pallas import tpu as pltpu

F32 = jnp.float32
BF16 = jnp.bfloat16

DA_HEADS = 8
DA_HEAD_DIM = 64
DA_V_DIM = 128
NSA_HEADS = 16
NSA_GROUPS = 4
NSA_HPG = 4
NSA_DIM = 64
CMP_BLOCK = 32
CMP_STRIDE = 16
SLC_BLOCK = 64
SLC_TOPK = 16
N_LOCAL_BLOCKS = 2
WINDOW = 512
REL_BUCKETS = 32
REL_MAX_DIST = 128
PEER_HEADS = 8
PEER_NKEYS = 128
PEER_TOPK = 16
DEPTH = 1
DN_ALPHA = (2 * DEPTH) ** 0.25
LN_EPS = 1e-5
NEG = -1e30
FORCE_SCORE = 1e4

LANES = 128
VMEM_LIMIT = 48 * 1024 * 1024
DA_TILE = 512
SLC_TILE = 512
WIN_TQ = 256
CMP_TQ = 512
PEER_TT = 128
PEER_SLOTS = 8
PEER_DMA_QUEUES = 2


def _cparams(n_axes):
    return pltpu.CompilerParams(dimension_semantics=("arbitrary",) * n_axes,
                                vmem_limit_bytes=VMEM_LIMIT)


def _dot_nt(a, b):
    return lax.dot_general(a, b, (((1,), (1,)), ((), ())), preferred_element_type=F32)


def _dot_exact01(x, onehot_bf16):
    hi = x.astype(BF16)
    r1 = x - hi.astype(F32)
    mid = r1.astype(BF16)
    lo = (r1 - mid.astype(F32)).astype(BF16)
    d = lambda a: jnp.dot(a, onehot_bf16, preferred_element_type=F32)
    return d(hi) + d(mid) + d(lo)


def _mm_kernel(a_ref, b_ref, o_ref):
    o_ref[...] = jnp.dot(a_ref[...], b_ref[...], preferred_element_type=F32).astype(o_ref.dtype)


def _matmul(a, b, out_dtype, tm, tn):
    m, k = a.shape
    n = b.shape[1]
    return pl.pallas_call(
        _mm_kernel,
        grid=(n // tn, m // tm),
        in_specs=[pl.BlockSpec((tm, k), lambda j, i: (i, 0)),
                  pl.BlockSpec((k, tn), lambda j, i: (0, j))],
        out_specs=pl.BlockSpec((tm, tn), lambda j, i: (i, j)),
        out_shape=jax.ShapeDtypeStruct((m, n), out_dtype),
        compiler_params=_cparams(2),
        name="matmul",
    )(a, b)


def _rel_bucket(dist):
    n = jnp.maximum(dist, 0)
    max_exact = REL_BUCKETS // 2
    nf = jnp.maximum(n, 1).astype(F32)
    large = max_exact + (jnp.log(nf / max_exact) / math.log(REL_MAX_DIST / max_exact)
                         * (REL_BUCKETS - max_exact)).astype(jnp.int32)
    large = jnp.minimum(large, REL_BUCKETS - 1)
    return jnp.where(n < max_exact, n, large)


def _toeplitz(rd, n_i, n_j):
    h, length = rd.shape
    rev = jnp.pad(rd[:, ::-1], ((0, 0), (0, 1)))
    flat = jnp.broadcast_to(rev[:, None, :], (h, n_i, length + 1)).reshape(h, n_i * (length + 1))
    skew = flat[:, :n_i * length].reshape(h, n_i, length)
    return skew[:, :, n_i - 1:n_i - 1 + n_j]


def _bias_by_distance(table, n):
    return table[_rel_bucket(jnp.arange(n))].T


def _causal_bias_tiles(table, t):
    bd = _bias_by_distance(table, 2 * t)
    neg = jnp.full((bd.shape[0], t - 1), NEG, F32)
    diag = _toeplitz(jnp.concatenate([neg, bd[:, :t]], axis=1), t, t)
    off = _toeplitz(bd[:, 1:], t, t)
    return jnp.stack([jnp.zeros_like(off), off, diag], axis=0)


def _window_bias_tile(table, tq, span):
    bd = _bias_by_distance(table, WINDOW)
    h = bd.shape[0]
    lo = jnp.full((h, span - 1 - WINDOW), NEG, F32)
    hi = jnp.full((h, tq), NEG, F32)
    return _toeplitz(jnp.concatenate([lo, bd, hi], axis=1), tq, span)


def _flash_init(m_scr, acc_scr):
    m_scr[...] = jnp.full(m_scr.shape, NEG, F32)
    acc_scr[...] = jnp.zeros(acc_scr.shape, F32)


def _flash_update(s, v1, m_scr, acc_scr, exp_dtype=F32):
    m_old = m_scr[...]
    m_new = jnp.maximum(m_old, jnp.broadcast_to(jnp.max(s, axis=-1, keepdims=True), m_old.shape))
    alpha = jnp.exp(m_old - m_new)
    widen = lambda x, width: jnp.concatenate([x] * (width // LANES), axis=1)
    p = jnp.exp((s - widen(m_new, s.shape[1])).astype(exp_dtype)).astype(BF16)
    acc = acc_scr[...]
    acc_scr[...] = widen(alpha, acc.shape[1]) * acc + jnp.dot(p, v1, preferred_element_type=F32)
    m_scr[...] = m_new


def _pipelined_sweep(n_tiles, n_far, logits_fn, consume_far_fn, consume_fn, s_scr):
    last = n_tiles - 1
    s_scr[0] = logits_fn(0)

    def pair_body(consume):
        def body(pair, carry):
            a = 2 * pair
            s_scr[1] = logits_fn(a + 1)
            consume(a, 0)
            s_scr[0] = logits_fn(jnp.minimum(a + 2, last))
            consume(a + 1, 1)
            return carry
        return body

    far_pairs = n_far // 2
    lax.fori_loop(0, far_pairs, pair_body(consume_far_fn), 0)
    lax.fori_loop(far_pairs, n_tiles // 2, pair_body(consume_fn), 0)

    @pl.when(n_tiles % 2 == 1)
    def _():
        consume_fn(last, 0)


def _near_kind(j, qi):
    return jnp.maximum(j - (qi - 2), 0)


def _head_gates(gate_ref, group, branch):
    sig = jax.nn.sigmoid(gate_ref[...])
    lane = lax.broadcasted_iota(jnp.int32, (1, sig.shape[-1]), 1)
    cols = [jnp.sum(jnp.where(lane == (group * NSA_HPG + h) * 3 + branch, sig, 0.0), axis=-1, keepdims=True)
            for h in range(NSA_HPG)]
    return jnp.stack(cols)


def _da_kernel(lam_ref, q_ref, k_ref, v_ref, bias_ref, g_ref, o_ref, m_scr, acc_scr, s_scr, *, t, lam_init):
    qi = pl.program_id(2)
    q = q_ref[...]
    lane = lax.broadcasted_iota(jnp.int32, (1, LANES), 1)
    zero = jnp.zeros_like(q)
    q2 = jnp.concatenate([jnp.where(lane < DA_HEAD_DIM, q, zero), jnp.where(lane >= DA_HEAD_DIM, q, zero)], axis=0)
    _flash_init(m_scr, acc_scr)

    def logits(j):
        return _dot_nt(q2, k_ref[pl.ds(pl.multiple_of(j * t, t), t), :])

    def values(j):
        v = v_ref[pl.ds(pl.multiple_of(j * t, t), t), :]
        return jnp.concatenate([v, jnp.ones((t, LANES), BF16)], axis=-1)

    def consume_far(j, slot):
        _flash_update(s_scr[slot], values(j), m_scr, acc_scr)

    def consume(j, slot):
        s = (s_scr[slot].reshape(2, t, t) + bias_ref[_near_kind(j, qi)][None]).reshape(2 * t, t)
        _flash_update(s, values(j), m_scr, acc_scr)

    _pipelined_sweep(qi + 1, jnp.maximum(qi - 1, 0), logits, consume_far, consume, s_scr)

    acc = acc_scr[...]
    o = acc[:t, :DA_V_DIM] / acc[:t, DA_V_DIM:] - lam_ref[0] * (acc[t:, :DA_V_DIM] / acc[t:, DA_V_DIM:])
    ms = jnp.mean(o * o, axis=-1, keepdims=True)
    o = (o * lax.rsqrt(ms + LN_EPS)) * g_ref[...] * (1.0 - lam_init)
    o_ref[...] = o.astype(o_ref.dtype)


def _diff_attention(att, lam, bias, subln_g, batch, seq, lam_init):
    t = DA_TILE
    assert seq % t == 0 and t >= REL_MAX_DIST
    nq = seq // t
    kern = functools.partial(_da_kernel, t=t, lam_init=lam_init)
    return pl.pallas_call(
        kern,
        grid=(batch, DA_HEADS, nq),
        in_specs=[pl.BlockSpec(memory_space=pltpu.SMEM),
                  pl.BlockSpec((t, LANES), lambda b, h, i: (b * nq + i, h)),
                  pl.BlockSpec((seq, LANES), lambda b, h, i: (b, DA_HEADS + h)),
                  pl.BlockSpec((seq, LANES), lambda b, h, i: (b, 2 * DA_HEADS + h)),
                  pl.BlockSpec((3, None, t, t), lambda b, h, i: (0, h, 0, 0)),
                  pl.BlockSpec((1, DA_V_DIM), lambda b, h, i: (0, 0))],
        out_specs=pl.BlockSpec((t, DA_V_DIM), lambda b, h, i: (b * nq + i, h)),
        out_shape=jax.ShapeDtypeStruct((batch * seq, DA_HEADS * DA_V_DIM), BF16),
        scratch_shapes=[pltpu.VMEM((2 * t, LANES), F32), pltpu.VMEM((2 * t, 2 * DA_V_DIM), F32),
                        pltpu.VMEM((2, 2 * t, t), F32)],
        compiler_params=_cparams(3),
        name="diff_attention",
    )(lam, att, att, att, bias, subln_g)


def _compress_kernel(x_ref, pe_ref, w1_ref, w2_ref, o_ref):
    x = (x_ref[...].astype(F32) + pe_ref[...]).astype(BF16)
    hid = jax.nn.gelu(jnp.dot(x, w1_ref[...], preferred_element_type=F32))
    o_ref[...] = jnp.dot(hid.astype(BF16), w2_ref[...], preferred_element_type=F32).astype(o_ref.dtype)


def _compress(t, pe, w1, w2, batch, seq):
    g, d = NSA_GROUPS, NSA_DIM
    r = CMP_BLOCK // CMP_STRIDE
    nch = seq // CMP_STRIDE
    nc = nch - r + 1
    ch = t.reshape(batch, nch, CMP_STRIDE, g, d)
    blocks = jnp.concatenate([ch[:, j:j + nc] for j in range(r)], axis=2)
    flat = blocks.transpose(0, 1, 3, 2, 4).reshape(batch * nc * g, CMP_BLOCK * d)
    rows = flat.shape[0]
    tm = 512
    rows_p = -(-rows // tm) * tm
    flat = jnp.pad(flat, ((0, rows_p - rows), (0, 0)))
    hidden = w1.shape[1]
    out = pl.pallas_call(
        _compress_kernel,
        grid=(rows_p // tm,),
        in_specs=[pl.BlockSpec((tm, CMP_BLOCK * d), lambda i: (i, 0)),
                  pl.BlockSpec((1, CMP_BLOCK * d), lambda i: (0, 0)),
                  pl.BlockSpec((CMP_BLOCK * d, hidden), lambda i: (0, 0)),
                  pl.BlockSpec((hidden, d), lambda i: (0, 0))],
        out_specs=pl.BlockSpec((tm, d), lambda i: (i, 0)),
        out_shape=jax.ShapeDtypeStruct((rows_p, d), BF16),
        compiler_params=_cparams(1),
        name="compress_mlp",
    )(flat, pe.reshape(1, CMP_BLOCK * d), w1.astype(BF16), w2.astype(BF16))
    out = out[:rows].reshape(batch, nc, g, d).transpose(2, 0, 1, 3)
    return jnp.pad(out, ((0, 0), (0, 0), (0, nch - nc), (0, 0)))


def _cmp_select_kernel(q_ref, kc_ref, vc_ref, ov_ref, gate_ref, o_ref, sel_ref, *, tq, k_sel):
    qi = pl.program_id(2)
    ncp = kc_ref.shape[0]
    nsp = sel_ref.shape[-1]
    q = q_ref[...].reshape(NSA_HPG * tq, NSA_DIM)
    q_pos = qi * tq + lax.broadcasted_iota(jnp.int32, (tq, 1), 0)
    c_end = lax.broadcasted_iota(jnp.int32, (1, ncp), 1) * CMP_STRIDE + (CMP_BLOCK - 1)
    cmask = (c_end <= q_pos)[None]
    s = _dot_nt(q, kc_ref[...]).reshape(NSA_HPG, tq, ncp)
    s = jnp.where(cmask, s, NEG)
    m = jnp.max(s, axis=-1, keepdims=True)
    p = jnp.where(cmask, jnp.exp(s - m), 0.0)
    p = p / jnp.maximum(jnp.sum(p, axis=-1, keepdims=True), 1e-30)
    o = jnp.dot(p.reshape(NSA_HPG * tq, ncp).astype(BF16), vc_ref[...], preferred_element_type=F32)
    o_ref[...] = o.reshape(NSA_HPG, tq, NSA_DIM) * _head_gates(gate_ref, pl.program_id(0), 0)

    psum = p[0]
    for h in range(1, NSA_HPG):
        psum = psum + p[h]
    imp = _dot_exact01(psum, ov_ref[...])
    blk = lax.broadcasted_iota(jnp.int32, (1, nsp), 1)
    cur = q_pos // SLC_BLOCK
    valid = blk <= cur
    forced = valid & ((blk == 0) | (blk > cur - N_LOCAL_BLOCKS))
    score = jnp.where(forced, FORCE_SCORE, jnp.where(valid, imp, NEG))
    blk_f = blk.astype(F32)
    sel = jnp.zeros((tq, nsp), F32)
    for _ in range(k_sel):
        mx = jnp.max(score, axis=-1, keepdims=True)
        idx = jnp.min(jnp.where(score == mx, blk_f, float(nsp)), axis=-1, keepdims=True)
        hit = blk_f == idx
        sel = jnp.where(hit, 1.0, sel)
        score = jnp.where(hit, -jnp.inf, score)
    sel_ref[...] = sel.astype(sel_ref.dtype)


def _cmp_select(q, kc, vc, overlap, gate, batch, seq, k_sel):
    tq = CMP_TQ
    assert seq % tq == 0
    nq = seq // tq
    ncp = kc.shape[2]
    nsp = overlap.shape[1]
    kern = functools.partial(_cmp_select_kernel, tq=tq, k_sel=k_sel)
    return pl.pallas_call(
        kern,
        grid=(NSA_GROUPS, batch, nq),
        in_specs=[pl.BlockSpec((NSA_HPG, tq, NSA_DIM), lambda g, b, i: (g, b * nq + i, 0)),
                  pl.BlockSpec((None, None, ncp, NSA_DIM), lambda g, b, i: (g, b, 0, 0)),
                  pl.BlockSpec((None, None, ncp, NSA_DIM), lambda g, b, i: (g, b, 0, 0)),
                  pl.BlockSpec((ncp, nsp), lambda g, b, i: (0, 0)),
                  pl.BlockSpec((tq, gate.shape[1]), lambda g, b, i: (b * nq + i, 0))],
        out_specs=[pl.BlockSpec((NSA_HPG, tq, NSA_DIM), lambda g, b, i: (g, b * nq + i, 0)),
                   pl.BlockSpec((None, tq, nsp), lambda g, b, i: (g, b * nq + i, 0))],
        out_shape=[jax.ShapeDtypeStruct((NSA_HEADS, batch * seq, NSA_DIM), F32),
                   jax.ShapeDtypeStruct((NSA_GROUPS, batch * seq, nsp), BF16)],
        compiler_params=_cparams(3),
        name="nsa_cmp_select",
    )(q, kc, vc, overlap, gate)


def _window_kernel(q_ref, k_ref, v_ref, bias_ref, gate_ref, o_ref, *, tq, span):
    qi = pl.program_id(2)
    q0 = pl.multiple_of(qi * tq, tq)
    q = q_ref[...].reshape(NSA_HPG * tq, NSA_DIM)
    ks = k_ref[pl.ds(q0, span), :]
    v1 = v_ref[pl.ds(q0, span), :]
    s = _dot_nt(q, ks).reshape(NSA_HPG, tq, span) + bias_ref[...]
    in_seq = (lax.broadcasted_iota(jnp.int32, (1, 1, span), 2) + q0) >= WINDOW
    s = jnp.where(in_seq, s, NEG)
    m = jnp.max(s, axis=-1, keepdims=True)
    p = jnp.exp(s - m).reshape(NSA_HPG * tq, span).astype(BF16)
    o = jnp.dot(p, v1, preferred_element_type=F32)
    o = (o[:, :NSA_DIM] / o[:, NSA_DIM:]).reshape(NSA_HPG, tq, NSA_DIM)
    o_ref[...] = o * _head_gates(gate_ref, pl.program_id(0), 2)


def _window_attention(q, kp, vp, bias, gate, batch, seq):
    tq = WIN_TQ
    assert seq % tq == 0
    span = WINDOW + tq
    nq = seq // tq
    kern = functools.partial(_window_kernel, tq=tq, span=span)
    return pl.pallas_call(
        kern,
        grid=(NSA_GROUPS, batch, nq),
        in_specs=[pl.BlockSpec((NSA_HPG, tq, NSA_DIM), lambda g, b, i: (g, b * nq + i, 0)),
                  pl.BlockSpec((None, None, WINDOW + seq, NSA_DIM), lambda g, b, i: (g, b, 0, 0)),
                  pl.BlockSpec((None, None, WINDOW + seq, 2 * NSA_DIM), lambda g, b, i: (g, b, 0, 0)),
                  pl.BlockSpec((NSA_HPG, tq, span), lambda g, b, i: (g, 0, 0)),
                  pl.BlockSpec((tq, gate.shape[1]), lambda g, b, i: (b * nq + i, 0))],
        out_specs=pl.BlockSpec((NSA_HPG, tq, NSA_DIM), lambda g, b, i: (g, b * nq + i, 0)),
        out_shape=jax.ShapeDtypeStruct((NSA_HEADS, batch * seq, NSA_DIM), F32),
        compiler_params=_cparams(3),
        name="nsa_window",
    )(q, kp, vp, bias, gate)


def _slc_kernel(q_ref, k_ref, v_ref, sel_ref, bias_ref, gate_ref, oc_ref, ow_ref, o_ref,
                m_scr, acc_scr, s_scr, *, t):
    g = pl.program_id(0)
    qi = pl.program_id(2)
    nsp = sel_ref.shape[-1]
    rows = NSA_HPG * t
    q = q_ref[...].reshape(rows, NSA_DIM)
    sel = sel_ref[...]
    _flash_init(m_scr, acc_scr)
    blk_row = lax.broadcasted_iota(jnp.int32, (nsp, t), 0)
    key_blk = lax.broadcasted_iota(jnp.int32, (nsp, t), 1) // SLC_BLOCK

    def logits(j):
        return _dot_nt(q, k_ref[pl.ds(pl.multiple_of(j * t, t), t), :])

    def selection_mask(j):
        expand = jnp.where(blk_row == key_blk + j * (t // SLC_BLOCK), 1.0, 0.0).astype(BF16)
        return (jnp.dot(sel, expand, preferred_element_type=F32) - 1.0) * (-NEG)

    def values(j):
        return v_ref[pl.ds(pl.multiple_of(j * t, t), t), :]

    def consume_far(j, slot):
        s = s_scr[slot].reshape(NSA_HPG, t, t) + selection_mask(j)[None]
        _flash_update(s.reshape(rows, t), values(j), m_scr, acc_scr, exp_dtype=BF16)

    def consume(j, slot):
        s = s_scr[slot].reshape(NSA_HPG, t, t) + (selection_mask(j)[None] + bias_ref[_near_kind(j, qi)])
        _flash_update(s.reshape(rows, t), values(j), m_scr, acc_scr, exp_dtype=BF16)

    _pipelined_sweep(qi + 1, jnp.maximum(qi - 1, 0), logits, consume_far, consume, s_scr)

    acc = acc_scr[...]
    o = (acc[:, :NSA_DIM] / acc[:, NSA_DIM:]).reshape(NSA_HPG, t, NSA_DIM)
    o = oc_ref[...] + ow_ref[...] + o * _head_gates(gate_ref, g, 1)
    o_ref[...] = o.astype(o_ref.dtype)


def _slc_attention(q, ks, vs1, sel, bias, gate, o_cmp, o_win, batch, seq):
    t = SLC_TILE
    assert seq % t == 0 and t >= REL_MAX_DIST and t % SLC_BLOCK == 0
    nq = seq // t
    nsp = sel.shape[-1]
    kern = functools.partial(_slc_kernel, t=t)
    head_blk = lambda last: pl.BlockSpec((NSA_HPG, t, last), lambda g, b, i: (g, b * nq + i, 0))
    return pl.pallas_call(
        kern,
        grid=(NSA_GROUPS, batch, nq),
        in_specs=[head_blk(NSA_DIM),
                  pl.BlockSpec((None, None, seq, NSA_DIM), lambda g, b, i: (g, b, 0, 0)),
                  pl.BlockSpec((None, None, seq, 2 * NSA_DIM), lambda g, b, i: (g, b, 0, 0)),
                  pl.BlockSpec((None, t, nsp), lambda g, b, i: (g, b * nq + i, 0)),
                  pl.BlockSpec((3, NSA_HPG, t, t), lambda g, b, i: (0, g, 0, 0), pipeline_mode=pl.Buffered(1)),
                  pl.BlockSpec((t, gate.shape[1]), lambda g, b, i: (b * nq + i, 0)),
                  head_blk(NSA_DIM), head_blk(NSA_DIM)],
        out_specs=head_blk(NSA_DIM),
        out_shape=jax.ShapeDtypeStruct((NSA_HEADS, batch * seq, NSA_DIM), BF16),
        scratch_shapes=[pltpu.VMEM((NSA_HPG * t, LANES), F32), pltpu.VMEM((NSA_HPG * t, 2 * NSA_DIM), F32),
                        pltpu.VMEM((2, NSA_HPG * t, t), F32)],
        compiler_params=_cparams(3),
        name="nsa_selected",
    )(q, ks, vs1, sel, bias, gate, o_cmp, o_win)


def _merge_kernel(x_ref, oda_ref, onsa_ref, wgda_ref, wgnsa_ref, wda_ref, wnsa_ref, o_ref):
    x = x_ref[...]
    gate_da = jnp.dot(x, wgda_ref[...], preferred_element_type=F32)
    gate_nsa = jnp.dot(x, wgnsa_ref[...], preferred_element_type=F32)
    a = jnp.dot(oda_ref[...], wda_ref[...], preferred_element_type=F32)
    n = jnp.dot(onsa_ref[...], wnsa_ref[...], preferred_element_type=F32)
    mixed = jax.nn.sigmoid(gate_da) * a + jax.nn.sigmoid(gate_nsa) * n
    o_ref[...] = mixed.astype(o_ref.dtype)


def _merge(xb, o_da, o_nsa, w_gates, w_da, w_nsa):
    tokens, kd = o_da.shape
    d = w_da.shape[1]
    tm, tn = 512, 512
    nb = d // tn
    return pl.pallas_call(
        _merge_kernel,
        grid=(tokens // tm, nb),
        in_specs=[pl.BlockSpec((tm, d), lambda i, j: (i, 0)),
                  pl.BlockSpec((tm, kd), lambda i, j: (i, 0)),
                  pl.BlockSpec((tm, kd), lambda i, j: (i, 0)),
                  pl.BlockSpec((d, tn), lambda i, j: (0, j)),
                  pl.BlockSpec((d, tn), lambda i, j: (0, nb + j)),
                  pl.BlockSpec((kd, tn), lambda i, j: (0, j)),
                  pl.BlockSpec((kd, tn), lambda i, j: (0, j))],
        out_specs=pl.BlockSpec((tm, tn), lambda i, j: (i, j)),
        out_shape=jax.ShapeDtypeStruct((tokens, d), BF16),
        compiler_params=_cparams(2),
        name="gated_merge",
    )(xb, o_da, o_nsa, w_gates, w_gates, w_da, w_nsa)


def _layer_norm(z, g, b):
    mu = jnp.mean(z, axis=-1, keepdims=True)
    zc = z - mu
    var = jnp.mean(zc * zc, axis=-1, keepdims=True)
    return (zc * lax.rsqrt(var + LN_EPS)) * g + b


def _outproj_ln_kernel(mixed_ref, w_ref, x_ref, g_ref, b_ref, wq_ref, h_ref, pq_ref):
    z = DN_ALPHA * x_ref[...] + jnp.dot(mixed_ref[...], w_ref[...], preferred_element_type=F32)
    h = _layer_norm(z, g_ref[...], b_ref[...])
    h_ref[...] = h
    pq_ref[...] = jnp.dot(h.astype(BF16), wq_ref[...], preferred_element_type=F32).astype(pq_ref.dtype)


def _outproj_ln(mixed, w_out, x, g, b, w_q):
    tokens, d = x.shape
    nq = w_q.shape[1]
    tm = 256
    row = pl.BlockSpec((tm, d), lambda i: (i, 0))
    vec = pl.BlockSpec((1, d), lambda i: (0, 0))
    resident = lambda shape: pl.BlockSpec(shape, lambda i: (0, 0), pipeline_mode=pl.Buffered(1))
    return pl.pallas_call(
        _outproj_ln_kernel,
        grid=(tokens // tm,),
        in_specs=[row, resident((d, d)), row, vec, vec, resident((d, nq))],
        out_specs=[row, pl.BlockSpec((tm, nq), lambda i: (i, 0))],
        out_shape=[jax.ShapeDtypeStruct((tokens, d), F32), jax.ShapeDtypeStruct((tokens, nq), BF16)],
        compiler_params=_cparams(1),
        name="outproj_ln1",
    )(mixed, w_out, x, g, b, w_q)


def _topk_rows(x, k, ids=None):
    n, t = x.shape
    if ids is None:
        ids = lax.broadcasted_iota(jnp.int32, (n, t), 0).astype(F32)
    slot = lax.broadcasted_iota(jnp.int32, (k, t), 0)
    vals = jnp.zeros((k, t), F32)
    idxs = jnp.zeros((k, t), F32)
    for r in range(k):
        mx = jnp.max(x, axis=0, keepdims=True)
        idx = jnp.min(jnp.where(x == mx, ids, jnp.inf), axis=0, keepdims=True)
        vals = jnp.where(slot == r, mx, vals)
        idxs = jnp.where(slot == r, idx, idxs)
        x = jnp.where(ids == idx, -jnp.inf, x)
    return vals, idxs


def _pair_candidates(v1, i1, v2, i2, nk):
    k, t = v1.shape
    sums, ids, eids = [], [], []
    for c in range(k):
        top = k // (c + 1) - 1
        for fixed_second in (True, False):
            lo = c if fixed_second else c + 1
            if lo > top:
                continue
            n = 8 if top < 8 else k
            run = lax.broadcasted_iota(jnp.int32, (n, t), 0)
            keep = (run >= lo) & (run <= top)
            runf = run.astype(F32)
            if fixed_second:
                val, pid = v1[:n] + v2[c:c + 1], runf * float(k) + float(c)
                eid = i1[:n] * float(nk) + i2[c:c + 1]
            else:
                val, pid = v1[c:c + 1] + v2[:n], float(c * k) + runf
                eid = i1[c:c + 1] * float(nk) + i2[:n]
            sums.append(jnp.where(keep, val, -jnp.inf))
            ids.append(pid)
            eids.append(eid)
    return jnp.concatenate(sums, axis=0), jnp.concatenate(ids, axis=0), jnp.concatenate(eids, axis=0)


def _peer_topk_kernel(q_ref, sk_ref, gate_ref, eid_ref, *, tm):
    k = PEER_TOPK
    nk = PEER_NKEYS
    for h in range(PEER_HEADS):
        qh = q_ref[:, h * LANES:(h + 1) * LANES]
        st = _dot_nt(sk_ref[...], qh)
        v1, i1 = _topk_rows(st[:nk], k)
        v2, i2 = _topk_rows(st[nk:], k)
        cand, ids, cid = _pair_candidates(v1, i1, v2, i2, nk)
        sc, j = _topk_rows(cand, k, ids)
        slot = lax.broadcasted_iota(jnp.int32, (k, tm), 0)
        eid = jnp.zeros((k, tm), F32)
        for r in range(k):
            picked = jnp.max(jnp.where(ids == j[r:r + 1], cid, -1.0), axis=0, keepdims=True)
            eid = jnp.where(slot == r, picked, eid)
        e = jnp.exp(sc - sc[0:1])
        gate_ref[h] = e / jnp.sum(e, axis=0, keepdims=True)
        eid_ref[h] = eid.astype(jnp.int32)


def _peer_topk(q, sk):
    tokens = q.shape[0]
    tm = 256
    kern = functools.partial(_peer_topk_kernel, tm=tm)
    out_blk = pl.BlockSpec((PEER_HEADS, PEER_TOPK, tm), lambda i: (0, 0, i))
    return pl.pallas_call(
        kern,
        grid=(tokens // tm,),
        in_specs=[pl.BlockSpec((tm, PEER_HEADS * LANES), lambda i: (i, 0)),
                  pl.BlockSpec((2 * PEER_NKEYS, LANES), lambda i: (0, 0))],
        out_specs=[out_blk, out_blk],
        out_shape=[jax.ShapeDtypeStruct((PEER_HEADS, PEER_TOPK, tokens), F32),
                   jax.ShapeDtypeStruct((PEER_HEADS, PEER_TOPK, tokens), jnp.int32)],
        compiler_params=_cparams(1),
        name="peer_topk",
    )(q, sk)


def _pack_bf16_pairs(t):
    bits = lax.bitcast_convert_type(t.astype(BF16), jnp.uint16).astype(jnp.uint32)
    half = t.shape[1] // 2
    return bits[:, :half] | (bits[:, half:] << 16)


def _unpack_bf16_pairs(w):
    return pltpu.bitcast(w << 16, F32), pltpu.bitcast(w & jnp.uint32(0xFFFF0000), F32)


def _peer_gather_kernel(eid_hbm, uv_hbm, gate_ref, h_ref, g_ref, b_ref, o_ref, idx_smem, acc, idx_sem, row_sem,
                        *rows, tt, d, tokens):
    i = pl.program_id(0)
    nsteps = pl.num_programs(0)
    nsel = PEER_HEADS * PEER_TOPK
    ns = len(rows)
    per = tt * nsel
    half = d // 2
    nchunk = half // LANES

    def idx_copy(step):
        dst = idx_smem.at[pl.ds(pl.multiple_of((step % 2) * per, per), per)]
        return pltpu.make_async_copy(eid_hbm.at[step], dst, idx_sem.at[step % 2])

    def issue(gtok, slot, first=0, count=None):
        base = (jnp.minimum(gtok, tokens - 1) % (2 * tt)) * nsel
        for j in range(first, first + (nsel if count is None else count)):
            e = idx_smem[base + j]
            pltpu.make_async_copy(uv_hbm.at[e], rows[slot].at[:, j, :], row_sem.at[slot]).start(priority=j % PEER_DMA_QUEUES)

    def wait_slot(slot):
        pltpu.make_async_copy(rows[slot], rows[slot], row_sem.at[slot]).wait()

    @pl.when(i == 0)
    def _():
        first = idx_copy(0)
        first.start()
        first.wait()
        for s in range(ns - 1):
            issue(s, s)

    @pl.when(i + 1 < nsteps)
    def _():
        idx_copy(i + 1).start()

    lane = lax.broadcasted_iota(jnp.int32, (nsel, tt), 1)

    per_chunk = nsel // (2 * nchunk)

    def compute_and_issue(tok, slot, ahead, ahead_slot):
        buf = rows[slot]
        y = h_ref[pl.ds(tok, 1), :]
        part = None
        for c in range(nchunk):
            issue(ahead, ahead_slot, c * per_chunk, per_chunk)
            lo, hi = _unpack_bf16_pairs(buf[c])
            term = lo * y[:, c * LANES:(c + 1) * LANES] + hi * y[:, half + c * LANES:half + (c + 1) * LANES]
            part = term if part is None else part + term
        act = jax.nn.gelu(jnp.sum(part, axis=-1, keepdims=True))
        gcol = jnp.sum(jnp.where(lane == tok, gate_ref[...], 0.0), axis=-1, keepdims=True)
        w = gcol * act
        out_lo, out_hi = [], []
        for c in range(nchunk):
            issue(ahead, ahead_slot, (nchunk + c) * per_chunk, per_chunk)
            lo, hi = _unpack_bf16_pairs(buf[nchunk + c])
            out_lo.append(jnp.sum(w * lo, axis=0, keepdims=True))
            out_hi.append(jnp.sum(w * hi, axis=0, keepdims=True))
        acc[pl.ds(tok, 1), :] = jnp.concatenate(out_lo + out_hi, axis=-1)

    n_groups = tt // ns

    def body(grp, carry):
        @pl.when((grp == n_groups - 1) & (i + 1 < nsteps))
        def _():
            idx_copy(i + 1).wait()

        for s in range(ns):
            tok = grp * ns + s
            wait_slot(s)
            compute_and_issue(tok, s, i * tt + tok + ns - 1, (s + ns - 1) % ns)
        return carry

    lax.fori_loop(0, n_groups, body, 0)

    @pl.when(i == nsteps - 1)
    def _():
        for s in range(ns - 1):
            wait_slot((tt + s) % ns)

    z = DN_ALPHA * h_ref[...] + acc[...]
    o_ref[...] = _layer_norm(z, g_ref[...], b_ref[...])


def _peer_gather(eid, uv, gate, h, g, b):
    tokens, d = h.shape
    tt = PEER_TT
    nsel = PEER_HEADS * PEER_TOPK
    assert tokens % tt == 0 and tt % PEER_SLOTS == 0 and nsel % (d // LANES) == 0
    assert uv.shape[1:] == (d // LANES, LANES) and eid.shape == (tokens // tt, tt * nsel)
    kern = functools.partial(_peer_gather_kernel, tt=tt, d=d, tokens=tokens)
    vec = pl.BlockSpec((1, d), lambda i: (0, 0))
    return pl.pallas_call(
        kern,
        grid=(tokens // tt,),
        in_specs=[pl.BlockSpec(memory_space=pl.ANY),
                  pl.BlockSpec(memory_space=pl.ANY),
                  pl.BlockSpec((nsel, tt), lambda i: (0, i)),
                  pl.BlockSpec((tt, d), lambda i: (i, 0)),
                  vec, vec],
        out_specs=pl.BlockSpec((tt, d), lambda i: (i, 0)),
        out_shape=jax.ShapeDtypeStruct((tokens, d), F32),
        scratch_shapes=[pltpu.SMEM((2 * tt * nsel,), jnp.int32),
                        pltpu.VMEM((tt, d), F32),
                        pltpu.SemaphoreType.DMA((2,)),
                        pltpu.SemaphoreType.DMA((PEER_SLOTS,))]
                       + [pltpu.VMEM((d // LANES, nsel, LANES), jnp.uint32) for _ in range(PEER_SLOTS)],
        compiler_params=_cparams(1),
        name="peer_gather_ln2",
    )(eid, uv, gate, h, g, b)


def _heads_major(t, heads, width):
    return t.reshape(t.shape[0], heads, width).transpose(1, 0, 2)


def kernel(x, w_in, da_lam_q, da_lam_k, da_subln_g, cmp_pe_k, cmp_w1_k, cmp_w2_k, cmp_pe_v, cmp_w1_v,
           cmp_w2_v, w_branch_da, w_branch_nsa, w_out, ln1_g, ln1_b, peer_wq, peer_subkey1, peer_subkey2,
           peer_u, peer_v, ln2_g, ln2_b, rel_bias):
    batch, seq, d_model = x.shape
    tokens = batch * seq
    g, hd = NSA_GROUPS, NSA_DIM
    table = rel_bias.astype(F32)
    xs = x.reshape(tokens, d_model)
    for l in range(DEPTH):
        lam_init = 0.8 - 0.6 * math.exp(-0.3 * l)
        xb = xs.astype(BF16)
        w = w_in[l]
        scale = DA_HEAD_DIM ** -0.5
        c_daq, c_dak, c_dav, c_nq = 0, 1024, 2048, 3072
        c_kv, c_gate, c_mg, c_end = 4096, 5632, 5680, 9776
        w_att = jnp.concatenate([w[:, c_daq:c_dak] * scale, w[:, c_dak:c_nq], w[:, c_nq:c_kv] * scale],
                                axis=1).astype(BF16)
        att = _matmul(xb, w_att, BF16, 1024, 512)
        kv = _matmul(xb, w[:, c_kv:c_gate].astype(BF16), F32, 512, 512)
        w_gate = jnp.pad(w[:, c_gate:c_mg], ((0, 0), (0, LANES - (c_mg - c_gate)))).astype(BF16)
        br_gate = _matmul(xb, w_gate, F32, 512, LANES)[:, :c_mg - c_gate]

        lam_e = jnp.exp(jnp.sum(da_lam_q[l].astype(F32) * da_lam_k[l].astype(F32), -1))
        lam = (lam_e[0] - lam_e[1] + lam_init).reshape(1)
        table_rel = table - table[REL_BUCKETS - 1]
        o_da = _diff_attention(att, lam, _causal_bias_tiles(table_rel[:, :DA_HEADS], DA_TILE),
                               da_subln_g[l].reshape(1, DA_V_DIM), batch, seq, lam_init)

        q_n = _heads_major(att[:, 3072:4096], NSA_HEADS, hd)
        kv6 = kv.reshape(tokens, 6, g * hd)
        kc = _compress(kv6[:, 0], cmp_pe_k[l], cmp_w1_k[l], cmp_w2_k[l], batch, seq)
        vc = _compress(kv6[:, 1], cmp_pe_v[l], cmp_w1_v[l], cmp_w2_v[l], batch, seq)
        grp = lambda t: t.astype(BF16).reshape(batch, seq, g, hd).transpose(2, 0, 1, 3)
        k_s, v_s, k_w, v_w = grp(kv6[:, 2]), grp(kv6[:, 3]), grp(kv6[:, 4]), grp(kv6[:, 5])

        ncp = seq // CMP_STRIDE
        ns = seq // SLC_BLOCK
        nsp = -(-ns // LANES) * LANES
        k_sel = min(SLC_TOPK, ns)
        cmp_start = jnp.arange(ncp) * CMP_STRIDE
        slc_start = jnp.arange(nsp) * SLC_BLOCK
        overlap = ((cmp_start[:, None] <= slc_start[None, :] + SLC_BLOCK - 1)
                   & (cmp_start[:, None] + CMP_BLOCK - 1 >= slc_start[None, :])).astype(BF16)
        o_cmp, sel = _cmp_select(q_n, kc, vc, overlap, br_gate, batch, seq, k_sel)

        win_bias = _window_bias_tile(table[:, DA_HEADS:], WIN_TQ, WINDOW + WIN_TQ)
        pad_w = lambda t: jnp.pad(t, ((0, 0), (0, 0), (WINDOW, 0), (0, 0)))
        with_ones = lambda t: jnp.concatenate([t, jnp.ones_like(t)], axis=-1)
        o_win = _window_attention(q_n, pad_w(k_w), pad_w(with_ones(v_w)), win_bias, br_gate, batch, seq)

        v_s1 = jnp.concatenate([v_s, jnp.ones_like(v_s)], axis=-1)
        o_nsa = _slc_attention(q_n, k_s, v_s1, sel, _causal_bias_tiles(table_rel[:, DA_HEADS:], SLC_TILE),
                               br_gate, o_cmp, o_win, batch, seq)
        o_nsa = o_nsa.transpose(1, 0, 2).reshape(tokens, NSA_HEADS * hd)

        mixed = _merge(xb, o_da, o_nsa, w[:, c_mg:c_end].astype(BF16),
                       w_branch_da[l].astype(BF16), w_branch_nsa[l].astype(BF16))
        h, pq = _outproj_ln(mixed, w_out[l].astype(BF16), xs, ln1_g[l].reshape(1, -1), ln1_b[l].reshape(1, -1),
                            peer_wq[l].astype(BF16))

        half = peer_subkey1.shape[-1]
        zeros = jnp.zeros((PEER_NKEYS, half), F32)
        sk = jnp.concatenate([jnp.concatenate([peer_subkey1[l], zeros], axis=1),
                              jnp.concatenate([zeros, peer_subkey2[l]], axis=1)], axis=0).astype(BF16)
        gate, eid = _peer_topk(pq, sk)
        nsel = PEER_HEADS * PEER_TOPK
        eid_tok = eid.reshape(nsel, tokens).T.reshape(tokens // PEER_TT, PEER_TT * nsel)
        uv = jnp.concatenate([_pack_bf16_pairs(peer_u[l]), _pack_bf16_pairs(peer_v[l])], axis=1)
        uv = uv.reshape(uv.shape[0], d_model // LANES, LANES)
        xs = _peer_gather(eid_tok, uv, gate.reshape(nsel, tokens), h,
                          ln2_g[l].reshape(1, -1), ln2_b[l].reshape(1, -1))
    return xs.reshape(batch, seq, d_model)
```
